```python
import jax, jax.numpy as jnp
from jax import lax
import numpy as np

D_MODEL = 1024
BATCH = 2
SEQ = 16384
DEPTH = 1
DEC_BATCH = 128
DEC_SEQ = 4
PAST_LEN = 8192
PAGE_SIZE = 128

N_HEADS = 8
HEAD_DIM = 64
ATTN_WIDTH = N_HEADS * HEAD_DIM
POOL_WINDOWS = (2, 4, 8, 16)
N_POOL_GROUPS = 4
POOL_GROUP_WIDTH = 128
POOL_WIDTH = N_POOL_GROUPS * POOL_GROUP_WIDTH
POOL_OUT_GROUP = D_MODEL // N_POOL_GROUPS
POOL_STATE_LEN = 15
D_FF = 2816
PLE_DIM = 256
Q_BLOCK = 128
RMS_EPS = 1e-6
NEG_INF = -1e30
FORGET_BIAS_INIT = 3.0
IN_WIDTH = POOL_WIDTH + 3 * ATTN_WIDTH + N_HEADS + 2 * D_MODEL
IN_SPLITS = (POOL_WIDTH,
             POOL_WIDTH + ATTN_WIDTH,
             POOL_WIDTH + 2 * ATTN_WIDTH,
             POOL_WIDTH + 3 * ATTN_WIDTH,
             POOL_WIDTH + 3 * ATTN_WIDTH + N_HEADS,
             POOL_WIDTH + 3 * ATTN_WIDTH + N_HEADS + D_MODEL)

kernel_name = 'fox_pool_macaron_hybrid_step'


def rmsnorm(x, g):
    xf = x.astype(jnp.float32)
    r = lax.rsqrt(jnp.mean(xf * xf, axis=-1, keepdims=True) + RMS_EPS)
    return (xf * r * g.astype(jnp.float32)).astype(x.dtype)


def swiglu(x, w_gate, w_up, w_down):
    return (jax.nn.silu(x @ w_gate) * (x @ w_up)) @ w_down


def causal_pool(z):
    L = z.shape[1]
    zf = z.astype(jnp.float32)
    cs = jnp.cumsum(zf, axis=1)
    count = jnp.arange(1, L + 1, dtype=jnp.float32)[None, :, None]
    outs = []
    for g, w in enumerate(POOL_WINDOWS):
        sl = slice(g * POOL_GROUP_WIDTH, (g + 1) * POOL_GROUP_WIDTH)
        csg = cs[..., sl]
        lag = jnp.pad(csg, ((0, 0), (w, 0), (0, 0)))[:, :L]
        outs.append((csg - lag) / jnp.minimum(count, float(w)) - zf[..., sl])
    return jnp.stack(outs, axis=2).astype(z.dtype)


def fox_attention(q, f_q, q_pos, segments):
    B, T, H, Dh = q.shape
    blk = min(Q_BLOCK, T)
    n_blk = -(-T // blk)
    pad = n_blk * blk - T
    qp = jnp.pad(q, ((0, 0), (0, pad), (0, 0), (0, 0)))
    fqp = jnp.pad(f_q, ((0, 0), (0, pad), (0, 0)), mode='edge')
    posp = jnp.pad(q_pos, (0, pad), mode='edge')
    q_blocks = qp.reshape(B, n_blk, blk, H, Dh).transpose(1, 0, 2, 3, 4)
    fq_blocks = fqp.reshape(B, n_blk, blk, H).transpose(1, 0, 3, 2)
    pos_blocks = posp.reshape(n_blk, blk)
    segs = [(k, v, jnp.swapaxes(f_k, 1, 2), k_pos) for (k, v, f_k, k_pos) in segments]
    scale = HEAD_DIM ** -0.5

    def block(args):
        qb, fqb, pb = args
        logits = []
        for k, v, fk, kp in segs:
            s = jnp.einsum('bqhd,bkhd->bhqk', qb, k, preferred_element_type=jnp.float32) * scale
            s = s + fqb[..., :, None] - fk[:, :, None, :]
            s = jnp.where(kp[None, None, None, :] <= pb[None, None, :, None], s, NEG_INF)
            logits.append(s)
        w = jax.nn.softmax(jnp.concatenate(logits, axis=-1), axis=-1)
        parts = []
        start = 0
        for k, v, _, _ in segs:
            n = k.shape[1]
            parts.append(jnp.einsum('bhqk,bkhd->bqhd', w[..., start:start + n].astype(v.dtype), v,
                                    preferred_element_type=jnp.float32))
            start += n
        return sum(parts)

    out = lax.map(block, (q_blocks, fq_blocks, pos_blocks))
    out = out.transpose(1, 0, 2, 3, 4).reshape(B, n_blk * blk, H, Dh)[:, :T]
    return out.astype(q.dtype)


def run_layer(x, p, lw, pool_prefix=None, past=None):
    B, T, _ = x.shape
    h = x + 0.5 * swiglu(rmsnorm(x, lw['g_ffn1']), lw['w_ffn1_gate'], lw['w_ffn1_up'], lw['w_ffn1_down'])
    u = rmsnorm(h, lw['g_mix'])
    proj = u @ lw['w_in']
    z_pool, q, k, v, f_logit, gate_pool, gate_attn = jnp.split(proj, IN_SPLITS, axis=-1)
    if pool_prefix is None:
        ext = z_pool
        n_pre = 0
    else:
        ext = jnp.concatenate([pool_prefix.astype(z_pool.dtype), z_pool], axis=1)
        n_pre = pool_prefix.shape[1]
    pooled = causal_pool(ext)[:, n_pre:]
    new_pool = ext[:, -POOL_STATE_LEN:]
    branch_pool = jnp.einsum('btgc,gcd->btgd', pooled, lw['w_pool_group']).reshape(B, T, D_MODEL)
    branch_pool = branch_pool * lw['pool_scale']
    q = q.reshape(B, T, N_HEADS, HEAD_DIM)
    k = k.reshape(B, T, N_HEADS, HEAD_DIM)
    v = v.reshape(B, T, N_HEADS, HEAD_DIM)
    log_f = jax.nn.log_sigmoid(f_logit.astype(jnp.float32) + lw['b_forget'].astype(jnp.float32))
    if past is None:
        F = jnp.cumsum(log_f, axis=1)
        q_pos = jnp.arange(T, dtype=jnp.int32)
        f_q = F
        segments = [(k, v, F, q_pos)]
    else:
        k_past, v_past, logf_past = past
        P = k_past.shape[1]
        F_all = jnp.cumsum(jnp.concatenate([logf_past.astype(jnp.float32), log_f], axis=1), axis=1)
        f_past, f_q = F_all[:, :P], F_all[:, P:]
        q_pos = P + jnp.arange(T, dtype=jnp.int32)
        segments = [(k_past, v_past, f_past, jnp.arange(P, dtype=jnp.int32)), (k, v, f_q, q_pos)]
    attn = fox_attention(q, f_q, q_pos, segments).reshape(B, T, ATTN_WIDTH)
    branch_attn = attn @ lw['w_attn_branch']
    merged = jax.nn.sigmoid(gate_pool) * branch_pool + jax.nn.sigmoid(gate_attn) * branch_attn
    h = h + merged @ lw['w_out']
    h = h + 0.5 * swiglu(rmsnorm(h, lw['g_ffn2']), lw['w_ffn2_gate'], lw['w_ffn2_up'], lw['w_ffn2_down'])
    ple = (p.astype(h.dtype) @ lw['w_ple']) * jax.nn.sigmoid(rmsnorm(h, lw['g_ple']) @ lw['w_ple_gate'])
    h = h + ple
    return h, (k, v, log_f, new_pool)


def setup_inputs(seed: int = 0) -> dict:
    key = jax.random.key(seed)
    ks = jax.random.split(key, 32)
    n_pages = PAST_LEN // PAGE_SIZE
    n_pool_pages = (5 * DEC_BATCH * n_pages) // 4

    def nrm(k, shape, scale):
        return jax.random.normal(k, shape, jnp.float32) * scale

    page_table = jax.random.permutation(ks[0], n_pool_pages)[:DEC_BATCH * n_pages]
    page_table = page_table.reshape(DEC_BATCH, n_pages).astype(jnp.int32)
    return {
        'x_prompt': nrm(ks[1], (BATCH, SEQ, D_MODEL), 1.0),
        'x_sample': nrm(ks[2], (DEC_BATCH, DEC_SEQ, D_MODEL), 1.0),
        'cache_k': nrm(ks[3], (DEPTH, n_pool_pages, PAGE_SIZE, N_HEADS, HEAD_DIM), 1.0),
        'cache_v': nrm(ks[4], (DEPTH, n_pool_pages, PAGE_SIZE, N_HEADS, HEAD_DIM), 1.0),
        'cache_logf': jax.nn.log_sigmoid(FORGET_BIAS_INIT + nrm(ks[5], (DEPTH, n_pool_pages, PAGE_SIZE, N_HEADS), 1.0)),
        'state_pool': nrm(ks[6], (DEPTH, DEC_BATCH, POOL_STATE_LEN, POOL_WIDTH), 1.0),
        'page_table': page_table,
        'p_prompt': nrm(ks[7], (DEPTH, BATCH, SEQ, PLE_DIM), 1.0),
        'p_sample': nrm(ks[8], (DEPTH, DEC_BATCH, DEC_SEQ, PLE_DIM), 1.0),
        'g_ffn1': 1.0 + nrm(ks[9], (DEPTH, D_MODEL), 0.05),
        'w_ffn1_gate': nrm(ks[10], (DEPTH, D_MODEL, D_FF), D_MODEL ** -0.5),
        'w_ffn1_up': nrm(ks[11], (DEPTH, D_MODEL, D_FF), D_MODEL ** -0.5),
        'w_ffn1_down': nrm(ks[12], (DEPTH, D_FF, D_MODEL), D_FF ** -0.5),
        'g_mix': 1.0 + nrm(ks[13], (DEPTH, D_MODEL), 0.05),
        'w_in': nrm(ks[14], (DEPTH, D_MODEL, IN_WIDTH), D_MODEL ** -0.5),
        'b_forget': FORGET_BIAS_INIT + nrm(ks[15], (DEPTH, N_HEADS), 0.5),
        'w_pool_group': nrm(ks[16], (DEPTH, N_POOL_GROUPS, POOL_GROUP_WIDTH, POOL_OUT_GROUP), POOL_GROUP_WIDTH ** -0.5),
        'pool_scale': 1.0 + nrm(ks[17], (DEPTH, D_MODEL), 0.1),
        'w_attn_branch': nrm(ks[18], (DEPTH, ATTN_WIDTH, D_MODEL), ATTN_WIDTH ** -0.5),
        'w_out': nrm(ks[19], (DEPTH, D_MODEL, D_MODEL), D_MODEL ** -0.5),
        'g_ffn2': 1.0 + nrm(ks[20], (DEPTH, D_MODEL), 0.05),
        'w_ffn2_gate': nrm(ks[21], (DEPTH, D_MODEL, D_FF), D_MODEL ** -0.5),
        'w_ffn2_up': nrm(ks[22], (DEPTH, D_MODEL, D_FF), D_MODEL ** -0.5),
        'w_ffn2_down': nrm(ks[23], (DEPTH, D_FF, D_MODEL), D_FF ** -0.5),
        'g_ple': 1.0 + nrm(ks[24], (DEPTH, D_MODEL), 0.05),
        'w_ple_gate': nrm(ks[25], (DEPTH, D_MODEL, D_MODEL), D_MODEL ** -0.5),
        'w_ple': nrm(ks[26], (DEPTH, PLE_DIM, D_MODEL), PLE_DIM ** -0.5),
        'g_final': 1.0 + nrm(ks[27], (D_MODEL,), 0.05),
    }


def reference(x_prompt, x_sample, cache_k, cache_v, cache_logf, state_pool, page_table,
              p_prompt, p_sample, g_ffn1, w_ffn1_gate, w_ffn1_up, w_ffn1_down, g_mix, w_in,
              b_forget, w_pool_group, pool_scale, w_attn_branch, w_out, g_ffn2, w_ffn2_gate,
              w_ffn2_up, w_ffn2_down, g_ple, w_ple_gate, w_ple, g_final):
    dec_b, n_pages = page_table.shape
    past_len = n_pages * cache_k.shape[2]
    h_p, h_s = x_prompt, x_sample
    kp_l, vp_l, fp_l, sp_l, ks_l, vs_l, fs_l, ss_l = [], [], [], [], [], [], [], []
    for i in range(DEPTH):
        lw = {
            'g_ffn1': g_ffn1[i], 'w_ffn1_gate': w_ffn1_gate[i], 'w_ffn1_up': w_ffn1_up[i],
            'w_ffn1_down': w_ffn1_down[i], 'g_mix': g_mix[i], 'w_in': w_in[i],
            'b_forget': b_forget[i], 'w_pool_group': w_pool_group[i], 'pool_scale': pool_scale[i],
            'w_attn_branch': w_attn_branch[i], 'w_out': w_out[i], 'g_ffn2': g_ffn2[i],
            'w_ffn2_gate': w_ffn2_gate[i], 'w_ffn2_up': w_ffn2_up[i], 'w_ffn2_down': w_ffn2_down[i],
            'g_ple': g_ple[i], 'w_ple_gate': w_ple_gate[i], 'w_ple': w_ple[i],
        }
        h_p, (kp, vp, fp, sp) = run_layer(h_p, p_prompt[i], lw)
        k_past = cache_k[i][page_table].reshape(dec_b, past_len, N_HEADS, HEAD_DIM)
        v_past = cache_v[i][page_table].reshape(dec_b, past_len, N_HEADS, HEAD_DIM)
        f_past = cache_logf[i][page_table].reshape(dec_b, past_len, N_HEADS)
        h_s, (k_s, v_s, f_s, s_s) = run_layer(h_s, p_sample[i], lw, state_pool[i], (k_past, v_past, f_past))
        kp_l.append(kp); vp_l.append(vp); fp_l.append(fp); sp_l.append(sp)
        ks_l.append(k_s); vs_l.append(v_s); fs_l.append(f_s); ss_l.append(s_s)
    y_prompt = rmsnorm(h_p, g_final)
    y_sample = rmsnorm(h_s, g_final)
    return (y_prompt, y_sample,
            jnp.stack(kp_l), jnp.stack(vp_l), jnp.stack(fp_l), jnp.stack(sp_l),
            jnp.stack(ks_l), jnp.stack(vs_l), jnp.stack(fs_l), jnp.stack(ss_l))
```

```python
import functools

import numpy as np
import jax
import jax.numpy as jnp
from jax import lax
from jax.experimental import pallas as pl
from jax.experimental.pallas import tpu as pltpu

F32 = jnp.float32
BF16 = jnp.bfloat16

RMS_EPS = 1e-6
MASKED = -1e30
POOL_WINDOWS = (2, 4, 8, 16)
POOL_HALO = 16

LANES = 128
HEAD_PAD = 128
V_ROWS = 80
F_PARTS = 3
FF_CHUNK = 768
TOKEN_TILE = 512
ATTN_TILE = 512
DECODE_PAGES = 8
VMEM_LIMIT = 56 * 1024 * 1024

_NT = (((1,), (1,)), ((), ()))
_TN = (((0,), (0,)), ((), ()))


def _rms(x, g):
    r = lax.rsqrt(jnp.mean(x * x, axis=-1, keepdims=True) + RMS_EPS)
    return x * r * g


def _dot(a, b):
    return jnp.dot(a, b, preferred_element_type=F32)


def _chunks(n, c):
    return [(s, min(c, n - s)) for s in range(0, n, c)]


def _split3(x):
    hi = x.astype(BF16).astype(F32)
    r = x - hi
    mid = r.astype(BF16).astype(F32)
    lo = (r - mid).astype(BF16).astype(F32)
    return hi, mid, lo


def _pack3(x, n):
    hi, mid, lo = _split3(x)
    return (hi + pltpu.roll(mid, n, 1) + pltpu.roll(lo, 2 * n, 1)).astype(BF16)


def _unpack3(c, n):
    return c + pltpu.roll(c, LANES - n, 1) + pltpu.roll(c, LANES - 2 * n, 1)


def _log_sigmoid(x):
    return jnp.minimum(x, 0.0) - jnp.log1p(jnp.exp(-jnp.abs(x)))


def _const_spec(shape):
    nd = len(shape)
    return pl.BlockSpec(shape, lambda *_: (0,) * nd, pipeline_mode=pl.Buffered(1))


def _params(sem):
    return pltpu.CompilerParams(dimension_semantics=sem, vmem_limit_bytes=VMEM_LIMIT)


def _ffn_value(x, g, wg_ref, wu_ref, wd_ref):
    u = _rms(x, g).astype(BF16)
    acc = None
    for s, n in _chunks(wg_ref.shape[1], FF_CHUNK):
        gate = _dot(u, wg_ref[:, s:s + n])
        up = _dot(u, wu_ref[:, s:s + n])
        a = (gate * jax.nn.sigmoid(gate) * up).astype(BF16)
        d = _dot(a, wd_ref[s:s + n, :])
        acc = d if acc is None else acc + d
    return x + 0.5 * acc


def _ffn_kernel(x_ref, g_ref, wg_ref, wu_ref, wd_ref, o_ref):
    o_ref[...] = _ffn_value(x_ref[...], g_ref[...], wg_ref, wu_ref, wd_ref)


def _ffn(x, g, wg, wu, wd, tm):
    n, d = x.shape
    return pl.pallas_call(
        _ffn_kernel,
        grid=(n // tm,),
        in_specs=[pl.BlockSpec((tm, d), lambda i: (i, 0)),
                  _const_spec(g.shape), _const_spec(wg.shape), _const_spec(wu.shape),
                  _const_spec(wd.shape)],
        out_specs=pl.BlockSpec((tm, d), lambda i: (i, 0)),
        out_shape=jax.ShapeDtypeStruct((n, d), F32),
        compiler_params=_params(("parallel",)),
        name="ffn",
    )(x, g, wg, wu, wd)


def _pool_branch(pooled_groups, wpg_ref):
    outs = [_dot(p.astype(BF16), wpg_ref[g]) for g, p in enumerate(pooled_groups)]
    return jnp.concatenate(outs, axis=1)


def _head_lane_mask(shape, n_heads):
    lane = lax.broadcasted_iota(jnp.int32, shape, 1)
    return lane < n_heads


def _proj_prompt_kernel(h_ref, g_ref, wpool_ref, wqa_ref, wka_ref, wk_ref, wv_ref, wvt_ref,
                        wf_ref, bf_ref, wgp_ref, wga_ref, wpg_ref, pscale_ref, place_ref,
                        tri_ref, qones_ref,
                        qa_ref, ka_ref, vt_ref, k_ref, v_ref, lf_ref, gpool_ref, sga_ref,
                        zlast_ref, zext_ref, fcarry_ref, *, n_heads):
    i = pl.program_id(1)
    tm = h_ref.shape[0]
    gw = LANES

    @pl.when(i == 0)
    def _():
        zext_ref[0:POOL_HALO, :] = jnp.zeros((POOL_HALO, zext_ref.shape[1]), F32)
        fcarry_ref[...] = jnp.zeros(fcarry_ref.shape, F32)

    u = _rms(h_ref[...], g_ref[...]).astype(BF16)

    z = _dot(u, wpool_ref[...])
    zext_ref[POOL_HALO:POOL_HALO + tm, :] = z
    row = lax.broadcasted_iota(jnp.int32, (tm, gw), 0) + i * tm
    pooled = []
    for g, w in enumerate(POOL_WINDOWS):
        sl = slice(g * gw, (g + 1) * gw)
        zg = z[:, sl]
        acc = zg
        for j in range(1, w):
            acc = acc + zext_ref[POOL_HALO - j:POOL_HALO - j + tm, sl]
        cnt = jnp.minimum(row + 1, w).astype(F32)
        pooled.append(acc / cnt - zg)
    zlast_ref[...] = zext_ref[tm:tm + POOL_HALO, :]
    zext_ref[0:POOL_HALO, :] = zext_ref[tm:tm + POOL_HALO, :]
    branch_pool = _pool_branch(pooled, wpg_ref) * pscale_ref[...]
    gpool_ref[...] = (jax.nn.sigmoid(_dot(u, wgp_ref[...])) * branch_pool).astype(BF16)
    sga_ref[...] = jax.nn.sigmoid(_dot(u, wga_ref[...])).astype(BF16)

    lane_ok = _head_lane_mask((tm, LANES), n_heads)
    logf = jnp.where(lane_ok, _log_sigmoid(_dot(u, wf_ref[...]) + bf_ref[...]), 0.0)
    lf_ref[...] = logf[:, 0:n_heads]
    csum = _unpack3(_dot(tri_ref[...], _pack3(logf, n_heads)), n_heads)
    fcum = jnp.where(lane_ok, csum + fcarry_ref[...], 0.0)
    fcarry_ref[...] = fcum[tm - 1:tm, :]

    qa = (_dot(u, wqa_ref[...]) + qones_ref[...]).astype(BF16)
    ka = (_dot(u, wka_ref[...]) + _dot(_pack3(fcum, n_heads), place_ref[...])).astype(BF16)
    vt = lax.dot_general(wvt_ref[...], u, _NT, preferred_element_type=F32)
    vrow = lax.broadcasted_iota(jnp.int32, vt.shape, 0)
    head_dim = wk_ref.shape[1] // n_heads
    vt = jnp.where(vrow % V_ROWS == head_dim, 1.0, vt).astype(BF16)
    for h in range(n_heads):
        qa_ref[h] = qa[:, h * HEAD_PAD:(h + 1) * HEAD_PAD]
        ka_ref[h] = ka[:, h * HEAD_PAD:(h + 1) * HEAD_PAD]
        vt_ref[h] = vt[h * V_ROWS:(h + 1) * V_ROWS, :]
    k_ref[...] = _dot(u, wk_ref[...])
    v_ref[...] = _dot(u, wv_ref[...])


def _proj_prompt(h, w, tm, n_heads):
    b, t, d = h.shape
    aw = w["wk"].shape[1]
    pw = w["wpool"].shape[1]
    names = ["g_mix", "wpool", "wqa", "wka", "wk", "wv", "wvt", "wf", "bf", "wgp", "wga", "wpg",
             "pscale", "place", "tri", "qones"]
    consts = [w[n] for n in names]
    tok = lambda width: pl.BlockSpec((None, tm, width), lambda bi, i: (bi, i, 0))
    head = pl.BlockSpec((None, n_heads, tm, HEAD_PAD), lambda bi, i: (bi, 0, i, 0))
    out_shape = (
        jax.ShapeDtypeStruct((b, n_heads, t, HEAD_PAD), BF16),
        jax.ShapeDtypeStruct((b, n_heads, t, HEAD_PAD), BF16),
        jax.ShapeDtypeStruct((b, n_heads, V_ROWS, t), BF16),
        jax.ShapeDtypeStruct((b, t, aw), F32),
        jax.ShapeDtypeStruct((b, t, aw), F32),
        jax.ShapeDtypeStruct((b, t, n_heads), F32),
        jax.ShapeDtypeStruct((b, t, d), BF16),
        jax.ShapeDtypeStruct((b, t, d), BF16),
        jax.ShapeDtypeStruct((b, POOL_HALO, pw), F32),
    )
    out_specs = (
        head, head,
        pl.BlockSpec((None, n_heads, V_ROWS, tm), lambda bi, i: (bi, 0, 0, i)),
        tok(aw), tok(aw), tok(n_heads), tok(d), tok(d),
        pl.BlockSpec((None, POOL_HALO, pw), lambda bi, i: (bi, 0, 0)),
    )
    return pl.pallas_call(
        functools.partial(_proj_prompt_kernel, n_heads=n_heads),
        grid=(b, t // tm),
        in_specs=[tok(d)] + [_const_spec(c.shape) for c in consts],
        out_specs=out_specs,
        out_shape=out_shape,
        scratch_shapes=[pltpu.VMEM((tm + POOL_HALO, pw), F32), pltpu.VMEM((1, LANES), F32)],
        compiler_params=_params(("arbitrary", "arbitrary")),
        name="proj_prompt",
    )(h, *consts)


def _proj_sample_kernel(h_ref, state_ref, g_ref, wpool_ref, wq_ref, wk_ref, wv_ref, wf_ref,
                        bf_ref, wgp_ref, wga_ref, wpg_ref, pscale_ref,
                        z_ref, q_ref, k_ref, v_ref, lf_ref, fc_ref, gpool_ref, sga_ref,
                        *, n_heads, n_new, n_seq):
    gw = LANES
    n_pre = state_ref.shape[0] // n_seq
    u = _rms(h_ref[...], g_ref[...]).astype(BF16)

    z = _dot(u, wpool_ref[...])
    z_ref[...] = z

    def ext_rows(r, sl):
        if r < n_pre:
            return state_ref[r * n_seq:(r + 1) * n_seq, sl]
        return z[(r - n_pre) * n_seq:(r - n_pre + 1) * n_seq, sl]

    pooled = []
    for g, w in enumerate(POOL_WINDOWS):
        sl = slice(g * gw, (g + 1) * gw)
        steps = []
        for s in range(n_new):
            r = n_pre + s
            acc = ext_rows(r, sl)
            for j in range(1, w):
                if r - j >= 0:
                    acc = acc + ext_rows(r - j, sl)
            steps.append(acc / float(min(r + 1, w)) - ext_rows(r, sl))
        pooled.append(jnp.concatenate(steps, axis=0))
    branch_pool = _pool_branch(pooled, wpg_ref) * pscale_ref[...]
    gpool_ref[...] = (jax.nn.sigmoid(_dot(u, wgp_ref[...])) * branch_pool).astype(BF16)
    sga_ref[...] = jax.nn.sigmoid(_dot(u, wga_ref[...])).astype(BF16)

    tm = h_ref.shape[0]
    lane_ok = _head_lane_mask((tm, LANES), n_heads)
    logf = jnp.where(lane_ok, _log_sigmoid(_dot(u, wf_ref[...]) + bf_ref[...]), 0.0)
    lf_ref[...] = logf[:, 0:n_heads]
    run = None
    sums = []
    for s in range(n_new):
        blk = logf[s * n_seq:(s + 1) * n_seq, :]
        run = blk if run is None else run + blk
        sums.append(run)
    fc_ref[...] = jnp.concatenate(sums, axis=0)[:, 0:n_heads]

    q_ref[...] = _dot(u, wq_ref[...]).astype(BF16)
    k_ref[...] = _dot(u, wk_ref[...])
    v_ref[...] = _dot(u, wv_ref[...])


def _proj_sample(h, state, w, n_heads, n_new, n_seq):
    n, d = h.shape
    aw = w["wk"].shape[1]
    pw = w["wpool"].shape[1]
    names = ["g_mix", "wpool", "wq", "wk", "wv", "wf", "bf", "wgp", "wga", "wpg", "pscale"]
    consts = [w[n_] for n_ in names]
    full = lambda shape: pl.BlockSpec(shape, lambda i: (0,) * len(shape))
    out_shape = (
        jax.ShapeDtypeStruct((n, pw), F32),
        jax.ShapeDtypeStruct((n, aw), BF16),
        jax.ShapeDtypeStruct((n, aw), F32),
        jax.ShapeDtypeStruct((n, aw), F32),
        jax.ShapeDtypeStruct((n, n_heads), F32),
        jax.ShapeDtypeStruct((n, n_heads), F32),
        jax.ShapeDtypeStruct((n, d), BF16),
        jax.ShapeDtypeStruct((n, d), BF16),
    )
    return pl.pallas_call(
        functools.partial(_proj_sample_kernel, n_heads=n_heads, n_new=n_new, n_seq=n_seq),
        grid=(1,),
        in_specs=[full(h.shape), full(state.shape)] + [_const_spec(c.shape) for c in consts],
        out_specs=tuple(full(s.shape) for s in out_shape),
        out_shape=out_shape,
        compiler_params=_params(("arbitrary",)),
        name="proj_sample",
    )(h, state, *consts)


def _attn_kernel(q_ref, k_ref, vt_ref, o_ref, *, head_dim, tk):
    i = pl.program_id(2)
    tq = q_ref.shape[0]
    per_q = tq // tk
    q = q_ref[...]

    def tile(j, m, acc, masked):
        start = pl.multiple_of(j * tk, tk)
        s = lax.dot_general(k_ref[pl.ds(start, tk), :], q, _NT, preferred_element_type=F32)
        if masked:
            kpos = lax.broadcasted_iota(jnp.int32, (tk, tq), 0) + j * tk
            qpos = lax.broadcasted_iota(jnp.int32, (tk, tq), 1) + i * tq
            s = jnp.where(kpos <= qpos, s, MASKED)
        m_new = jnp.maximum(m, jnp.max(s, axis=0, keepdims=True))
        alpha = jnp.exp(m - m_new)
        p = jnp.exp(s - m_new).astype(BF16)
        acc = alpha * acc + _dot(vt_ref[:, pl.ds(start, tk)], p)
        return m_new, acc

    m = jnp.full((1, tq), MASKED, F32)
    acc = jnp.zeros((V_ROWS, tq), F32)
    m, acc = lax.fori_loop(0, i * per_q, lambda j, c: tile(j, c[0], c[1], False), (m, acc))
    for d in range(per_q):
        m, acc = tile(i * per_q + d, m, acc, True)
    o_ref[...] = (acc[0:head_dim, :] / acc[head_dim:head_dim + 1, :]).astype(BF16)


def _attention(qa, ka, vt, head_dim, tq, tk):
    b, nh, t, _ = qa.shape
    return pl.pallas_call(
        functools.partial(_attn_kernel, head_dim=head_dim, tk=tk),
        grid=(b, nh, t // tq),
        in_specs=[pl.BlockSpec((None, None, tq, HEAD_PAD), lambda bi, h, i: (bi, h, i, 0)),
                  pl.BlockSpec((None, None, t, HEAD_PAD), lambda bi, h, i: (bi, h, 0, 0)),
                  pl.BlockSpec((None, None, V_ROWS, t), lambda bi, h, i: (bi, h, 0, 0))],
        out_specs=pl.BlockSpec((None, head_dim, tq), lambda bi, h, i: (bi, h, i)),
        out_shape=jax.ShapeDtypeStruct((b, nh * head_dim, t), BF16),
        compiler_params=_params(("parallel", "parallel", "arbitrary")),
        name="attn_prompt",
    )(qa, ka, vt)


def _decode_kernel(pt_ref, q_ref, kn_ref, vn_ref, negc_ref, upper_ref, *refs, n_heads, n_new):
    del pt_ref
    pp = DECODE_PAGES
    kp, vp, lp = refs[0:pp], refs[pp:2 * pp], refs[2 * pp:3 * pp]
    o_ref, kb_ref, vb_ref, m_ref, l_ref, acc_ref, fcar_ref = refs[3 * pp:]
    g = pl.program_id(1)
    page = kp[0].shape[0]
    width = kp[0].shape[1]
    head_dim = width // n_heads
    rows = n_new * n_heads

    @pl.when(g == 0)
    def _():
        m_ref[...] = jnp.full(m_ref.shape, MASKED, F32)
        l_ref[...] = jnp.zeros(l_ref.shape, F32)
        acc_ref[...] = jnp.zeros(acc_ref.shape, F32)
        fcar_ref[...] = jnp.zeros(fcar_ref.shape, F32)

    q = q_ref[...].astype(F32)
    qrows = jnp.concatenate([jnp.broadcast_to(q[s:s + 1, :], (n_heads, width)) for s in range(n_new)],
                            axis=0)
    rid = lax.broadcasted_iota(jnp.int32, (rows, width), 0)
    cid = lax.broadcasted_iota(jnp.int32, (rows, width), 1)
    own = (cid // head_dim) == (rid % n_heads)
    qbd = jnp.where(own, qrows, 0.0).astype(BF16)

    def update(s, vals):
        m_new = jnp.maximum(m_ref[...], jnp.max(s, axis=1, keepdims=True))
        alpha = jnp.exp(m_ref[...] - m_new)
        p = jnp.exp(s - m_new)
        l_ref[...] = alpha * l_ref[...] + jnp.sum(p, axis=1, keepdims=True)
        acc_ref[...] = alpha * acc_ref[...] + _dot(p.astype(BF16), vals)
        m_ref[...] = m_new

    for i in range(pp):
        kb_ref[i * page:(i + 1) * page, :] = kp[i][...].astype(BF16)
        vb_ref[i * page:(i + 1) * page, :] = vp[i][...].astype(BF16)

    x = jnp.concatenate([lp[i][...] for i in range(pp)], axis=0)
    parts = jnp.concatenate(_split3(x), axis=0).astype(BF16)
    c = _dot(parts, upper_ref[...])
    nr = pp * n_heads
    fin = c[0:nr] + c[nr:2 * nr] + c[2 * nr:3 * nr]
    carry = fcar_ref[...]
    biases = []
    for i in range(pp):
        fi = fin[i * n_heads:(i + 1) * n_heads, :]
        biases.append(jnp.concatenate([-(fi + carry)] * n_new, axis=0))
        carry = carry + jnp.broadcast_to(fi[:, page - 1:page], carry.shape)
    fcar_ref[...] = carry
    s = lax.dot_general(qbd, kb_ref[...], _NT, preferred_element_type=F32)
    update(s + jnp.concatenate(biases, axis=1), vb_ref[...])

    @pl.when(g == pl.num_programs(1) - 1)
    def _():
        sn = lax.dot_general(qbd, kn_ref[...], _NT, preferred_element_type=F32)
        bias = jnp.concatenate([negc_ref[...] - carry] * n_new, axis=0)
        r2 = lax.broadcasted_iota(jnp.int32, sn.shape, 0)
        c2 = lax.broadcasted_iota(jnp.int32, sn.shape, 1)
        ok = (c2 < n_new) & (c2 <= r2 // n_heads)
        update(jnp.where(ok, sn + bias, MASKED), vn_ref[...])
        out = jnp.where(own, acc_ref[...] / l_ref[...], 0.0)
        o_ref[...] = jnp.sum(out.reshape(n_new, n_heads, width), axis=1)


def _decode(page_table, q, kn, vn, negc, upper, cache_k, cache_v, cache_lft, n_heads):
    bd, n_new, width = q.shape
    n_pages = page_table.shape[1]
    page = cache_k.shape[1]
    pp = DECODE_PAGES
    rows = n_new * n_heads
    pt = page_table.reshape(-1)

    def page_map(i, b, g, pt_ref):
        return (pt_ref[b * n_pages + g * pp + i], 0, 0)

    seq = lambda shape: pl.BlockSpec((None,) + shape, lambda b, g, pt_ref: (b, 0, 0))
    in_specs = [seq((n_new, width)), seq((page, width)), seq((page, width)), seq((n_heads, LANES)),
                pl.BlockSpec(upper.shape, lambda b, g, pt_ref: (0, 0))]
    in_specs += [pl.BlockSpec((None, page, width), functools.partial(page_map, i)) for i in range(pp)]
    in_specs += [pl.BlockSpec((None, page, width), functools.partial(page_map, i)) for i in range(pp)]
    in_specs += [pl.BlockSpec((None, n_heads, page), functools.partial(page_map, i)) for i in range(pp)]
    grid_spec = pltpu.PrefetchScalarGridSpec(
        num_scalar_prefetch=1,
        grid=(bd, n_pages // pp),
        in_specs=in_specs,
        out_specs=seq((n_new, width)),
        scratch_shapes=[pltpu.VMEM((pp * page, width), BF16), pltpu.VMEM((pp * page, width), BF16),
                        pltpu.VMEM((rows, 1), F32), pltpu.VMEM((rows, 1), F32),
                        pltpu.VMEM((rows, width), F32), pltpu.VMEM((n_heads, LANES), F32)],
    )
    return pl.pallas_call(
        functools.partial(_decode_kernel, n_heads=n_heads, n_new=n_new),
        grid_spec=grid_spec,
        out_shape=jax.ShapeDtypeStruct((bd, n_new, width), F32),
        compiler_params=_params(("parallel", "arbitrary")),
        name="attn_decode",
    )(pt, q, kn, vn, negc, upper, *([cache_k] * pp), *([cache_v] * pp), *([cache_lft] * pp))


def _post_kernel(h_ref, gpool_ref, sga_ref, attn_ref, p_ref, wab_ref, wout_ref, g2_ref, wg_ref,
                 wu_ref, wd_ref, gple_ref, wpg_ref, wple_ref, gfin_ref, y_ref, *, attn_transposed):
    if attn_transposed:
        branch_attn = lax.dot_general(attn_ref[...], wab_ref[...], _TN, preferred_element_type=F32)
    else:
        branch_attn = _dot(attn_ref[...].astype(BF16), wab_ref[...])
    merged = gpool_ref[...].astype(F32) + sga_ref[...].astype(F32) * branch_attn
    h = h_ref[...] + _dot(merged.astype(BF16), wout_ref[...])
    h = _ffn_value(h, g2_ref[...], wg_ref, wu_ref, wd_ref)
    gate = jax.nn.sigmoid(_dot(_rms(h, gple_ref[...]).astype(BF16), wpg_ref[...]))
    h = h + _dot(p_ref[...].astype(BF16), wple_ref[...]) * gate
    y_ref[...] = _rms(h, gfin_ref[...])


def _post(h, gpool, sga, attn, p, w, tm, attn_transposed):
    b, t, d = h.shape
    names = ["wab", "wout", "g_ffn2", "wg2", "wu2", "wd2", "g_ple", "wpgate", "wple", "g_final"]
    consts = [w[n] for n in names]
    tok = lambda width: pl.BlockSpec((None, tm, width), lambda bi, i: (bi, i, 0))
    if attn_transposed:
        attn_spec = pl.BlockSpec((None, attn.shape[1], tm), lambda bi, i: (bi, 0, i))
    else:
        attn_spec = tok(attn.shape[2])
    return pl.pallas_call(
        functools.partial(_post_kernel, attn_transposed=attn_transposed),
        grid=(b, t // tm),
        in_specs=[tok(d), tok(d), tok(d), attn_spec, tok(p.shape[2])]
                 + [_const_spec(c.shape) for c in consts],
        out_specs=tok(d),
        out_shape=jax.ShapeDtypeStruct((b, t, d), F32),
        compiler_params=_params(("parallel", "parallel")),
        name="post",
    )(h, gpool, sga, attn, p, *consts)


def _prep_weights(g_ffn1, w_ffn1_gate, w_ffn1_up, w_ffn1_down, g_mix, w_in, b_forget, w_pool_group,
                  pool_scale, w_attn_branch, w_out, g_ffn2, w_ffn2_gate, w_ffn2_up, w_ffn2_down,
                  g_ple, w_ple_gate, w_ple, g_final, n_heads, head_dim, pool_width, tm):
    d = w_in.shape[0]
    aw = n_heads * head_dim
    o = 0
    wpool = w_in[:, o:o + pool_width]; o += pool_width
    wq = w_in[:, o:o + aw] * (head_dim ** -0.5); o += aw
    wk = w_in[:, o:o + aw]; o += aw
    wv = w_in[:, o:o + aw]; o += aw
    wf = w_in[:, o:o + n_heads]; o += n_heads
    wgp = w_in[:, o:o + d]; o += d
    wga = w_in[:, o:o + d]

    def per_head_pad(x):
        x = x.reshape(d, n_heads, head_dim)
        return jnp.pad(x, ((0, 0), (0, 0), (0, HEAD_PAD - head_dim))).reshape(d, n_heads * HEAD_PAD)

    wvt = jnp.pad(wv.T.reshape(n_heads, head_dim, d), ((0, 0), (0, V_ROWS - head_dim), (0, 0)))
    place = np.zeros((LANES, n_heads * HEAD_PAD), np.float32)
    qones = np.zeros((1, n_heads * HEAD_PAD), np.float32)
    for h in range(n_heads):
        for part in range(F_PARTS):
            place[part * n_heads + h, h * HEAD_PAD + head_dim + part] = -1.0
            qones[0, h * HEAD_PAD + head_dim + part] = 1.0
    row = lambda x: x.reshape(1, -1).astype(F32)
    bf = lambda x: x.astype(BF16)
    return {
        "g_ffn1": row(g_ffn1), "wg1": bf(w_ffn1_gate), "wu1": bf(w_ffn1_up), "wd1": bf(w_ffn1_down),
        "g_mix": row(g_mix), "wpool": bf(wpool), "wq": bf(wq), "wk": bf(wk), "wv": bf(wv),
        "wqa": bf(per_head_pad(wq)), "wka": bf(per_head_pad(wk)),
        "wvt": bf(wvt.reshape(n_heads * V_ROWS, d)),
        "wf": bf(jnp.pad(wf, ((0, 0), (0, LANES - n_heads)))),
        "bf": jnp.pad(row(b_forget), ((0, 0), (0, LANES - n_heads))),
        "wgp": bf(wgp), "wga": bf(wga), "wpg": bf(w_pool_group), "pscale": row(pool_scale),
        "place": jnp.asarray(place, BF16), "qones": jnp.asarray(qones, F32),
        "tri": jnp.asarray(np.tril(np.ones((tm, tm), np.float32)), BF16),
        "wab": bf(w_attn_branch), "wout": bf(w_out),
        "g_ffn2": row(g_ffn2), "wg2": bf(w_ffn2_gate), "wu2": bf(w_ffn2_up), "wd2": bf(w_ffn2_down),
        "g_ple": row(g_ple), "wpgate": bf(w_ple_gate), "wple": bf(w_ple), "g_final": row(g_final),
    }


def kernel(x_prompt, x_sample, cache_k, cache_v, cache_logf, state_pool, page_table, p_prompt, p_sample, g_ffn1, w_ffn1_gate, w_ffn1_up, w_ffn1_down, g_mix, w_in, b_forget, w_pool_group, pool_scale, w_attn_branch, w_out, g_ffn2, w_ffn2_gate, w_ffn2_up, w_ffn2_down, g_ple, w_ple_gate, w_ple, g_final):
    depth = cache_k.shape[0]
    assert depth == 1, "one trunk layer"
    b, t, d = x_prompt.shape
    bd, n_new, _ = x_sample.shape
    _, n_pool_pages, page, n_heads, head_dim = cache_k.shape
    aw = n_heads * head_dim
    n_pre, pool_width = state_pool.shape[2], state_pool.shape[3]
    assert head_dim + F_PARTS <= HEAD_PAD and head_dim < V_ROWS and page == LANES
    assert pool_width == len(POOL_WINDOWS) * LANES and n_pre == POOL_WINDOWS[-1] - 1
    tm = min(TOKEN_TILE, t)
    ta = min(ATTN_TILE, t)
    ns = bd * n_new

    w = _prep_weights(g_ffn1[0], w_ffn1_gate[0], w_ffn1_up[0], w_ffn1_down[0], g_mix[0], w_in[0],
                      b_forget[0], w_pool_group[0], pool_scale[0], w_attn_branch[0], w_out[0],
                      g_ffn2[0], w_ffn2_gate[0], w_ffn2_up[0], w_ffn2_down[0], g_ple[0],
                      w_ple_gate[0], w_ple[0], g_final, n_heads, head_dim, pool_width, tm)

    h1 = _ffn(x_prompt.reshape(b * t, d), w["g_ffn1"], w["wg1"], w["wu1"], w["wd1"], tm)
    h1 = h1.reshape(b, t, d)
    qa, ka, vt, k_p, v_p, lf_p, gpool, sga, zlast = _proj_prompt(h1, w, tm, n_heads)
    attn_t = _attention(qa, ka, vt, head_dim, ta, ta)
    y_prompt = _post(h1, gpool, sga, attn_t, p_prompt[0], w, tm, True)

    step_major = lambda x: jnp.swapaxes(x, 0, 1).reshape(ns, x.shape[-1])
    seq_major = lambda x: jnp.swapaxes(x.reshape(n_new, bd, x.shape[-1]), 0, 1)
    hs1 = _ffn(step_major(x_sample), w["g_ffn1"], w["wg1"], w["wu1"], w["wd1"], ns)
    state = jnp.swapaxes(state_pool[0], 0, 1).reshape(n_pre * bd, pool_width)
    z_s, q_s, k_s, v_s, lf_s, fc_s, gpool_s, sga_s = _proj_sample(hs1, state, w, n_heads, n_new, bd)
    k_s, v_s, lf_s = seq_major(k_s), seq_major(v_s), seq_major(lf_s)
    pad_rows = lambda x: jnp.pad(x.astype(BF16), ((0, 0), (0, page - n_new), (0, 0)))
    negc = -jnp.swapaxes(seq_major(fc_s), 1, 2)
    negc = jnp.pad(negc, ((0, 0), (0, 0), (0, LANES - n_new)))
    upper = jnp.asarray(np.triu(np.ones((page, page), np.float32)), BF16)
    attn_s = _decode(page_table, seq_major(q_s), pad_rows(k_s), pad_rows(v_s), negc, upper,
                     cache_k[0].reshape(n_pool_pages, page, aw),
                     cache_v[0].reshape(n_pool_pages, page, aw),
                     jnp.swapaxes(cache_logf[0], 1, 2), n_heads)
    y_s = _post(hs1[None], gpool_s[None], sga_s[None], step_major(attn_s)[None],
                step_major(p_sample[0])[None], w, ns, False)
    y_sample = seq_major(y_s[0])
    pool_sample = jnp.concatenate([state_pool[0], seq_major(z_s)], axis=1)[:, -n_pre:]

    return (y_prompt, y_sample,
            k_p.reshape(1, b, t, n_heads, head_dim), v_p.reshape(1, b, t, n_heads, head_dim),
            lf_p[None], zlast[None, :, POOL_HALO - n_pre:],
            k_s.reshape(1, bd, n_new, n_heads, head_dim), v_s.reshape(1, bd, n_new, n_heads, head_dim),
            lf_s[None], pool_sample[None])
```

```python
import functools

import numpy as np
import jax
import jax.numpy as jnp
from jax import lax
from jax.experimental import pallas as pl
from jax.experimental.pallas import tpu as pltpu

F32 = jnp.float32
BF16 = jnp.bfloat16

RMS_EPS = 1e-6
MASKED = -1e30
POOL_WINDOWS = (2, 4, 8, 16)
POOL_HALO = 16

LANES = 128
HEAD_PAD = 128
V_ROWS = 80
F_PARTS = 3
FF_CHUNK = 768
TOKEN_TILE = 512
DECODE_PAGES = 8
ATTN_HEADS = 2
VMEM_LIMIT = 56 * 1024 * 1024
LOG2E = 1.4426950408889634
NORM_SLACK = 1.01
SKIP_MARGIN = 160.0

_NT = (((1,), (1,)), ((), ()))
_TN = (((0,), (0,)), ((), ()))


def _rms(x, g):
    r = lax.rsqrt(jnp.mean(x * x, axis=-1, keepdims=True) + RMS_EPS)
    return x * r * g


def _dot(a, b):
    return jnp.dot(a, b, preferred_element_type=F32)


def _chunks(n, c):
    return [(s, min(c, n - s)) for s in range(0, n, c)]


def _split3(x):
    hi = x.astype(BF16).astype(F32)
    r = x - hi
    mid = r.astype(BF16).astype(F32)
    lo = (r - mid).astype(BF16).astype(F32)
    return hi, mid, lo


def _pack3(x, n):
    hi, mid, lo = _split3(x)
    return (hi + pltpu.roll(mid, n, 1) + pltpu.roll(lo, 2 * n, 1)).astype(BF16)


def _unpack3(c, n):
    return c + pltpu.roll(c, LANES - n, 1) + pltpu.roll(c, LANES - 2 * n, 1)


def _log_sigmoid(x):
    return jnp.minimum(x, 0.0) - jnp.log1p(jnp.exp(-jnp.abs(x)))


def _const_spec(shape):
    nd = len(shape)
    return pl.BlockSpec(shape, lambda *_: (0,) * nd, pipeline_mode=pl.Buffered(1))


def _params(sem):
    return pltpu.CompilerParams(dimension_semantics=sem, vmem_limit_bytes=VMEM_LIMIT)


def _ffn_value(x, g, wg_ref, wu_ref, wd_ref):
    u = _rms(x, g).astype(BF16)
    acc = None
    for s, n in _chunks(wg_ref.shape[1], FF_CHUNK):
        gate = _dot(u, wg_ref[:, s:s + n])
        up = _dot(u, wu_ref[:, s:s + n])
        a = (gate * jax.nn.sigmoid(gate) * up).astype(BF16)
        d = _dot(a, wd_ref[s:s + n, :])
        acc = d if acc is None else acc + d
    return x + 0.5 * acc


def _ffn_kernel(x_ref, g_ref, wg_ref, wu_ref, wd_ref, o_ref):
    o_ref[...] = _ffn_value(x_ref[...], g_ref[...], wg_ref, wu_ref, wd_ref)


def _ffn(x, g, wg, wu, wd, tm):
    n, d = x.shape
    return pl.pallas_call(
        _ffn_kernel,
        grid=(n // tm,),
        in_specs=[pl.BlockSpec((tm, d), lambda i: (i, 0)),
                  _const_spec(g.shape), _const_spec(wg.shape), _const_spec(wu.shape),
                  _const_spec(wd.shape)],
        out_specs=pl.BlockSpec((tm, d), lambda i: (i, 0)),
        out_shape=jax.ShapeDtypeStruct((n, d), F32),
        compiler_params=_params(("parallel",)),
        name="ffn",
    )(x, g, wg, wu, wd)


def _pool_branch(pooled_groups, wpg_ref):
    outs = [_dot(p.astype(BF16), wpg_ref[g]) for g, p in enumerate(pooled_groups)]
    return jnp.concatenate(outs, axis=1)


def _head_lane_mask(shape, n_heads):
    lane = lax.broadcasted_iota(jnp.int32, shape, 1)
    return lane < n_heads


def _proj_prompt_kernel(h_ref, g_ref, wpool_ref, wqa_ref, wka_ref, wkt_ref, wvt_ref,
                        wf_ref, bf_ref, wgp_ref, wga_ref, wpg_ref, pscale_ref, place_ref,
                        tri_ref, qones_ref, headsel_ref,
                        qa_ref, ka_ref, vt_ref, kt_ref, vto_ref, lf_ref, gpool_ref, sga_ref,
                        zlast_ref, qn_ref, kn_ref, fe_ref, zext_ref, fcarry_ref, *, n_heads):
    i = pl.program_id(1)
    tm = h_ref.shape[0]
    gw = LANES

    @pl.when(i == 0)
    def _():
        zext_ref[0:POOL_HALO, :] = jnp.zeros((POOL_HALO, zext_ref.shape[1]), F32)
        fcarry_ref[...] = jnp.zeros(fcarry_ref.shape, F32)

    u = _rms(h_ref[...], g_ref[...]).astype(BF16)

    z = _dot(u, wpool_ref[...])
    zext_ref[POOL_HALO:POOL_HALO + tm, :] = z
    row = lax.broadcasted_iota(jnp.int32, (tm, gw), 0) + i * tm
    pooled = []
    for g, w in enumerate(POOL_WINDOWS):
        sl = slice(g * gw, (g + 1) * gw)
        zg = z[:, sl]
        acc = zg
        for j in range(1, w):
            acc = acc + zext_ref[POOL_HALO - j:POOL_HALO - j + tm, sl]
        cnt = jnp.minimum(row + 1, w).astype(F32)
        pooled.append(acc / cnt - zg)
    zlast_ref[...] = zext_ref[tm:tm + POOL_HALO, :]
    zext_ref[0:POOL_HALO, :] = zext_ref[tm:tm + POOL_HALO, :]
    branch_pool = _pool_branch(pooled, wpg_ref) * pscale_ref[...]
    gpool_ref[...] = (jax.nn.sigmoid(_dot(u, wgp_ref[...])) * branch_pool).astype(BF16)
    sga_ref[...] = jax.nn.sigmoid(_dot(u, wga_ref[...])).astype(BF16)

    lane_ok = _head_lane_mask((tm, LANES), n_heads)
    logf = jnp.where(lane_ok, _log_sigmoid(_dot(u, wf_ref[...]) + bf_ref[...]), 0.0)
    lf_ref[...] = logf[:, 0:n_heads]
    csum = _unpack3(_dot(tri_ref[...], _pack3(logf, n_heads)), n_heads)
    fcum = jnp.where(lane_ok, csum + fcarry_ref[...], 0.0)
    fcarry_ref[...] = fcum[tm - 1:tm, :]
    fcum2 = fcum * LOG2E
    fe_ref[...] = fcum2[tm - 1:tm, :]

    qa = (_dot(u, wqa_ref[...]) + qones_ref[...]).astype(BF16)
    ka = (_dot(u, wka_ref[...]) + _dot(_pack3(fcum2, n_heads), place_ref[...])).astype(BF16)
    for x, n_ref in ((qa, qn_ref), (ka, kn_ref)):
        xf = x.astype(F32)
        sq = (xf * xf * NORM_SLACK).astype(BF16)
        n_ref[...] = jnp.max(_dot(sq, headsel_ref[...]), axis=0, keepdims=True)
    kt = lax.dot_general(wkt_ref[...], u, _NT, preferred_element_type=F32)
    vt = lax.dot_general(wvt_ref[...], u, _NT, preferred_element_type=F32)
    kt_ref[...] = kt
    vto_ref[...] = vt
    head_dim = kt.shape[0] // n_heads
    rid = lax.broadcasted_iota(jnp.int32, (V_ROWS - head_dim, tm), 0)
    ones_rows = jnp.where(rid == 0, 1.0, 0.0).astype(BF16)
    for h in range(n_heads):
        qa_ref[h] = qa[:, h * HEAD_PAD:(h + 1) * HEAD_PAD]
        ka_ref[h] = ka[:, h * HEAD_PAD:(h + 1) * HEAD_PAD]
        vt_ref[h, 0:head_dim, :] = vt[h * head_dim:(h + 1) * head_dim, :].astype(BF16)
        vt_ref[h, head_dim:V_ROWS, :] = ones_rows


def _proj_prompt(h, w, tm, n_heads):
    b, t, d = h.shape
    aw = w["wkt"].shape[0]
    pw = w["wpool"].shape[1]
    nt = t // tm
    names = ["g_mix", "wpool", "wqa", "wka", "wkt", "wvt", "wf", "bf", "wgp", "wga", "wpg",
             "pscale", "place", "tri", "qones", "headsel"]
    consts = [w[n] for n in names]
    tok = lambda width: pl.BlockSpec((None, tm, width), lambda bi, i: (bi, i, 0))
    tok_t = lambda rows: pl.BlockSpec((None, rows, tm), lambda bi, i: (bi, 0, i))
    head = pl.BlockSpec((None, n_heads, tm, HEAD_PAD), lambda bi, i: (bi, 0, i, 0))
    stat = pl.BlockSpec((None, None, 1, LANES), lambda bi, i: (bi, i, 0, 0))
    stat_shape = jax.ShapeDtypeStruct((b, nt, 1, LANES), F32)
    out_shape = (
        jax.ShapeDtypeStruct((b, n_heads, t, HEAD_PAD), BF16),
        jax.ShapeDtypeStruct((b, n_heads, t, HEAD_PAD), BF16),
        jax.ShapeDtypeStruct((b, n_heads, V_ROWS, t), BF16),
        jax.ShapeDtypeStruct((b, aw, t), F32),
        jax.ShapeDtypeStruct((b, aw, t), F32),
        jax.ShapeDtypeStruct((b, t, n_heads), F32),
        jax.ShapeDtypeStruct((b, t, d), BF16),
        jax.ShapeDtypeStruct((b, t, d), BF16),
        jax.ShapeDtypeStruct((b, POOL_HALO, pw), F32),
        stat_shape, stat_shape, stat_shape,
    )
    out_specs = (
        head, head,
        pl.BlockSpec((None, n_heads, V_ROWS, tm), lambda bi, i: (bi, 0, 0, i)),
        tok_t(aw), tok_t(aw), tok(n_heads), tok(d), tok(d),
        pl.BlockSpec((None, POOL_HALO, pw), lambda bi, i: (bi, 0, 0)),
        stat, stat, stat,
    )
    return pl.pallas_call(
        functools.partial(_proj_prompt_kernel, n_heads=n_heads),
        grid=(b, nt),
        in_specs=[tok(d)] + [_const_spec(c.shape) for c in consts],
        out_specs=out_specs,
        out_shape=out_shape,
        scratch_shapes=[pltpu.VMEM((tm + POOL_HALO, pw), F32), pltpu.VMEM((1, LANES), F32)],
        compiler_params=_params(("arbitrary", "arbitrary")),
        name="proj_prompt",
    )(h, *consts)


def _proj_sample_kernel(h_ref, state_ref, g_ref, wpool_ref, wq_ref, wk_ref, wv_ref, wf_ref,
                        bf_ref, wgp_ref, wga_ref, wpg_ref, pscale_ref,
                        z_ref, q_ref, k_ref, v_ref, lf_ref, fc_ref, gpool_ref, sga_ref,
                        *, n_heads, n_new, n_seq):
    gw = LANES
    n_pre = state_ref.shape[0] // n_seq
    u = _rms(h_ref[...], g_ref[...]).astype(BF16)

    z = _dot(u, wpool_ref[...])
    z_ref[...] = z

    def ext_rows(r, sl):
        if r < n_pre:
            return state_ref[r * n_seq:(r + 1) * n_seq, sl]
        return z[(r - n_pre) * n_seq:(r - n_pre + 1) * n_seq, sl]

    pooled = []
    for g, w in enumerate(POOL_WINDOWS):
        sl = slice(g * gw, (g + 1) * gw)
        steps = []
        for s in range(n_new):
            r = n_pre + s
            acc = ext_rows(r, sl)
            for j in range(1, w):
                if r - j >= 0:
                    acc = acc + ext_rows(r - j, sl)
            steps.append(acc / float(min(r + 1, w)) - ext_rows(r, sl))
        pooled.append(jnp.concatenate(steps, axis=0))
    branch_pool = _pool_branch(pooled, wpg_ref) * pscale_ref[...]
    gpool_ref[...] = (jax.nn.sigmoid(_dot(u, wgp_ref[...])) * branch_pool).astype(BF16)
    sga_ref[...] = jax.nn.sigmoid(_dot(u, wga_ref[...])).astype(BF16)

    tm = h_ref.shape[0]
    lane_ok = _head_lane_mask((tm, LANES), n_heads)
    logf = jnp.where(lane_ok, _log_sigmoid(_dot(u, wf_ref[...]) + bf_ref[...]), 0.0)
    lf_ref[...] = logf[:, 0:n_heads]
    run = None
    sums = []
    for s in range(n_new):
        blk = logf[s * n_seq:(s + 1) * n_seq, :]
        run = blk if run is None else run + blk
        sums.append(run)
    fc_ref[...] = jnp.concatenate(sums, axis=0)[:, 0:n_heads]

    q_ref[...] = _dot(u, wq_ref[...]).astype(BF16)
    k_ref[...] = _dot(u, wk_ref[...])
    v_ref[...] = _dot(u, wv_ref[...])


def _proj_sample(h, state, w, n_heads, n_new, n_seq):
    n, d = h.shape
    aw = w["wk"].shape[1]
    pw = w["wpool"].shape[1]
    names = ["g_mix", "wpool", "wq", "wk", "wv", "wf", "bf", "wgp", "wga", "wpg", "pscale"]
    consts = [w[n_] for n_ in names]
    full = lambda shape: pl.BlockSpec(shape, lambda i: (0,) * len(shape))
    out_shape = (
        jax.ShapeDtypeStruct((n, pw), F32),
        jax.ShapeDtypeStruct((n, aw), BF16),
        jax.ShapeDtypeStruct((n, aw), F32),
        jax.ShapeDtypeStruct((n, aw), F32),
        jax.ShapeDtypeStruct((n, n_heads), F32),
        jax.ShapeDtypeStruct((n, n_heads), F32),
        jax.ShapeDtypeStruct((n, d), BF16),
        jax.ShapeDtypeStruct((n, d), BF16),
    )
    return pl.pallas_call(
        functools.partial(_proj_sample_kernel, n_heads=n_heads, n_new=n_new, n_seq=n_seq),
        grid=(1,),
        in_specs=[full(h.shape), full(state.shape)] + [_const_spec(c.shape) for c in consts],
        out_specs=tuple(full(s.shape) for s in out_shape),
        out_shape=out_shape,
        compiler_params=_params(("arbitrary",)),
        name="proj_sample",
    )(h, state, *consts)


def _attn_kernel(q_ref, k_ref, vt_ref, qn_ref, kn_ref, fe_ref, o_ref, *, head_dim):
    i = pl.program_id(2)
    nh, tq, _ = q_ref.shape
    tk = tq
    nt = fe_ref.shape[1]
    qs = [q_ref[h] for h in range(nh)]

    def tile(h, j, m, acc, masked):
        start = pl.multiple_of(j * tk, tk)
        s = lax.dot_general(k_ref[h, pl.ds(start, tk), :], qs[h], _NT, preferred_element_type=F32)
        if masked:
            kpos = lax.broadcasted_iota(jnp.int32, (tk, tq), 0)
            qpos = lax.broadcasted_iota(jnp.int32, (tk, tq), 1)
            s = jnp.where(kpos <= qpos, s, MASKED)
        m_new = jnp.maximum(m, jnp.max(s, axis=0, keepdims=True))
        alpha = jnp.exp2(m - m_new)
        p = jnp.exp2(s - m_new).astype(BF16)
        acc = alpha * acc + _dot(vt_ref[h, :, pl.ds(start, tk)], p)
        return m_new, acc

    state = []
    for h in range(nh):
        state += list(tile(h, i, jnp.full((1, tq), MASKED, F32), jnp.zeros((V_ROWS, tq), F32), True))

    lane = lax.broadcasted_iota(jnp.int32, (1, nt), 1)
    n_back = None
    for h in range(nh):
        m_min = jnp.min(state[2 * h], axis=1, keepdims=True)
        q2 = jnp.max(jnp.where(lane == i, qn_ref[h:h + 1, :], 0.0), axis=1, keepdims=True)
        k2 = jnp.max(kn_ref[h:h + 1, :], axis=1, keepdims=True)
        bound = jnp.sqrt(q2 * k2) - fe_ref[h:h + 1, :]
        need = (lane < i) & (bound > m_min - SKIP_MARGIN)
        n_h = jnp.sum(need.astype(jnp.int32))
        n_back = n_h if n_back is None else jnp.maximum(n_back, n_h)

    def body(jj, c):
        out = []
        for h in range(nh):
            out += list(tile(h, i - 1 - jj, c[2 * h], c[2 * h + 1], False))
        return tuple(out)

    state = lax.fori_loop(0, n_back, body, tuple(state))
    for h in range(nh):
        acc = state[2 * h + 1]
        o_ref[h * head_dim:(h + 1) * head_dim, :] = (
            acc[0:head_dim, :] / acc[head_dim:head_dim + 1, :]).astype(BF16)


def _attention(qa, ka, vt, qn, kn, fe, head_dim, tq):
    b, nh, t, _ = qa.shape
    g = ATTN_HEADS
    nt = t // tq
    stat = pl.BlockSpec((None, None, g, nt), lambda bi, h, i: (bi, h, 0, 0))
    return pl.pallas_call(
        functools.partial(_attn_kernel, head_dim=head_dim),
        grid=(b, nh // g, nt),
        in_specs=[pl.BlockSpec((None, g, tq, HEAD_PAD), lambda bi, h, i: (bi, h, i, 0)),
                  pl.BlockSpec((None, g, t, HEAD_PAD), lambda bi, h, i: (bi, h, 0, 0)),
                  pl.BlockSpec((None, g, V_ROWS, t), lambda bi, h, i: (bi, h, 0, 0)),
                  stat, stat, stat],
        out_specs=pl.BlockSpec((None, g * head_dim, tq), lambda bi, h, i: (bi, h, i)),
        out_shape=jax.ShapeDtypeStruct((b, nh * head_dim, t), BF16),
        compiler_params=_params(("parallel", "parallel", "arbitrary")),
        name="attn_prompt",
    )(qa, ka, vt, qn, kn, fe)


def _decode_kernel(pt_ref, q_ref, kn_ref, vn_ref, negc_ref, upper_ref, *refs, n_heads, n_new):
    del pt_ref
    pp = DECODE_PAGES
    kp, vp, lp = refs[0:pp], refs[pp:2 * pp], refs[2 * pp:3 * pp]
    o_ref, kb_ref, vb_ref, m_ref, l_ref, acc_ref, fcar_ref = refs[3 * pp:]
    g = pl.program_id(1)
    width, page = kp[0].shape
    head_dim = width // n_heads
    rows = n_new * n_heads

    @pl.when(g == 0)
    def _():
        m_ref[...] = jnp.full(m_ref.shape, MASKED, F32)
        l_ref[...] = jnp.zeros(l_ref.shape, F32)
        acc_ref[...] = jnp.zeros(acc_ref.shape, F32)
        fcar_ref[...] = jnp.zeros(fcar_ref.shape, F32)

    q = q_ref[...].astype(F32)
    qrows = jnp.concatenate([jnp.broadcast_to(q[s:s + 1, :], (n_heads, width)) for s in range(n_new)],
                            axis=0)
    rid = lax.broadcasted_iota(jnp.int32, (rows, width), 0)
    cid = lax.broadcasted_iota(jnp.int32, (rows, width), 1)
    own = (cid // head_dim) == (rid % n_heads)
    qbd = jnp.where(own, qrows, 0.0).astype(BF16)

    def update(s, vals_t):
        m_new = jnp.maximum(m_ref[...], jnp.max(s, axis=1, keepdims=True))
        alpha = jnp.exp(m_ref[...] - m_new)
        p = jnp.exp(s - m_new)
        l_ref[...] = alpha * l_ref[...] + jnp.sum(p, axis=1, keepdims=True)
        pv = lax.dot_general(p.astype(BF16), vals_t, _NT, preferred_element_type=F32)
        acc_ref[...] = alpha * acc_ref[...] + pv
        m_ref[...] = m_new

    for i in range(pp):
        kb_ref[:, i * page:(i + 1) * page] = kp[i][...].astype(BF16)
        vb_ref[:, i * page:(i + 1) * page] = vp[i][...].astype(BF16)

    x = jnp.concatenate([lp[i][...] for i in range(pp)], axis=0)
    parts = jnp.concatenate(_split3(x), axis=0).astype(BF16)
    c = _dot(parts, upper_ref[...])
    nr = pp * n_heads
    fin = c[0:nr] + c[nr:2 * nr] + c[2 * nr:3 * nr]
    carry = fcar_ref[...]
    biases = []
    for i in range(pp):
        fi = fin[i * n_heads:(i + 1) * n_heads, :]
        biases.append(jnp.concatenate([-(fi + carry)] * n_new, axis=0))
        carry = carry + jnp.broadcast_to(fi[:, page - 1:page], carry.shape)
    fcar_ref[...] = carry
    update(_dot(qbd, kb_ref[...]) + jnp.concatenate(biases, axis=1), vb_ref[...])

    @pl.when(g == pl.num_programs(1) - 1)
    def _():
        sn = _dot(qbd, kn_ref[...])
        bias = jnp.concatenate([negc_ref[...] - carry] * n_new, axis=0)
        r2 = lax.broadcasted_iota(jnp.int32, sn.shape, 0)
        c2 = lax.broadcasted_iota(jnp.int32, sn.shape, 1)
        ok = (c2 < n_new) & (c2 <= r2 // n_heads)
        update(jnp.where(ok, sn + bias, MASKED), vn_ref[...])
        out = jnp.where(own, acc_ref[...] / l_ref[...], 0.0)
        o_ref[...] = jnp.sum(out.reshape(n_new, n_heads, width), axis=1)


def _decode(page_table, q, kn_t, vn_t, negc, upper, cache_kt, cache_vt, cache_lft, n_heads):
    bd, n_new, width = q.shape
    n_pages = page_table.shape[1]
    page = cache_kt.shape[2]
    pp = DECODE_PAGES
    rows = n_new * n_heads
    pt = page_table.reshape(-1)

    def page_map(i, b, g, pt_ref):
        return (pt_ref[b * n_pages + g * pp + i], 0, 0)

    seq = lambda shape: pl.BlockSpec((None,) + shape, lambda b, g, pt_ref: (b, 0, 0))
    in_specs = [seq((n_new, width)), seq((width, page)), seq((width, page)), seq((n_heads, LANES)),
                pl.BlockSpec(upper.shape, lambda b, g, pt_ref: (0, 0))]
    in_specs += [pl.BlockSpec((None, width, page), functools.partial(page_map, i)) for i in range(pp)]
    in_specs += [pl.BlockSpec((None, width, page), functools.partial(page_map, i)) for i in range(pp)]
    in_specs += [pl.BlockSpec((None, n_heads, page), functools.partial(page_map, i)) for i in range(pp)]
    grid_spec = pltpu.PrefetchScalarGridSpec(
        num_scalar_prefetch=1,
        grid=(bd, n_pages // pp),
        in_specs=in_specs,
        out_specs=seq((n_new, width)),
        scratch_shapes=[pltpu.VMEM((width, pp * page), BF16), pltpu.VMEM((width, pp * page), BF16),
                        pltpu.VMEM((rows, 1), F32), pltpu.VMEM((rows, 1), F32),
                        pltpu.VMEM((rows, width), F32), pltpu.VMEM((n_heads, LANES), F32)],
    )
    return pl.pallas_call(
        functools.partial(_decode_kernel, n_heads=n_heads, n_new=n_new),
        grid_spec=grid_spec,
        out_shape=jax.ShapeDtypeStruct((bd, n_new, width), F32),
        compiler_params=_params(("parallel", "arbitrary")),
        name="attn_decode",
    )(pt, q, kn_t, vn_t, negc, upper, *([cache_kt] * pp), *([cache_vt] * pp), *([cache_lft] * pp))


def _post_kernel(h_ref, gpool_ref, sga_ref, attn_ref, p_ref, wab_ref, wout_ref, g2_ref, wg_ref,
                 wu_ref, wd_ref, gple_ref, wpg_ref, wple_ref, gfin_ref, y_ref, *, attn_transposed):
    if attn_transposed:
        branch_attn = lax.dot_general(attn_ref[...], wab_ref[...], _TN, preferred_element_type=F32)
    else:
        branch_attn = _dot(attn_ref[...].astype(BF16), wab_ref[...])
    merged = gpool_ref[...].astype(F32) + sga_ref[...].astype(F32) * branch_attn
    h = h_ref[...] + _dot(merged.astype(BF16), wout_ref[...])
    h = _ffn_value(h, g2_ref[...], wg_ref, wu_ref, wd_ref)
    gate = jax.nn.sigmoid(_dot(_rms(h, gple_ref[...]).astype(BF16), wpg_ref[...]))
    h = h + _dot(p_ref[...].astype(BF16), wple_ref[...]) * gate
    y_ref[...] = _rms(h, gfin_ref[...])


def _post(h, gpool, sga, attn, p, w, tm, attn_transposed):
    b, t, d = h.shape
    names = ["wab", "wout", "g_ffn2", "wg2", "wu2", "wd2", "g_ple", "wpgate", "wple", "g_final"]
    consts = [w[n] for n in names]
    tok = lambda width: pl.BlockSpec((None, tm, width), lambda bi, i: (bi, i, 0))
    if attn_transposed:
        attn_spec = pl.BlockSpec((None, attn.shape[1], tm), lambda bi, i: (bi, 0, i))
    else:
        attn_spec = tok(attn.shape[2])
    return pl.pallas_call(
        functools.partial(_post_kernel, attn_transposed=attn_transposed),
        grid=(b, t // tm),
        in_specs=[tok(d), tok(d), tok(d), attn_spec, tok(p.shape[2])]
                 + [_const_spec(c.shape) for c in consts],
        out_specs=tok(d),
        out_shape=jax.ShapeDtypeStruct((b, t, d), F32),
        compiler_params=_params(("parallel", "parallel")),
        name="post",
    )(h, gpool, sga, attn, p, *consts)


def _prep_weights(g_ffn1, w_ffn1_gate, w_ffn1_up, w_ffn1_down, g_mix, w_in, b_forget, w_pool_group,
                  pool_scale, w_attn_branch, w_out, g_ffn2, w_ffn2_gate, w_ffn2_up, w_ffn2_down,
                  g_ple, w_ple_gate, w_ple, g_final, n_heads, head_dim, pool_width, tm):
    d = w_in.shape[0]
    aw = n_heads * head_dim
    o = 0
    wpool = w_in[:, o:o + pool_width]; o += pool_width
    wq = w_in[:, o:o + aw] * (head_dim ** -0.5); o += aw
    wk = w_in[:, o:o + aw]; o += aw
    wv = w_in[:, o:o + aw]; o += aw
    wf = w_in[:, o:o + n_heads]; o += n_heads
    wgp = w_in[:, o:o + d]; o += d
    wga = w_in[:, o:o + d]

    def per_head_pad(x):
        x = x.reshape(d, n_heads, head_dim)
        return jnp.pad(x, ((0, 0), (0, 0), (0, HEAD_PAD - head_dim))).reshape(d, n_heads * HEAD_PAD)

    place = np.zeros((LANES, n_heads * HEAD_PAD), np.float32)
    qones = np.zeros((1, n_heads * HEAD_PAD), np.float32)
    headsel = np.zeros((n_heads * HEAD_PAD, LANES), np.float32)
    for h in range(n_heads):
        headsel[h * HEAD_PAD:h * HEAD_PAD + head_dim, h] = 1.0
        for part in range(F_PARTS):
            place[part * n_heads + h, h * HEAD_PAD + head_dim + part] = -1.0
            qones[0, h * HEAD_PAD + head_dim + part] = 1.0
    row = lambda x: x.reshape(1, -1).astype(F32)
    bf = lambda x: x.astype(BF16)
    return {
        "g_ffn1": row(g_ffn1), "wg1": bf(w_ffn1_gate), "wu1": bf(w_ffn1_up), "wd1": bf(w_ffn1_down),
        "g_mix": row(g_mix), "wpool": bf(wpool), "wq": bf(wq), "wk": bf(wk), "wv": bf(wv),
        "wqa": bf(per_head_pad(wq * LOG2E)), "wka": bf(per_head_pad(wk)),
        "wkt": bf(wk.T), "wvt": bf(wv.T),
        "wf": bf(jnp.pad(wf, ((0, 0), (0, LANES - n_heads)))),
        "bf": jnp.pad(row(b_forget), ((0, 0), (0, LANES - n_heads))),
        "wgp": bf(wgp), "wga": bf(wga), "wpg": bf(w_pool_group), "pscale": row(pool_scale),
        "place": jnp.asarray(place, BF16), "qones": jnp.asarray(qones, F32),
        "headsel": jnp.asarray(headsel, BF16),
        "tri": jnp.asarray(np.tril(np.ones((tm, tm), np.float32)), BF16),
        "wab": bf(w_attn_branch), "wout": bf(w_out),
        "g_ffn2": row(g_ffn2), "wg2": bf(w_ffn2_gate), "wu2": bf(w_ffn2_up), "wd2": bf(w_ffn2_down),
        "g_ple": row(g_ple), "wpgate": bf(w_ple_gate), "wple": bf(w_ple), "g_final": row(g_final),
    }


def kernel(x_prompt, x_sample, cache_k, cache_v, cache_logf, state_pool, page_table, p_prompt, p_sample, g_ffn1, w_ffn1_gate, w_ffn1_up, w_ffn1_down, g_mix, w_in, b_forget, w_pool_group, pool_scale, w_attn_branch, w_out, g_ffn2, w_ffn2_gate, w_ffn2_up, w_ffn2_down, g_ple, w_ple_gate, w_ple, g_final):
    depth = cache_k.shape[0]
    assert depth == 1, "one trunk layer"
    b, t, d = x_prompt.shape
    bd, n_new, _ = x_sample.shape
    _, n_pool_pages, page, n_heads, head_dim = cache_k.shape
    aw = n_heads * head_dim
    n_pre, pool_width = state_pool.shape[2], state_pool.shape[3]
    assert head_dim + F_PARTS <= HEAD_PAD and head_dim < V_ROWS and page == LANES
    assert pool_width == len(POOL_WINDOWS) * LANES and n_pre == POOL_WINDOWS[-1] - 1
    assert n_heads % ATTN_HEADS == 0
    tm = min(TOKEN_TILE, t)
    assert t // tm <= LANES
    ns = bd * n_new

    w = _prep_weights(g_ffn1[0], w_ffn1_gate[0], w_ffn1_up[0], w_ffn1_down[0], g_mix[0], w_in[0],
                      b_forget[0], w_pool_group[0], pool_scale[0], w_attn_branch[0], w_out[0],
                      g_ffn2[0], w_ffn2_gate[0], w_ffn2_up[0], w_ffn2_down[0], g_ple[0],
                      w_ple_gate[0], w_ple[0], g_final, n_heads, head_dim, pool_width, tm)

    h1 = _ffn(x_prompt.reshape(b * t, d), w["g_ffn1"], w["wg1"], w["wu1"], w["wd1"], tm)
    h1 = h1.reshape(b, t, d)
    qa, ka, vt, kt_p, vt_p, lf_p, gpool, sga, zlast, qn, kn, fe = _proj_prompt(h1, w, tm, n_heads)
    stat = lambda x: jnp.swapaxes(x[:, :, 0, :n_heads], 1, 2).reshape(
        b, n_heads // ATTN_HEADS, ATTN_HEADS, t // tm)
    attn_t = _attention(qa, ka, vt, stat(qn), stat(kn), stat(fe), head_dim, tm)
    y_prompt = _post(h1, gpool, sga, attn_t, p_prompt[0], w, tm, True)
    heads_last = lambda x: jnp.transpose(x.reshape(b, n_heads, head_dim, t), (0, 3, 1, 2))[None]

    step_major = lambda x: jnp.swapaxes(x, 0, 1).reshape(ns, x.shape[-1])
    seq_major = lambda x: jnp.swapaxes(x.reshape(n_new, bd, x.shape[-1]), 0, 1)
    hs1 = _ffn(step_major(x_sample), w["g_ffn1"], w["wg1"], w["wu1"], w["wd1"], ns)
    state = jnp.swapaxes(state_pool[0], 0, 1).reshape(n_pre * bd, pool_width)
    z_s, q_s, k_s, v_s, lf_s, fc_s, gpool_s, sga_s = _proj_sample(hs1, state, w, n_heads, n_new, bd)
    k_s, v_s, lf_s = seq_major(k_s), seq_major(v_s), seq_major(lf_s)
    new_page = lambda x: jnp.pad(jnp.swapaxes(x, 1, 2).astype(BF16), ((0, 0), (0, 0), (0, page - n_new)))
    negc = -jnp.swapaxes(seq_major(fc_s), 1, 2)
    negc = jnp.pad(negc, ((0, 0), (0, 0), (0, LANES - n_new)))
    upper = jnp.asarray(np.triu(np.ones((page, page), np.float32)), BF16)
    pages_t = lambda c: jnp.transpose(c[0], (0, 2, 3, 1)).reshape(n_pool_pages, aw, page)
    attn_s = _decode(page_table, seq_major(q_s), new_page(k_s), new_page(v_s), negc, upper,
                     pages_t(cache_k), pages_t(cache_v), jnp.swapaxes(cache_logf[0], 1, 2), n_heads)
    y_s = _post(hs1[None], gpool_s[None], sga_s[None], step_major(attn_s)[None],
                step_major(p_sample[0])[None], w, ns, False)
    y_sample = seq_major(y_s[0])
    pool_sample = jnp.concatenate([state_pool[0], seq_major(z_s)], axis=1)[:, -n_pre:]

    return (y_prompt, y_sample, heads_last(kt_p), heads_last(vt_p),
            lf_p[None], zlast[None, :, POOL_HALO - n_pre:],
            k_s.reshape(1, bd, n_new, n_heads, head_dim), v_s.reshape(1, bd, n_new, n_heads, head_dim),
            lf_s[None], pool_sample[None])
```

```python
import functools

import numpy as np
import jax
import jax.numpy as jnp
from jax import lax
from jax.experimental import pallas as pl
from jax.experimental.pallas import tpu as pltpu

F32 = jnp.float32
BF16 = jnp.bfloat16

RMS_EPS = 1e-6
MASKED = -1e30
POOL_WINDOWS = (2, 4, 8, 16)
POOL_HALO = 16

LANES = 128
HEAD_PAD = 128
V_ROWS = 80
F_PARTS = 3
FF_CHUNK = 768
TOKEN_TILE = 512
DECODE_PAGES = 8
ATTN_HEADS = 2
ATTN_GROUP = 2
VMEM_LIMIT = 56 * 1024 * 1024
LOG2E = 1.4426950408889634
NORM_SLACK = 1.01
SKIP_MARGIN = 140.0

_NT = (((1,), (1,)), ((), ()))
_TN = (((0,), (0,)), ((), ()))


def _rms(x, g):
    r = lax.rsqrt(jnp.mean(x * x, axis=-1, keepdims=True) + RMS_EPS)
    return x * r * g


def _dot(a, b):
    return jnp.dot(a, b, preferred_element_type=F32)


def _chunks(n, c):
    return [(s, min(c, n - s)) for s in range(0, n, c)]


def _split3(x):
    hi = x.astype(BF16).astype(F32)
    r = x - hi
    mid = r.astype(BF16).astype(F32)
    lo = (r - mid).astype(BF16).astype(F32)
    return hi, mid, lo


def _pack3(x, n):
    hi, mid, lo = _split3(x)
    return (hi + pltpu.roll(mid, n, 1) + pltpu.roll(lo, 2 * n, 1)).astype(BF16)


def _unpack3(c, n):
    return c + pltpu.roll(c, LANES - n, 1) + pltpu.roll(c, LANES - 2 * n, 1)


def _log_sigmoid(x):
    return jnp.minimum(x, 0.0) - jnp.log1p(jnp.exp(-jnp.abs(x)))


def _const_spec(shape):
    nd = len(shape)
    return pl.BlockSpec(shape, lambda *_: (0,) * nd, pipeline_mode=pl.Buffered(1))


def _params(sem):
    return pltpu.CompilerParams(dimension_semantics=sem, vmem_limit_bytes=VMEM_LIMIT)


def _ffn_value(x, g, wg_ref, wu_ref, wd_ref):
    u = _rms(x, g).astype(BF16)
    acc = None
    for s, n in _chunks(wg_ref.shape[1], FF_CHUNK):
        gate = _dot(u, wg_ref[:, s:s + n])
        up = _dot(u, wu_ref[:, s:s + n])
        a = (gate * jax.nn.sigmoid(gate) * up).astype(BF16)
        d = _dot(a, wd_ref[s:s + n, :])
        acc = d if acc is None else acc + d
    return x + 0.5 * acc


def _ffn_kernel(x_ref, g_ref, wg_ref, wu_ref, wd_ref, o_ref):
    o_ref[...] = _ffn_value(x_ref[...], g_ref[...], wg_ref, wu_ref, wd_ref)


def _ffn(x, g, wg, wu, wd, tm):
    n, d = x.shape
    return pl.pallas_call(
        _ffn_kernel,
        grid=(n // tm,),
        in_specs=[pl.BlockSpec((tm, d), lambda i: (i, 0)),
                  _const_spec(g.shape), _const_spec(wg.shape), _const_spec(wu.shape),
                  _const_spec(wd.shape)],
        out_specs=pl.BlockSpec((tm, d), lambda i: (i, 0)),
        out_shape=jax.ShapeDtypeStruct((n, d), F32),
        compiler_params=_params(("parallel",)),
        name="ffn",
    )(x, g, wg, wu, wd)


def _pool_branch(pooled_groups, wpg_ref):
    outs = [_dot(p.astype(BF16), wpg_ref[g]) for g, p in enumerate(pooled_groups)]
    return jnp.concatenate(outs, axis=1)


def _head_lane_mask(shape, n_heads):
    lane = lax.broadcasted_iota(jnp.int32, shape, 1)
    return lane < n_heads


def _proj_prompt_kernel(h_ref, g_ref, wpool_ref, wqa_ref, wka_ref, wkt_ref, wvt_ref,
                        wf_ref, bf_ref, wgp_ref, wga_ref, wpg_ref, pscale_ref, place_ref,
                        tri_ref, qones_ref, headsel_ref,
                        qa_ref, ka_ref, vt_ref, kt_ref, vto_ref, lf_ref, gpool_ref, sga_ref,
                        zlast_ref, qn_ref, kn_ref, fe_ref, zext_ref, fcarry_ref, *, n_heads):
    i = pl.program_id(1)
    tm = h_ref.shape[0]
    gw = LANES

    @pl.when(i == 0)
    def _():
        zext_ref[0:POOL_HALO, :] = jnp.zeros((POOL_HALO, zext_ref.shape[1]), F32)
        fcarry_ref[...] = jnp.zeros(fcarry_ref.shape, F32)

    u = _rms(h_ref[...], g_ref[...]).astype(BF16)

    z = _dot(u, wpool_ref[...])
    zext_ref[POOL_HALO:POOL_HALO + tm, :] = z
    row = lax.broadcasted_iota(jnp.int32, (tm, gw), 0) + i * tm
    pooled = []
    for g, w in enumerate(POOL_WINDOWS):
        sl = slice(g * gw, (g + 1) * gw)
        zg = z[:, sl]
        acc = zg
        for j in range(1, w):
            acc = acc + zext_ref[POOL_HALO - j:POOL_HALO - j + tm, sl]
        cnt = jnp.minimum(row + 1, w).astype(F32)
        pooled.append(acc / cnt - zg)
    zlast_ref[...] = zext_ref[tm:tm + POOL_HALO, :]
    zext_ref[0:POOL_HALO, :] = zext_ref[tm:tm + POOL_HALO, :]
    branch_pool = _pool_branch(pooled, wpg_ref) * pscale_ref[...]
    gpool_ref[...] = (jax.nn.sigmoid(_dot(u, wgp_ref[...])) * branch_pool).astype(BF16)
    sga_ref[...] = jax.nn.sigmoid(_dot(u, wga_ref[...])).astype(BF16)

    lane_ok = _head_lane_mask((tm, LANES), n_heads)
    logf = jnp.where(lane_ok, _log_sigmoid(_dot(u, wf_ref[...]) + bf_ref[...]), 0.0)
    lf_ref[...] = logf[:, 0:n_heads]
    csum = _unpack3(_dot(tri_ref[...], _pack3(logf, n_heads)), n_heads)
    fcum = jnp.where(lane_ok, csum + fcarry_ref[...], 0.0)
    fcarry_ref[...] = fcum[tm - 1:tm, :]
    fcum2 = fcum * LOG2E
    fe_ref[...] = fcum2[tm - 1:tm, :]

    qa = (_dot(u, wqa_ref[...]) + qones_ref[...]).astype(BF16)
    ka = (_dot(u, wka_ref[...]) + _dot(_pack3(fcum2, n_heads), place_ref[...])).astype(BF16)
    for x, n_ref in ((qa, qn_ref), (ka, kn_ref)):
        xf = x.astype(F32)
        sq = (xf * xf * NORM_SLACK).astype(BF16)
        n_ref[...] = jnp.max(_dot(sq, headsel_ref[...]), axis=0, keepdims=True)
    kt = lax.dot_general(wkt_ref[...], u, _NT, preferred_element_type=F32)
    vt = lax.dot_general(wvt_ref[...], u, _NT, preferred_element_type=F32)
    kt_ref[...] = kt
    vto_ref[...] = vt
    head_dim = kt.shape[0] // n_heads
    rid = lax.broadcasted_iota(jnp.int32, (V_ROWS - head_dim, tm), 0)
    ones_rows = jnp.where(rid == 0, 1.0, 0.0).astype(BF16)
    for h in range(n_heads):
        qa_ref[h] = qa[:, h * HEAD_PAD:(h + 1) * HEAD_PAD]
        ka_ref[h] = ka[:, h * HEAD_PAD:(h + 1) * HEAD_PAD]
        vt_ref[h, 0:head_dim, :] = vt[h * head_dim:(h + 1) * head_dim, :].astype(BF16)
        vt_ref[h, head_dim:V_ROWS, :] = ones_rows


def _proj_prompt(h, w, tm, n_heads):
    b, t, d = h.shape
    aw = w["wkt"].shape[0]
    pw = w["wpool"].shape[1]
    nt = t // tm
    names = ["g_mix", "wpool", "wqa", "wka", "wkt", "wvt", "wf", "bf", "wgp", "wga", "wpg",
             "pscale", "place", "tri", "qones", "headsel"]
    consts = [w[n] for n in names]
    tok = lambda width: pl.BlockSpec((None, tm, width), lambda bi, i: (bi, i, 0))
    tok_t = lambda rows: pl.BlockSpec((None, rows, tm), lambda bi, i: (bi, 0, i))
    head = pl.BlockSpec((None, n_heads, tm, HEAD_PAD), lambda bi, i: (bi, 0, i, 0))
    stat = pl.BlockSpec((None, None, 1, LANES), lambda bi, i: (bi, i, 0, 0))
    stat_shape = jax.ShapeDtypeStruct((b, nt, 1, LANES), F32)
    out_shape = (
        jax.ShapeDtypeStruct((b, n_heads, t, HEAD_PAD), BF16),
        jax.ShapeDtypeStruct((b, n_heads, t, HEAD_PAD), BF16),
        jax.ShapeDtypeStruct((b, n_heads, V_ROWS, t), BF16),
        jax.ShapeDtypeStruct((b, aw, t), F32),
        jax.ShapeDtypeStruct((b, aw, t), F32),
        jax.ShapeDtypeStruct((b, t, n_heads), F32),
        jax.ShapeDtypeStruct((b, t, d), BF16),
        jax.ShapeDtypeStruct((b, t, d), BF16),
        jax.ShapeDtypeStruct((b, POOL_HALO, pw), F32),
        stat_shape, stat_shape, stat_shape,
    )
    out_specs = (
        head, head,
        pl.BlockSpec((None, n_heads, V_ROWS, tm), lambda bi, i: (bi, 0, 0, i)),
        tok_t(aw), tok_t(aw), tok(n_heads), tok(d), tok(d),
        pl.BlockSpec((None, POOL_HALO, pw), lambda bi, i: (bi, 0, 0)),
        stat, stat, stat,
    )
    return pl.pallas_call(
        functools.partial(_proj_prompt_kernel, n_heads=n_heads),
        grid=(b, nt),
        in_specs=[tok(d)] + [_const_spec(c.shape) for c in consts],
        out_specs=out_specs,
        out_shape=out_shape,
        scratch_shapes=[pltpu.VMEM((tm + POOL_HALO, pw), F32), pltpu.VMEM((1, LANES), F32)],
        compiler_params=_params(("arbitrary", "arbitrary")),
        name="proj_prompt",
    )(h, *consts)


def _proj_sample_kernel(h_ref, state_ref, g_ref, wpool_ref, wq_ref, wk_ref, wv_ref, wf_ref,
                        bf_ref, wgp_ref, wga_ref, wpg_ref, pscale_ref,
                        z_ref, q_ref, k_ref, v_ref, lf_ref, fc_ref, gpool_ref, sga_ref,
                        *, n_heads, n_new, n_seq):
    gw = LANES
    n_pre = state_ref.shape[0] // n_seq
    u = _rms(h_ref[...], g_ref[...]).astype(BF16)

    z = _dot(u, wpool_ref[...])
    z_ref[...] = z

    def ext_rows(r, sl):
        if r < n_pre:
            return state_ref[r * n_seq:(r + 1) * n_seq, sl]
        return z[(r - n_pre) * n_seq:(r - n_pre + 1) * n_seq, sl]

    pooled = []
    for g, w in enumerate(POOL_WINDOWS):
        sl = slice(g * gw, (g + 1) * gw)
        steps = []
        for s in range(n_new):
            r = n_pre + s
            acc = ext_rows(r, sl)
            for j in range(1, w):
                if r - j >= 0:
                    acc = acc + ext_rows(r - j, sl)
            steps.append(acc / float(min(r + 1, w)) - ext_rows(r, sl))
        pooled.append(jnp.concatenate(steps, axis=0))
    branch_pool = _pool_branch(pooled, wpg_ref) * pscale_ref[...]
    gpool_ref[...] = (jax.nn.sigmoid(_dot(u, wgp_ref[...])) * branch_pool).astype(BF16)
    sga_ref[...] = jax.nn.sigmoid(_dot(u, wga_ref[...])).astype(BF16)

    tm = h_ref.shape[0]
    lane_ok = _head_lane_mask((tm, LANES), n_heads)
    logf = jnp.where(lane_ok, _log_sigmoid(_dot(u, wf_ref[...]) + bf_ref[...]), 0.0)
    lf_ref[...] = logf[:, 0:n_heads]
    run = None
    sums = []
    for s in range(n_new):
        blk = logf[s * n_seq:(s + 1) * n_seq, :]
        run = blk if run is None else run + blk
        sums.append(run)
    fc_ref[...] = jnp.concatenate(sums, axis=0)[:, 0:n_heads]

    q_ref[...] = _dot(u, wq_ref[...]).astype(BF16)
    k_ref[...] = _dot(u, wk_ref[...])
    v_ref[...] = _dot(u, wv_ref[...])


def _proj_sample(h, state, w, n_heads, n_new, n_seq):
    n, d = h.shape
    aw = w["wk"].shape[1]
    pw = w["wpool"].shape[1]
    names = ["g_mix", "wpool", "wq", "wk", "wv", "wf", "bf", "wgp", "wga", "wpg", "pscale"]
    consts = [w[n_] for n_ in names]
    full = lambda shape: pl.BlockSpec(shape, lambda i: (0,) * len(shape))
    out_shape = (
        jax.ShapeDtypeStruct((n, pw), F32),
        jax.ShapeDtypeStruct((n, aw), BF16),
        jax.ShapeDtypeStruct((n, aw), F32),
        jax.ShapeDtypeStruct((n, aw), F32),
        jax.ShapeDtypeStruct((n, n_heads), F32),
        jax.ShapeDtypeStruct((n, n_heads), F32),
        jax.ShapeDtypeStruct((n, d), BF16),
        jax.ShapeDtypeStruct((n, d), BF16),
    )
    return pl.pallas_call(
        functools.partial(_proj_sample_kernel, n_heads=n_heads, n_new=n_new, n_seq=n_seq),
        grid=(1,),
        in_specs=[full(h.shape), full(state.shape)] + [_const_spec(c.shape) for c in consts],
        out_specs=tuple(full(s.shape) for s in out_shape),
        out_shape=out_shape,
        compiler_params=_params(("arbitrary",)),
        name="proj_sample",
    )(h, state, *consts)


def _attn_kernel(q_ref, k_ref, vt_ref, qn_ref, kn_ref, fe_ref, o_ref, *, head_dim):
    i = pl.program_id(2)
    nh, tq, _ = q_ref.shape
    tk = tq
    nt = fe_ref.shape[1]
    qs = [q_ref[h] for h in range(nh)]

    def scores(h, j):
        start = pl.multiple_of(j * tk, tk)
        return lax.dot_general(k_ref[h, pl.ds(start, tk), :], qs[h], _NT, preferred_element_type=F32)

    def absorb(h, j, s, m, acc):
        start = pl.multiple_of(j * tk, tk)
        m_new = jnp.maximum(m, jnp.max(s, axis=0, keepdims=True))
        alpha = jnp.exp2(m - m_new)
        p = jnp.exp2(s - m_new).astype(BF16)
        acc = alpha * acc + _dot(vt_ref[h, :, pl.ds(start, tk)], p)
        return m_new, acc

    kpos = lax.broadcasted_iota(jnp.int32, (tk, tq), 0)
    qpos = lax.broadcasted_iota(jnp.int32, (tk, tq), 1)
    diag = [jnp.where(kpos <= qpos, scores(h, i), MASKED) for h in range(nh)]
    state = []
    for h in range(nh):
        state += list(absorb(h, i, diag[h], jnp.full((1, tq), MASKED, F32),
                             jnp.zeros((V_ROWS, tq), F32)))

    lane = lax.broadcasted_iota(jnp.int32, (1, nt), 1)
    n_back = None
    for h in range(nh):
        m_min = jnp.min(state[2 * h], axis=1, keepdims=True)
        q2 = jnp.max(jnp.where(lane == i, qn_ref[h:h + 1, :], 0.0), axis=1, keepdims=True)
        k2 = jnp.max(kn_ref[h:h + 1, :], axis=1, keepdims=True)
        bound = jnp.sqrt(q2 * k2) - fe_ref[h:h + 1, :]
        need = (lane < i) & (bound > m_min - SKIP_MARGIN)
        n_h = jnp.sum(need.astype(jnp.int32))
        n_back = n_h if n_back is None else jnp.maximum(n_back, n_h)

    def make_body(group):
        def body(jj, c):
            c = list(c)
            first = c[-1]
            tiles = [first - jj * group - g for g in range(group)]
            ss = [[scores(h, j) for h in range(nh)] for j in tiles]
            for g, j in enumerate(tiles):
                for h in range(nh):
                    c[2 * h], c[2 * h + 1] = absorb(h, j, ss[g][h], c[2 * h], c[2 * h + 1])
            return tuple(c)
        return body

    group = ATTN_GROUP
    odd = n_back % group
    state = lax.fori_loop(0, odd, make_body(1), tuple(state) + (i - 1,))
    state = lax.fori_loop(0, n_back // group, make_body(group), state[:-1] + (i - 1 - odd,))
    for h in range(nh):
        acc = state[2 * h + 1]
        o_ref[h * head_dim:(h + 1) * head_dim, :] = (
            acc[0:head_dim, :] / acc[head_dim:head_dim + 1, :]).astype(BF16)


def _attention(qa, ka, vt, qn, kn, fe, head_dim, tq):
    b, nh, t, _ = qa.shape
    g = ATTN_HEADS
    nt = t // tq
    stat = pl.BlockSpec((None, None, g, nt), lambda bi, h, i: (bi, h, 0, 0))
    return pl.pallas_call(
        functools.partial(_attn_kernel, head_dim=head_dim),
        grid=(b, nh // g, nt),
        in_specs=[pl.BlockSpec((None, g, tq, HEAD_PAD), lambda bi, h, i: (bi, h, i, 0)),
                  pl.BlockSpec((None, g, t, HEAD_PAD), lambda bi, h, i: (bi, h, 0, 0)),
                  pl.BlockSpec((None, g, V_ROWS, t), lambda bi, h, i: (bi, h, 0, 0)),
                  stat, stat, stat],
        out_specs=pl.BlockSpec((None, g * head_dim, tq), lambda bi, h, i: (bi, h, i)),
        out_shape=jax.ShapeDtypeStruct((b, nh * head_dim, t), BF16),
        compiler_params=_params(("parallel", "parallel", "arbitrary")),
        name="attn_prompt",
    )(qa, ka, vt, qn, kn, fe)


def _decode_kernel(pt_ref, q_ref, kn_ref, vn_ref, negc_ref, upper_ref, ck_hbm, cv_hbm, cl_hbm,
                   o_ref, kbuf, vbuf, lbuf, sems, kb_ref, vb_ref, m_ref, l_ref, acc_ref, fcar_ref,
                   *, n_heads, n_new):
    _, pp, width, page = kbuf.shape
    g = pl.program_id(1)
    step = pl.program_id(0) * pl.num_programs(1) + g
    n_steps = pl.num_programs(0) * pl.num_programs(1)
    head_dim = width // n_heads
    rows = n_new * n_heads

    def copies(slot, page_ids):
        return [pltpu.make_async_copy(hbm.at[pid], buf.at[slot, i], sems.at[slot, k])
                for i, pid in enumerate(page_ids)
                for k, (hbm, buf) in enumerate(((ck_hbm, kbuf), (cv_hbm, vbuf), (cl_hbm, lbuf)))]

    def start(slot, first):
        for c in copies(slot, [pt_ref[first + i] for i in range(pp)]):
            c.start()

    def wait(slot):
        for c in copies(slot, [0] * pp):
            c.wait()

    @pl.when(step == 0)
    def _():
        start(0, 0)

    @pl.when(g == 0)
    def _():
        m_ref[...] = jnp.full(m_ref.shape, MASKED, F32)
        l_ref[...] = jnp.zeros(l_ref.shape, F32)
        acc_ref[...] = jnp.zeros(acc_ref.shape, F32)
        fcar_ref[...] = jnp.zeros(fcar_ref.shape, F32)

    q = q_ref[...].astype(F32)
    qrows = jnp.concatenate([jnp.broadcast_to(q[s:s + 1, :], (n_heads, width)) for s in range(n_new)],
                            axis=0)
    rid = lax.broadcasted_iota(jnp.int32, (rows, width), 0)
    cid = lax.broadcasted_iota(jnp.int32, (rows, width), 1)
    own = (cid // head_dim) == (rid % n_heads)
    qbd = jnp.where(own, qrows, 0.0).astype(BF16)

    def update(s, vals_t):
        m_new = jnp.maximum(m_ref[...], jnp.max(s, axis=1, keepdims=True))
        alpha = jnp.exp(m_ref[...] - m_new)
        p = jnp.exp(s - m_new)
        l_ref[...] = alpha * l_ref[...] + jnp.sum(p, axis=1, keepdims=True)
        pv = lax.dot_general(p.astype(BF16), vals_t, _NT, preferred_element_type=F32)
        acc_ref[...] = alpha * acc_ref[...] + pv
        m_ref[...] = m_new

    def consume(slot):
        for i in range(pp):
            kb_ref[:, i * page:(i + 1) * page] = kbuf[slot, i].astype(BF16)
            vb_ref[:, i * page:(i + 1) * page] = vbuf[slot, i].astype(BF16)
        x = jnp.concatenate([lbuf[slot, i] for i in range(pp)], axis=0)
        parts = jnp.concatenate(_split3(x), axis=0).astype(BF16)
        c = _dot(parts, upper_ref[...])
        nr = pp * n_heads
        fin = c[0:nr] + c[nr:2 * nr] + c[2 * nr:3 * nr]
        carry = fcar_ref[...]
        biases = []
        for i in range(pp):
            fi = fin[i * n_heads:(i + 1) * n_heads, :]
            biases.append(jnp.concatenate([-(fi + carry)] * n_new, axis=0))
            carry = carry + jnp.broadcast_to(fi[:, page - 1:page], carry.shape)
        fcar_ref[...] = carry
        update(_dot(qbd, kb_ref[...]) + jnp.concatenate(biases, axis=1), vb_ref[...])

    first = step * (2 * pp)
    wait(0)
    start(1, first + pp)
    consume(0)
    wait(1)

    @pl.when(step + 1 < n_steps)
    def _():
        start(0, first + 2 * pp)

    consume(1)

    @pl.when(g == pl.num_programs(1) - 1)
    def _():
        carry = fcar_ref[...]
        sn = _dot(qbd, kn_ref[...])
        bias = jnp.concatenate([negc_ref[...] - carry] * n_new, axis=0)
        r2 = lax.broadcasted_iota(jnp.int32, sn.shape, 0)
        c2 = lax.broadcasted_iota(jnp.int32, sn.shape, 1)
        ok = (c2 < n_new) & (c2 <= r2 // n_heads)
        update(jnp.where(ok, sn + bias, MASKED), vn_ref[...])
        out = jnp.where(own, acc_ref[...] / l_ref[...], 0.0)
        o_ref[...] = jnp.sum(out.reshape(n_new, n_heads, width), axis=1)


def _decode(page_table, q, kn_t, vn_t, negc, upper, cache_kt, cache_vt, cache_lft, n_heads):
    bd, n_new, width = q.shape
    n_pages = page_table.shape[1]
    page = cache_kt.shape[2]
    pp = min(DECODE_PAGES, n_pages // 2)
    assert n_pages % (2 * pp) == 0
    rows = n_new * n_heads
    pt = page_table.reshape(-1)

    seq = lambda shape: pl.BlockSpec((None,) + shape, lambda b, g, pt_ref: (b, 0, 0))
    hbm = pl.BlockSpec(memory_space=pl.ANY)
    in_specs = [seq((n_new, width)), seq((width, page)), seq((width, page)), seq((n_heads, LANES)),
                pl.BlockSpec(upper.shape, lambda b, g, pt_ref: (0, 0)), hbm, hbm, hbm]
    grid_spec = pltpu.PrefetchScalarGridSpec(
        num_scalar_prefetch=1,
        grid=(bd, n_pages // (2 * pp)),
        in_specs=in_specs,
        out_specs=seq((n_new, width)),
        scratch_shapes=[pltpu.VMEM((2, pp, width, page), F32), pltpu.VMEM((2, pp, width, page), F32),
                        pltpu.VMEM((2, pp, n_heads, page), F32), pltpu.SemaphoreType.DMA((2, 3)),
                        pltpu.VMEM((width, pp * page), BF16), pltpu.VMEM((width, pp * page), BF16),
                        pltpu.VMEM((rows, 1), F32), pltpu.VMEM((rows, 1), F32),
                        pltpu.VMEM((rows, width), F32), pltpu.VMEM((n_heads, LANES), F32)],
    )
    return pl.pallas_call(
        functools.partial(_decode_kernel, n_heads=n_heads, n_new=n_new),
        grid_spec=grid_spec,
        out_shape=jax.ShapeDtypeStruct((bd, n_new, width), F32),
        compiler_params=_params(("arbitrary", "arbitrary")),
        name="attn_decode",
    )(pt, q, kn_t, vn_t, negc, upper, cache_kt, cache_vt, cache_lft)


def _post_kernel(h_ref, gpool_ref, sga_ref, attn_ref, p_ref, wab_ref, wout_ref, g2_ref, wg_ref,
                 wu_ref, wd_ref, gple_ref, wpg_ref, wple_ref, gfin_ref, y_ref, *, attn_transposed):
    if attn_transposed:
        branch_attn = lax.dot_general(attn_ref[...], wab_ref[...], _TN, preferred_element_type=F32)
    else:
        branch_attn = _dot(attn_ref[...].astype(BF16), wab_ref[...])
    merged = gpool_ref[...].astype(F32) + sga_ref[...].astype(F32) * branch_attn
    h = h_ref[...] + _dot(merged.astype(BF16), wout_ref[...])
    h = _ffn_value(h, g2_ref[...], wg_ref, wu_ref, wd_ref)
    gate = jax.nn.sigmoid(_dot(_rms(h, gple_ref[...]).astype(BF16), wpg_ref[...]))
    h = h + _dot(p_ref[...].astype(BF16), wple_ref[...]) * gate
    y_ref[...] = _rms(h, gfin_ref[...])


def _post(h, gpool, sga, attn, p, w, tm, attn_transposed):
    b, t, d = h.shape
    names = ["wab", "wout", "g_ffn2", "wg2", "wu2", "wd2", "g_ple", "wpgate", "wple", "g_final"]
    consts = [w[n] for n in names]
    tok = lambda width: pl.BlockSpec((None, tm, width), lambda bi, i: (bi, i, 0))
    if attn_transposed:
        attn_spec = pl.BlockSpec((None, attn.shape[1], tm), lambda bi, i: (bi, 0, i))
    else:
        attn_spec = tok(attn.shape[2])
    return pl.pallas_call(
        functools.partial(_post_kernel, attn_transposed=attn_transposed),
        grid=(b, t // tm),
        in_specs=[tok(d), tok(d), tok(d), attn_spec, tok(p.shape[2])]
                 + [_const_spec(c.shape) for c in consts],
        out_specs=tok(d),
        out_shape=jax.ShapeDtypeStruct((b, t, d), F32),
        compiler_params=_params(("parallel", "parallel")),
        name="post",
    )(h, gpool, sga, attn, p, *consts)


def _prep_weights(g_ffn1, w_ffn1_gate, w_ffn1_up, w_ffn1_down, g_mix, w_in, b_forget, w_pool_group,
                  pool_scale, w_attn_branch, w_out, g_ffn2, w_ffn2_gate, w_ffn2_up, w_ffn2_down,
                  g_ple, w_ple_gate, w_ple, g_final, n_heads, head_dim, pool_width, tm):
    d = w_in.shape[0]
    aw = n_heads * head_dim
    o = 0
    wpool = w_in[:, o:o + pool_width]; o += pool_width
    wq = w_in[:, o:o + aw] * (head_dim ** -0.5); o += aw
    wk = w_in[:, o:o + aw]; o += aw
    wv = w_in[:, o:o + aw]; o += aw
    wf = w_in[:, o:o + n_heads]; o += n_heads
    wgp = w_in[:, o:o + d]; o += d
    wga = w_in[:, o:o + d]

    def per_head_pad(x):
        x = x.reshape(d, n_heads, head_dim)
        return jnp.pad(x, ((0, 0), (0, 0), (0, HEAD_PAD - head_dim))).reshape(d, n_heads * HEAD_PAD)

    place = np.zeros((LANES, n_heads * HEAD_PAD), np.float32)
    qones = np.zeros((1, n_heads * HEAD_PAD), np.float32)
    headsel = np.zeros((n_heads * HEAD_PAD, LANES), np.float32)
    for h in range(n_heads):
        headsel[h * HEAD_PAD:h * HEAD_PAD + head_dim, h] = 1.0
        for part in range(F_PARTS):
            place[part * n_heads + h, h * HEAD_PAD + head_dim + part] = -1.0
            qones[0, h * HEAD_PAD + head_dim + part] = 1.0
    row = lambda x: x.reshape(1, -1).astype(F32)
    bf = lambda x: x.astype(BF16)
    return {
        "g_ffn1": row(g_ffn1), "wg1": bf(w_ffn1_gate), "wu1": bf(w_ffn1_up), "wd1": bf(w_ffn1_down),
        "g_mix": row(g_mix), "wpool": bf(wpool), "wq": bf(wq), "wk": bf(wk), "wv": bf(wv),
        "wqa": bf(per_head_pad(wq * LOG2E)), "wka": bf(per_head_pad(wk)),
        "wkt": bf(wk.T), "wvt": bf(wv.T),
        "wf": bf(jnp.pad(wf, ((0, 0), (0, LANES - n_heads)))),
        "bf": jnp.pad(row(b_forget), ((0, 0), (0, LANES - n_heads))),
        "wgp": bf(wgp), "wga": bf(wga), "wpg": bf(w_pool_group), "pscale": row(pool_scale),
        "place": jnp.asarray(place, BF16), "qones": jnp.asarray(qones, F32),
        "headsel": jnp.asarray(headsel, BF16),
        "tri": jnp.asarray(np.tril(np.ones((tm, tm), np.float32)), BF16),
        "wab": bf(w_attn_branch), "wout": bf(w_out),
        "g_ffn2": row(g_ffn2), "wg2": bf(w_ffn2_gate), "wu2": bf(w_ffn2_up), "wd2": bf(w_ffn2_down),
        "g_ple": row(g_ple), "wpgate": bf(w_ple_gate), "wple": bf(w_ple), "g_final": row(g_final),
    }


def kernel(x_prompt, x_sample, cache_k, cache_v, cache_logf, state_pool, page_table, p_prompt, p_sample, g_ffn1, w_ffn1_gate, w_ffn1_up, w_ffn1_down, g_mix, w_in, b_forget, w_pool_group, pool_scale, w_attn_branch, w_out, g_ffn2, w_ffn2_gate, w_ffn2_up, w_ffn2_down, g_ple, w_ple_gate, w_ple, g_final):
    depth = cache_k.shape[0]
    assert depth == 1, "one trunk layer"
    b, t, d = x_prompt.shape
    bd, n_new, _ = x_sample.shape
    _, n_pool_pages, page, n_heads, head_dim = cache_k.shape
    aw = n_heads * head_dim
    n_pre, pool_width = state_pool.shape[2], state_pool.shape[3]
    assert head_dim + F_PARTS <= HEAD_PAD and head_dim < V_ROWS and page == LANES
    assert pool_width == len(POOL_WINDOWS) * LANES and n_pre == POOL_WINDOWS[-1] - 1
    assert n_heads % ATTN_HEADS == 0
    tm = min(TOKEN_TILE, t)
    assert t // tm <= LANES
    ns = bd * n_new

    w = _prep_weights(g_ffn1[0], w_ffn1_gate[0], w_ffn1_up[0], w_ffn1_down[0], g_mix[0], w_in[0],
                      b_forget[0], w_pool_group[0], pool_scale[0], w_attn_branch[0], w_out[0],
                      g_ffn2[0], w_ffn2_gate[0], w_ffn2_up[0], w_ffn2_down[0], g_ple[0],
                      w_ple_gate[0], w_ple[0], g_final, n_heads, head_dim, pool_width, tm)

    h1 = _ffn(x_prompt.reshape(b * t, d), w["g_ffn1"], w["wg1"], w["wu1"], w["wd1"], tm)
    h1 = h1.reshape(b, t, d)
    qa, ka, vt, kt_p, vt_p, lf_p, gpool, sga, zlast, qn, kn, fe = _proj_prompt(h1, w, tm, n_heads)
    stat = lambda x: jnp.swapaxes(x[:, :, 0, :n_heads], 1, 2).reshape(
        b, n_heads // ATTN_HEADS, ATTN_HEADS, t // tm)
    attn_t = _attention(qa, ka, vt, stat(qn), stat(kn), stat(fe), head_dim, tm)
    y_prompt = _post(h1, gpool, sga, attn_t, p_prompt[0], w, tm, True)
    heads_last = lambda x: jnp.transpose(x.reshape(b, n_heads, head_dim, t), (0, 3, 1, 2))[None]

    step_major = lambda x: jnp.swapaxes(x, 0, 1).reshape(ns, x.shape[-1])
    seq_major = lambda x: jnp.swapaxes(x.reshape(n_new, bd, x.shape[-1]), 0, 1)
    hs1 = _ffn(step_major(x_sample), w["g_ffn1"], w["wg1"], w["wu1"], w["wd1"], ns)
    state = jnp.swapaxes(state_pool[0], 0, 1).reshape(n_pre * bd, pool_width)
    z_s, q_s, k_s, v_s, lf_s, fc_s, gpool_s, sga_s = _proj_sample(hs1, state, w, n_heads, n_new, bd)
    k_s, v_s, lf_s = seq_major(k_s), seq_major(v_s), seq_major(lf_s)
    new_page = lambda x: jnp.pad(jnp.swapaxes(x, 1, 2).astype(BF16), ((0, 0), (0, 0), (0, page - n_new)))
    negc = -jnp.swapaxes(seq_major(fc_s), 1, 2)
    negc = jnp.pad(negc, ((0, 0), (0, 0), (0, LANES - n_new)))
    upper = jnp.asarray(np.triu(np.ones((page, page), np.float32)), BF16)
    pages_t = lambda c: jnp.transpose(c[0], (0, 2, 3, 1)).reshape(n_pool_pages, aw, page)
    attn_s = _decode(page_table, seq_major(q_s), new_page(k_s), new_page(v_s), negc, upper,
                     pages_t(cache_k), pages_t(cache_v), jnp.swapaxes(cache_logf[0], 1, 2), n_heads)
    y_s = _post(hs1[None], gpool_s[None], sga_s[None], step_major(attn_s)[None],
                step_major(p_sample[0])[None], w, ns, False)
    y_sample = seq_major(y_s[0])
    pool_sample = jnp.concatenate([state_pool[0], seq_major(z_s)], axis=1)[:, -n_pre:]

    return (y_prompt, y_sample, heads_last(kt_p), heads_last(vt_p),
            lf_p[None], zlast[None, :, POOL_HALO - n_pre:],
            k_s.reshape(1, bd, n_new, n_heads, head_dim), v_s.reshape(1, bd, n_new, n_heads, head_dim),
            lf_s[None], pool_sample[None])
```

```python
import functools

import numpy as np
import jax
import jax.numpy as jnp
from jax import lax
from jax.experimental import pallas as pl
from jax.experimental.pallas import tpu as pltpu

F32 = jnp.float32
BF16 = jnp.bfloat16

RMS_EPS = 1e-6
MASKED = -1e30
POOL_WINDOWS = (2, 4, 8, 16)
POOL_HALO = 16

LANES = 128
HEAD_PAD = 128
V_ROWS = 80
F_PARTS = 3
FF_CHUNK = 768
TOKEN_TILE = 512
DECODE_PAGES = 8
DECODE_SLOTS = 4
ATTN_HEADS = 2
ATTN_GROUP = 2
VMEM_LIMIT = 56 * 1024 * 1024
LOG2E = 1.4426950408889634
NORM_SLACK = 1.01
SKIP_MARGIN = 140.0

_NT = (((1,), (1,)), ((), ()))
_TN = (((0,), (0,)), ((), ()))


def _rms(x, g):
    r = lax.rsqrt(jnp.mean(x * x, axis=-1, keepdims=True) + RMS_EPS)
    return x * r * g


def _dot(a, b):
    return jnp.dot(a, b, preferred_element_type=F32)


def _chunks(n, c):
    return [(s, min(c, n - s)) for s in range(0, n, c)]


def _split3(x):
    hi = x.astype(BF16).astype(F32)
    r = x - hi
    mid = r.astype(BF16).astype(F32)
    lo = (r - mid).astype(BF16).astype(F32)
    return hi, mid, lo


def _pack3(x, n):
    hi, mid, lo = _split3(x)
    return (hi + pltpu.roll(mid, n, 1) + pltpu.roll(lo, 2 * n, 1)).astype(BF16)


def _unpack3(c, n):
    return c + pltpu.roll(c, LANES - n, 1) + pltpu.roll(c, LANES - 2 * n, 1)


def _log_sigmoid(x):
    return jnp.minimum(x, 0.0) - jnp.log1p(jnp.exp(-jnp.abs(x)))


def _const_spec(shape):
    nd = len(shape)
    return pl.BlockSpec(shape, lambda *_: (0,) * nd, pipeline_mode=pl.Buffered(1))


def _params(sem):
    return pltpu.CompilerParams(dimension_semantics=sem, vmem_limit_bytes=VMEM_LIMIT)


def _ffn_value(x, g, wg_ref, wu_ref, wd_ref):
    u = _rms(x, g).astype(BF16)
    acc = None
    for s, n in _chunks(wg_ref.shape[1], FF_CHUNK):
        gate = _dot(u, wg_ref[:, s:s + n])
        up = _dot(u, wu_ref[:, s:s + n])
        a = (gate * jax.nn.sigmoid(gate) * up).astype(BF16)
        d = _dot(a, wd_ref[s:s + n, :])
        acc = d if acc is None else acc + d
    return x + 0.5 * acc


def _ffn_kernel(x_ref, g_ref, wg_ref, wu_ref, wd_ref, o_ref):
    o_ref[...] = _ffn_value(x_ref[...], g_ref[...], wg_ref, wu_ref, wd_ref)


def _ffn(x, g, wg, wu, wd, tm):
    n, d = x.shape
    return pl.pallas_call(
        _ffn_kernel,
        grid=(n // tm,),
        in_specs=[pl.BlockSpec((tm, d), lambda i: (i, 0)),
                  _const_spec(g.shape), _const_spec(wg.shape), _const_spec(wu.shape),
                  _const_spec(wd.shape)],
        out_specs=pl.BlockSpec((tm, d), lambda i: (i, 0)),
        out_shape=jax.ShapeDtypeStruct((n, d), F32),
        compiler_params=_params(("parallel",)),
        name="ffn",
    )(x, g, wg, wu, wd)


def _pool_branch(pooled_groups, wpg_ref):
    outs = [_dot(p.astype(BF16), wpg_ref[g]) for g, p in enumerate(pooled_groups)]
    return jnp.concatenate(outs, axis=1)


def _head_lane_mask(shape, n_heads):
    lane = lax.broadcasted_iota(jnp.int32, shape, 1)
    return lane < n_heads


def _proj_prompt_kernel(h_ref, g_ref, wpool_ref, wqa_ref, wka_ref, wkt_ref, wvt_ref,
                        wf_ref, bf_ref, wgp_ref, wga_ref, wpg_ref, pscale_ref, place_ref,
                        tri_ref, qones_ref, headsel_ref,
                        qa_ref, ka_ref, vt_ref, kt_ref, vto_ref, lf_ref, gpool_ref, sga_ref,
                        zlast_ref, qn_ref, kn_ref, fe_ref, zext_ref, fcarry_ref, *, n_heads):
    i = pl.program_id(1)
    tm = h_ref.shape[0]
    gw = LANES

    @pl.when(i == 0)
    def _():
        zext_ref[0:POOL_HALO, :] = jnp.zeros((POOL_HALO, zext_ref.shape[1]), F32)
        fcarry_ref[...] = jnp.zeros(fcarry_ref.shape, F32)

    u = _rms(h_ref[...], g_ref[...]).astype(BF16)

    z = _dot(u, wpool_ref[...])
    zext_ref[POOL_HALO:POOL_HALO + tm, :] = z
    row = lax.broadcasted_iota(jnp.int32, (tm, gw), 0) + i * tm
    pooled = []
    for g, w in enumerate(POOL_WINDOWS):
        sl = slice(g * gw, (g + 1) * gw)
        zg = z[:, sl]
        acc = zg
        for j in range(1, w):
            acc = acc + zext_ref[POOL_HALO - j:POOL_HALO - j + tm, sl]
        cnt = jnp.minimum(row + 1, w).astype(F32)
        pooled.append(acc / cnt - zg)
    zlast_ref[...] = zext_ref[tm:tm + POOL_HALO, :]
    zext_ref[0:POOL_HALO, :] = zext_ref[tm:tm + POOL_HALO, :]
    branch_pool = _pool_branch(pooled, wpg_ref) * pscale_ref[...]
    gpool_ref[...] = (jax.nn.sigmoid(_dot(u, wgp_ref[...])) * branch_pool).astype(BF16)
    sga_ref[...] = jax.nn.sigmoid(_dot(u, wga_ref[...])).astype(BF16)

    lane_ok = _head_lane_mask((tm, LANES), n_heads)
    logf = jnp.where(lane_ok, _log_sigmoid(_dot(u, wf_ref[...]) + bf_ref[...]), 0.0)
    lf_ref[...] = logf[:, 0:n_heads]
    csum = _unpack3(_dot(tri_ref[...], _pack3(logf, n_heads)), n_heads)
    fcum = jnp.where(lane_ok, csum + fcarry_ref[...], 0.0)
    fcarry_ref[...] = fcum[tm - 1:tm, :]
    fcum2 = fcum * LOG2E
    fe_ref[...] = fcum2[tm - 1:tm, :]

    qa = (_dot(u, wqa_ref[...]) + qones_ref[...]).astype(BF16)
    ka = (_dot(u, wka_ref[...]) + _dot(_pack3(fcum2, n_heads), place_ref[...])).astype(BF16)
    for x, n_ref in ((qa, qn_ref), (ka, kn_ref)):
        xf = x.astype(F32)
        sq = (xf * xf * NORM_SLACK).astype(BF16)
        n_ref[...] = jnp.max(_dot(sq, headsel_ref[...]), axis=0, keepdims=True)
    kt = lax.dot_general(wkt_ref[...], u, _NT, preferred_element_type=F32)
    vt = lax.dot_general(wvt_ref[...], u, _NT, preferred_element_type=F32)
    kt_ref[...] = kt
    vto_ref[...] = vt
    head_dim = kt.shape[0] // n_heads
    rid = lax.broadcasted_iota(jnp.int32, (V_ROWS - head_dim, tm), 0)
    ones_rows = jnp.where(rid == 0, 1.0, 0.0).astype(BF16)
    for h in range(n_heads):
        qa_ref[h] = qa[:, h * HEAD_PAD:(h + 1) * HEAD_PAD]
        ka_ref[h] = ka[:, h * HEAD_PAD:(h + 1) * HEAD_PAD]
        vt_ref[h, 0:head_dim, :] = vt[h * head_dim:(h + 1) * head_dim, :].astype(BF16)
        vt_ref[h, head_dim:V_ROWS, :] = ones_rows


def _proj_prompt(h, w, tm, n_heads):
    b, t, d = h.shape
    aw = w["wkt"].shape[0]
    pw = w["wpool"].shape[1]
    nt = t // tm
    names = ["g_mix", "wpool", "wqa", "wka", "wkt", "wvt", "wf", "bf", "wgp", "wga", "wpg",
             "pscale", "place", "tri", "qones", "headsel"]
    consts = [w[n] for n in names]
    tok = lambda width: pl.BlockSpec((None, tm, width), lambda bi, i: (bi, i, 0))
    tok_t = lambda rows: pl.BlockSpec((None, rows, tm), lambda bi, i: (bi, 0, i))
    head = pl.BlockSpec((None, n_heads, tm, HEAD_PAD), lambda bi, i: (bi, 0, i, 0))
    stat = pl.BlockSpec((None, None, 1, LANES), lambda bi, i: (bi, i, 0, 0))
    stat_shape = jax.ShapeDtypeStruct((b, nt, 1, LANES), F32)
    out_shape = (
        jax.ShapeDtypeStruct((b, n_heads, t, HEAD_PAD), BF16),
        jax.ShapeDtypeStruct((b, n_heads, t, HEAD_PAD), BF16),
        jax.ShapeDtypeStruct((b, n_heads, V_ROWS, t), BF16),
        jax.ShapeDtypeStruct((b, aw, t), F32),
        jax.ShapeDtypeStruct((b, aw, t), F32),
        jax.ShapeDtypeStruct((b, t, n_heads), F32),
        jax.ShapeDtypeStruct((b, t, d), BF16),
        jax.ShapeDtypeStruct((b, t, d), BF16),
        jax.ShapeDtypeStruct((b, POOL_HALO, pw), F32),
        stat_shape, stat_shape, stat_shape,
    )
    out_specs = (
        head, head,
        pl.BlockSpec((None, n_heads, V_ROWS, tm), lambda bi, i: (bi, 0, 0, i)),
        tok_t(aw), tok_t(aw), tok(n_heads), tok(d), tok(d),
        pl.BlockSpec((None, POOL_HALO, pw), lambda bi, i: (bi, 0, 0)),
        stat, stat, stat,
    )
    return pl.pallas_call(
        functools.partial(_proj_prompt_kernel, n_heads=n_heads),
        grid=(b, nt),
        in_specs=[tok(d)] + [_const_spec(c.shape) for c in consts],
        out_specs=out_specs,
        out_shape=out_shape,
        scratch_shapes=[pltpu.VMEM((tm + POOL_HALO, pw), F32), pltpu.VMEM((1, LANES), F32)],
        compiler_params=_params(("arbitrary", "arbitrary")),
        name="proj_prompt",
    )(h, *consts)


def _proj_sample_kernel(h_ref, state_ref, g_ref, wpool_ref, wq_ref, wk_ref, wv_ref, wf_ref,
                        bf_ref, wgp_ref, wga_ref, wpg_ref, pscale_ref,
                        z_ref, q_ref, k_ref, v_ref, lf_ref, fc_ref, gpool_ref, sga_ref,
                        *, n_heads, n_new, n_seq):
    gw = LANES
    n_pre = state_ref.shape[0] // n_seq
    u = _rms(h_ref[...], g_ref[...]).astype(BF16)

    z = _dot(u, wpool_ref[...])
    z_ref[...] = z

    def ext_rows(r, sl):
        if r < n_pre:
            return state_ref[r * n_seq:(r + 1) * n_seq, sl]
        return z[(r - n_pre) * n_seq:(r - n_pre + 1) * n_seq, sl]

    pooled = []
    for g, w in enumerate(POOL_WINDOWS):
        sl = slice(g * gw, (g + 1) * gw)
        steps = []
        for s in range(n_new):
            r = n_pre + s
            acc = ext_rows(r, sl)
            for j in range(1, w):
                if r - j >= 0:
                    acc = acc + ext_rows(r - j, sl)
            steps.append(acc / float(min(r + 1, w)) - ext_rows(r, sl))
        pooled.append(jnp.concatenate(steps, axis=0))
    branch_pool = _pool_branch(pooled, wpg_ref) * pscale_ref[...]
    gpool_ref[...] = (jax.nn.sigmoid(_dot(u, wgp_ref[...])) * branch_pool).astype(BF16)
    sga_ref[...] = jax.nn.sigmoid(_dot(u, wga_ref[...])).astype(BF16)

    tm = h_ref.shape[0]
    lane_ok = _head_lane_mask((tm, LANES), n_heads)
    logf = jnp.where(lane_ok, _log_sigmoid(_dot(u, wf_ref[...]) + bf_ref[...]), 0.0)
    lf_ref[...] = logf[:, 0:n_heads]
    run = None
    sums = []
    for s in range(n_new):
        blk = logf[s * n_seq:(s + 1) * n_seq, :]
        run = blk if run is None else run + blk
        sums.append(run)
    fc_ref[...] = jnp.concatenate(sums, axis=0)[:, 0:n_heads]

    q_ref[...] = _dot(u, wq_ref[...]).astype(BF16)
    k_ref[...] = _dot(u, wk_ref[...])
    v_ref[...] = _dot(u, wv_ref[...])


def _proj_sample(h, state, w, n_heads, n_new, n_seq):
    n, d = h.shape
    aw = w["wk"].shape[1]
    pw = w["wpool"].shape[1]
    names = ["g_mix", "wpool", "wq", "wk", "wv", "wf", "bf", "wgp", "wga", "wpg", "pscale"]
    consts = [w[n_] for n_ in names]
    full = lambda shape: pl.BlockSpec(shape, lambda i: (0,) * len(shape))
    out_shape = (
        jax.ShapeDtypeStruct((n, pw), F32),
        jax.ShapeDtypeStruct((n, aw), BF16),
        jax.ShapeDtypeStruct((n, aw), F32),
        jax.ShapeDtypeStruct((n, aw), F32),
        jax.ShapeDtypeStruct((n, n_heads), F32),
        jax.ShapeDtypeStruct((n, n_heads), F32),
        jax.ShapeDtypeStruct((n, d), BF16),
        jax.ShapeDtypeStruct((n, d), BF16),
    )
    return pl.pallas_call(
        functools.partial(_proj_sample_kernel, n_heads=n_heads, n_new=n_new, n_seq=n_seq),
        grid=(1,),
        in_specs=[full(h.shape), full(state.shape)] + [_const_spec(c.shape) for c in consts],
        out_specs=tuple(full(s.shape) for s in out_shape),
        out_shape=out_shape,
        compiler_params=_params(("arbitrary",)),
        name="proj_sample",
    )(h, state, *consts)


def _attn_kernel(q_ref, k_ref, vt_ref, qn_ref, kn_ref, fe_ref, o_ref, *, head_dim):
    i = pl.program_id(2)
    nh, tq, _ = q_ref.shape
    tk = tq
    nt = fe_ref.shape[1]
    qs = [q_ref[h] for h in range(nh)]

    def scores(h, j):
        start = pl.multiple_of(j * tk, tk)
        return lax.dot_general(k_ref[h, pl.ds(start, tk), :], qs[h], _NT, preferred_element_type=F32)

    def absorb(h, j, s, m, acc):
        start = pl.multiple_of(j * tk, tk)
        m_new = jnp.maximum(m, jnp.max(s, axis=0, keepdims=True))
        alpha = jnp.exp2(m - m_new)
        p = jnp.exp2(s - m_new).astype(BF16)
        acc = alpha * acc + _dot(vt_ref[h, :, pl.ds(start, tk)], p)
        return m_new, acc

    kpos = lax.broadcasted_iota(jnp.int32, (tk, tq), 0)
    qpos = lax.broadcasted_iota(jnp.int32, (tk, tq), 1)
    diag = [jnp.where(kpos <= qpos, scores(h, i), MASKED) for h in range(nh)]
    state = []
    for h in range(nh):
        state += list(absorb(h, i, diag[h], jnp.full((1, tq), MASKED, F32),
                             jnp.zeros((V_ROWS, tq), F32)))

    lane = lax.broadcasted_iota(jnp.int32, (1, nt), 1)
    n_back = None
    for h in range(nh):
        m_min = jnp.min(state[2 * h], axis=1, keepdims=True)
        q2 = jnp.max(jnp.where(lane == i, qn_ref[h:h + 1, :], 0.0), axis=1, keepdims=True)
        k2 = jnp.max(kn_ref[h:h + 1, :], axis=1, keepdims=True)
        bound = jnp.sqrt(q2 * k2) - fe_ref[h:h + 1, :]
        need = (lane < i) & (bound > m_min - SKIP_MARGIN)
        n_h = jnp.sum(need.astype(jnp.int32))
        n_back = n_h if n_back is None else jnp.maximum(n_back, n_h)

    def make_body(group):
        def body(jj, c):
            c = list(c)
            first = c[-1]
            tiles = [first - jj * group - g for g in range(group)]
            ss = [[scores(h, j) for h in range(nh)] for j in tiles]
            for g, j in enumerate(tiles):
                for h in range(nh):
                    c[2 * h], c[2 * h + 1] = absorb(h, j, ss[g][h], c[2 * h], c[2 * h + 1])
            return tuple(c)
        return body

    group = ATTN_GROUP
    odd = n_back % group
    state = lax.fori_loop(0, odd, make_body(1), tuple(state) + (i - 1,))
    state = lax.fori_loop(0, n_back // group, make_body(group), state[:-1] + (i - 1 - odd,))
    for h in range(nh):
        acc = state[2 * h + 1]
        o_ref[h * head_dim:(h + 1) * head_dim, :] = (
            acc[0:head_dim, :] / acc[head_dim:head_dim + 1, :]).astype(BF16)


def _attention(qa, ka, vt, qn, kn, fe, head_dim, tq):
    b, nh, t, _ = qa.shape
    g = ATTN_HEADS
    nt = t // tq
    stat = pl.BlockSpec((None, None, g, nt), lambda bi, h, i: (bi, h, 0, 0))
    return pl.pallas_call(
        functools.partial(_attn_kernel, head_dim=head_dim),
        grid=(b, nh // g, nt),
        in_specs=[pl.BlockSpec((None, g, tq, HEAD_PAD), lambda bi, h, i: (bi, h, i, 0)),
                  pl.BlockSpec((None, g, t, HEAD_PAD), lambda bi, h, i: (bi, h, 0, 0)),
                  pl.BlockSpec((None, g, V_ROWS, t), lambda bi, h, i: (bi, h, 0, 0)),
                  stat, stat, stat],
        out_specs=pl.BlockSpec((None, g * head_dim, tq), lambda bi, h, i: (bi, h, i)),
        out_shape=jax.ShapeDtypeStruct((b, nh * head_dim, t), BF16),
        compiler_params=_params(("parallel", "parallel", "arbitrary")),
        name="attn_prompt",
    )(qa, ka, vt, qn, kn, fe)


def _decode_kernel(pt_ref, q_ref, kn_ref, vn_ref, negc_ref, upper_ref, ck_hbm, cv_hbm, cl_hbm,
                   o_ref, kbuf, vbuf, lbuf, sems, kb_ref, vb_ref, m_ref, l_ref, acc_ref, fcar_ref,
                   *, n_heads, n_new):
    slots, pp, width, page = kbuf.shape
    g = pl.program_id(1)
    step = pl.program_id(0) * pl.num_programs(1) + g
    n_steps = pl.num_programs(0) * pl.num_programs(1)
    head_dim = width // n_heads
    rows = n_new * n_heads

    def copies(slot, page_ids):
        return [pltpu.make_async_copy(hbm.at[pid], buf.at[slot, i], sems.at[slot, k])
                for i, pid in enumerate(page_ids)
                for k, (hbm, buf) in enumerate(((ck_hbm, kbuf), (cv_hbm, vbuf), (cl_hbm, lbuf)))]

    def start(slot, first):
        for c in copies(slot, [pt_ref[first + i] for i in range(pp)]):
            c.start()

    def wait(slot):
        for c in copies(slot, [0] * pp):
            c.wait()

    @pl.when(step == 0)
    def _():
        for s in range(slots - 1):
            start(s, s * pp)

    @pl.when(g == 0)
    def _():
        m_ref[...] = jnp.full(m_ref.shape, MASKED, F32)
        l_ref[...] = jnp.zeros(l_ref.shape, F32)
        acc_ref[...] = jnp.zeros(acc_ref.shape, F32)
        fcar_ref[...] = jnp.zeros(fcar_ref.shape, F32)

    q = q_ref[...].astype(F32)
    qrows = jnp.concatenate([jnp.broadcast_to(q[s:s + 1, :], (n_heads, width)) for s in range(n_new)],
                            axis=0)
    rid = lax.broadcasted_iota(jnp.int32, (rows, width), 0)
    cid = lax.broadcasted_iota(jnp.int32, (rows, width), 1)
    own = (cid // head_dim) == (rid % n_heads)
    qbd = jnp.where(own, qrows, 0.0).astype(BF16)

    def update(s, vals_t):
        m_new = jnp.maximum(m_ref[...], jnp.max(s, axis=1, keepdims=True))
        alpha = jnp.exp(m_ref[...] - m_new)
        p = jnp.exp(s - m_new)
        l_ref[...] = alpha * l_ref[...] + jnp.sum(p, axis=1, keepdims=True)
        pv = lax.dot_general(p.astype(BF16), vals_t, _NT, preferred_element_type=F32)
        acc_ref[...] = alpha * acc_ref[...] + pv
        m_ref[...] = m_new

    def consume(slot):
        for i in range(pp):
            kb_ref[:, i * page:(i + 1) * page] = kbuf[slot, i].astype(BF16)
            vb_ref[:, i * page:(i + 1) * page] = vbuf[slot, i].astype(BF16)
        x = jnp.concatenate([lbuf[slot, i] for i in range(pp)], axis=0)
        parts = jnp.concatenate(_split3(x), axis=0).astype(BF16)
        c = _dot(parts, upper_ref[...])
        nr = pp * n_heads
        fin = c[0:nr] + c[nr:2 * nr] + c[2 * nr:3 * nr]
        carry = fcar_ref[...]
        biases = []
        for i in range(pp):
            fi = fin[i * n_heads:(i + 1) * n_heads, :]
            biases.append(jnp.concatenate([-(fi + carry)] * n_new, axis=0))
            carry = carry + jnp.broadcast_to(fi[:, page - 1:page], carry.shape)
        fcar_ref[...] = carry
        update(_dot(qbd, kb_ref[...]) + jnp.concatenate(biases, axis=1), vb_ref[...])

    first = step * (slots * pp)
    for s in range(slots):
        wait(s)
        ahead = first + (s + slots - 1) * pp
        if s == 0:
            start(slots - 1, ahead)
        else:
            @pl.when(step + 1 < n_steps)
            def _():
                start(s - 1, ahead)
        consume(s)

    @pl.when(g == pl.num_programs(1) - 1)
    def _():
        carry = fcar_ref[...]
        sn = _dot(qbd, kn_ref[...])
        bias = jnp.concatenate([negc_ref[...] - carry] * n_new, axis=0)
        r2 = lax.broadcasted_iota(jnp.int32, sn.shape, 0)
        c2 = lax.broadcasted_iota(jnp.int32, sn.shape, 1)
        ok = (c2 < n_new) & (c2 <= r2 // n_heads)
        update(jnp.where(ok, sn + bias, MASKED), vn_ref[...])
        out = jnp.where(own, acc_ref[...] / l_ref[...], 0.0)
        o_ref[...] = jnp.sum(out.reshape(n_new, n_heads, width), axis=1)


def _decode(page_table, q, kn_t, vn_t, negc, upper, cache_kt, cache_vt, cache_lft, n_heads):
    bd, n_new, width = q.shape
    n_pages = page_table.shape[1]
    page = cache_kt.shape[2]
    slots = DECODE_SLOTS
    pp = min(DECODE_PAGES, n_pages // slots)
    assert n_pages % (slots * pp) == 0
    rows = n_new * n_heads
    pt = page_table.reshape(-1)

    seq = lambda shape: pl.BlockSpec((None,) + shape, lambda b, g, pt_ref: (b, 0, 0))
    hbm = pl.BlockSpec(memory_space=pl.ANY)
    in_specs = [seq((n_new, width)), seq((width, page)), seq((width, page)), seq((n_heads, LANES)),
                pl.BlockSpec(upper.shape, lambda b, g, pt_ref: (0, 0)), hbm, hbm, hbm]
    grid_spec = pltpu.PrefetchScalarGridSpec(
        num_scalar_prefetch=1,
        grid=(bd, n_pages // (slots * pp)),
        in_specs=in_specs,
        out_specs=seq((n_new, width)),
        scratch_shapes=[pltpu.VMEM((slots, pp, width, page), F32),
                        pltpu.VMEM((slots, pp, width, page), F32),
                        pltpu.VMEM((slots, pp, n_heads, page), F32),
                        pltpu.SemaphoreType.DMA((slots, 3)),
                        pltpu.VMEM((width, pp * page), BF16), pltpu.VMEM((width, pp * page), BF16),
                        pltpu.VMEM((rows, 1), F32), pltpu.VMEM((rows, 1), F32),
                        pltpu.VMEM((rows, width), F32), pltpu.VMEM((n_heads, LANES), F32)],
    )
    return pl.pallas_call(
        functools.partial(_decode_kernel, n_heads=n_heads, n_new=n_new),
        grid_spec=grid_spec,
        out_shape=jax.ShapeDtypeStruct((bd, n_new, width), F32),
        compiler_params=_params(("arbitrary", "arbitrary")),
        name="attn_decode",
    )(pt, q, kn_t, vn_t, negc, upper, cache_kt, cache_vt, cache_lft)


def _post_kernel(h_ref, gpool_ref, sga_ref, attn_ref, p_ref, wab_ref, wout_ref, g2_ref, wg_ref,
                 wu_ref, wd_ref, gple_ref, wpg_ref, wple_ref, gfin_ref, y_ref, *, attn_transposed):
    if attn_transposed:
        branch_attn = lax.dot_general(attn_ref[...], wab_ref[...], _TN, preferred_element_type=F32)
    else:
        branch_attn = _dot(attn_ref[...].astype(BF16), wab_ref[...])
    merged = gpool_ref[...].astype(F32) + sga_ref[...].astype(F32) * branch_attn
    h = h_ref[...] + _dot(merged.astype(BF16), wout_ref[...])
    h = _ffn_value(h, g2_ref[...], wg_ref, wu_ref, wd_ref)
    gate = jax.nn.sigmoid(_dot(_rms(h, gple_ref[...]).astype(BF16), wpg_ref[...]))
    h = h + _dot(p_ref[...].astype(BF16), wple_ref[...]) * gate
    y_ref[...] = _rms(h, gfin_ref[...])


def _post(h, gpool, sga, attn, p, w, tm, attn_transposed):
    b, t, d = h.shape
    names = ["wab", "wout", "g_ffn2", "wg2", "wu2", "wd2", "g_ple", "wpgate", "wple", "g_final"]
    consts = [w[n] for n in names]
    tok = lambda width: pl.BlockSpec((None, tm, width), lambda bi, i: (bi, i, 0))
    if attn_transposed:
        attn_spec = pl.BlockSpec((None, attn.shape[1], tm), lambda bi, i: (bi, 0, i))
    else:
        attn_spec = tok(attn.shape[2])
    return pl.pallas_call(
        functools.partial(_post_kernel, attn_transposed=attn_transposed),
        grid=(b, t // tm),
        in_specs=[tok(d), tok(d), tok(d), attn_spec, tok(p.shape[2])]
                 + [_const_spec(c.shape) for c in consts],
        out_specs=tok(d),
        out_shape=jax.ShapeDtypeStruct((b, t, d), F32),
        compiler_params=_params(("parallel", "parallel")),
        name="post",
    )(h, gpool, sga, attn, p, *consts)


def _prep_weights(g_ffn1, w_ffn1_gate, w_ffn1_up, w_ffn1_down, g_mix, w_in, b_forget, w_pool_group,
                  pool_scale, w_attn_branch, w_out, g_ffn2, w_ffn2_gate, w_ffn2_up, w_ffn2_down,
                  g_ple, w_ple_gate, w_ple, g_final, n_heads, head_dim, pool_width, tm):
    d = w_in.shape[0]
    aw = n_heads * head_dim
    o = 0
    wpool = w_in[:, o:o + pool_width]; o += pool_width
    wq = w_in[:, o:o + aw] * (head_dim ** -0.5); o += aw
    wk = w_in[:, o:o + aw]; o += aw
    wv = w_in[:, o:o + aw]; o += aw
    wf = w_in[:, o:o + n_heads]; o += n_heads
    wgp = w_in[:, o:o + d]; o += d
    wga = w_in[:, o:o + d]

    def per_head_pad(x):
        x = x.reshape(d, n_heads, head_dim)
        return jnp.pad(x, ((0, 0), (0, 0), (0, HEAD_PAD - head_dim))).reshape(d, n_heads * HEAD_PAD)

    place = np.zeros((LANES, n_heads * HEAD_PAD), np.float32)
    qones = np.zeros((1, n_heads * HEAD_PAD), np.float32)
    headsel = np.zeros((n_heads * HEAD_PAD, LANES), np.float32)
    for h in range(n_heads):
        headsel[h * HEAD_PAD:h * HEAD_PAD + head_dim, h] = 1.0
        for part in range(F_PARTS):
            place[part * n_heads + h, h * HEAD_PAD + head_dim + part] = -1.0
            qones[0, h * HEAD_PAD + head_dim + part] = 1.0
    row = lambda x: x.reshape(1, -1).astype(F32)
    bf = lambda x: x.astype(BF16)
    return {
        "g_ffn1": row(g_ffn1), "wg1": bf(w_ffn1_gate), "wu1": bf(w_ffn1_up), "wd1": bf(w_ffn1_down),
        "g_mix": row(g_mix), "wpool": bf(wpool), "wq": bf(wq), "wk": bf(wk), "wv": bf(wv),
        "wqa": bf(per_head_pad(wq * LOG2E)), "wka": bf(per_head_pad(wk)),
        "wkt": bf(wk.T), "wvt": bf(wv.T),
        "wf": bf(jnp.pad(wf, ((0, 0), (0, LANES - n_heads)))),
        "bf": jnp.pad(row(b_forget), ((0, 0), (0, LANES - n_heads))),
        "wgp": bf(wgp), "wga": bf(wga), "wpg": bf(w_pool_group), "pscale": row(pool_scale),
        "place": jnp.asarray(place, BF16), "qones": jnp.asarray(qones, F32),
        "headsel": jnp.asarray(headsel, BF16),
        "tri": jnp.asarray(np.tril(np.ones((tm, tm), np.float32)), BF16),
        "wab": bf(w_attn_branch), "wout": bf(w_out),
        "g_ffn2": row(g_ffn2), "wg2": bf(w_ffn2_gate), "wu2": bf(w_ffn2_up), "wd2": bf(w_ffn2_down),
        "g_ple": row(g_ple), "wpgate": bf(w_ple_gate), "wple": bf(w_ple), "g_final": row(g_final),
    }


def kernel(x_prompt, x_sample, cache_k, cache_v, cache_logf, state_pool, page_table, p_prompt, p_sample, g_ffn1, w_ffn1_gate, w_ffn1_up, w_ffn1_down, g_mix, w_in, b_forget, w_pool_group, pool_scale, w_attn_branch, w_out, g_ffn2, w_ffn2_gate, w_ffn2_up, w_ffn2_down, g_ple, w_ple_gate, w_ple, g_final):
    depth = cache_k.shape[0]
    assert depth == 1, "one trunk layer"
    b, t, d = x_prompt.shape
    bd, n_new, _ = x_sample.shape
    _, n_pool_pages, page, n_heads, head_dim = cache_k.shape
    aw = n_heads * head_dim
    n_pre, pool_width = state_pool.shape[2], state_pool.shape[3]
    assert head_dim + F_PARTS <= HEAD_PAD and head_dim < V_ROWS and page == LANES
    assert pool_width == len(POOL_WINDOWS) * LANES and n_pre == POOL_WINDOWS[-1] - 1
    assert n_heads % ATTN_HEADS == 0
    tm = min(TOKEN_TILE, t)
    assert t // tm <= LANES
    ns = bd * n_new

    w = _prep_weights(g_ffn1[0], w_ffn1_gate[0], w_ffn1_up[0], w_ffn1_down[0], g_mix[0], w_in[0],
                      b_forget[0], w_pool_group[0], pool_scale[0], w_attn_branch[0], w_out[0],
                      g_ffn2[0], w_ffn2_gate[0], w_ffn2_up[0], w_ffn2_down[0], g_ple[0],
                      w_ple_gate[0], w_ple[0], g_final, n_heads, head_dim, pool_width, tm)

    h1 = _ffn(x_prompt.reshape(b * t, d), w["g_ffn1"], w["wg1"], w["wu1"], w["wd1"], tm)
    h1 = h1.reshape(b, t, d)
    qa, ka, vt, kt_p, vt_p, lf_p, gpool, sga, zlast, qn, kn, fe = _proj_prompt(h1, w, tm, n_heads)
    stat = lambda x: jnp.swapaxes(x[:, :, 0, :n_heads], 1, 2).reshape(
        b, n_heads // ATTN_HEADS, ATTN_HEADS, t // tm)
    attn_t = _attention(qa, ka, vt, stat(qn), stat(kn), stat(fe), head_dim, tm)
    y_prompt = _post(h1, gpool, sga, attn_t, p_prompt[0], w, tm, True)
    heads_last = lambda x: jnp.transpose(x.reshape(b, n_heads, head_dim, t), (0, 3, 1, 2))[None]

    step_major = lambda x: jnp.swapaxes(x, 0, 1).reshape(ns, x.shape[-1])
    seq_major = lambda x: jnp.swapaxes(x.reshape(n_new, bd, x.shape[-1]), 0, 1)
    hs1 = _ffn(step_major(x_sample), w["g_ffn1"], w["wg1"], w["wu1"], w["wd1"], ns)
    state = jnp.swapaxes(state_pool[0], 0, 1).reshape(n_pre * bd, pool_width)
    z_s, q_s, k_s, v_s, lf_s, fc_s, gpool_s, sga_s = _proj_sample(hs1, state, w, n_heads, n_new, bd)
    k_s, v_s, lf_s = seq_major(k_s), seq_major(v_s), seq_major(lf_s)
    new_page = lambda x: jnp.pad(jnp.swapaxes(x, 1, 2).astype(BF16), ((0, 0), (0, 0), (0, page - n_new)))
    negc = -jnp.swapaxes(seq_major(fc_s), 1, 2)
    negc = jnp.pad(negc, ((0, 0), (0, 0), (0, LANES - n_new)))
    upper = jnp.asarray(np.triu(np.ones((page, page), np.float32)), BF16)
    pages_t = lambda c: jnp.transpose(c[0], (0, 2, 3, 1)).reshape(n_pool_pages, aw, page)
    attn_s = _decode(page_table, seq_major(q_s), new_page(k_s), new_page(v_s), negc, upper,
                     pages_t(cache_k), pages_t(cache_v), jnp.swapaxes(cache_logf[0], 1, 2), n_heads)
    y_s = _post(hs1[None], gpool_s[None], sga_s[None], step_major(attn_s)[None],
                step_major(p_sample[0])[None], w, ns, False)
    y_sample = seq_major(y_s[0])
    pool_sample = jnp.concatenate([state_pool[0], seq_major(z_s)], axis=1)[:, -n_pre:]

    return (y_prompt, y_sample, heads_last(kt_p), heads_last(vt_p),
            lf_p[None], zlast[None, :, POOL_HALO - n_pre:],
            k_s.reshape(1, bd, n_new, n_heads, head_dim), v_s.reshape(1, bd, n_new, n_heads, head_dim),
            lf_s[None], pool_sample[None])
```

```python
import functools

import numpy as np
import jax
import jax.numpy as jnp
from jax import lax
from jax.experimental import pallas as pl
from jax.experimental.pallas import tpu as pltpu

F32 = jnp.float32
BF16 = jnp.bfloat16

RMS_EPS = 1e-6
MASKED = -1e30
POOL_WINDOWS = (2, 4, 8, 16)
POOL_HALO = 16

LANES = 128
HEAD_PAD = 128
V_ROWS = 80
F_PARTS = 3
FF_CHUNK = 768
TOKEN_TILE = 512
DECODE_PAGES = 8
DECODE_SLOTS = 4
ATTN_HEADS = 2
ATTN_GROUPS = (1, 2)
VMEM_LIMIT = 56 * 1024 * 1024
LOG2E = 1.4426950408889634
NORM_SLACK = 1.01
SKIP_MARGIN = 140.0

_NT = (((1,), (1,)), ((), ()))
_TN = (((0,), (0,)), ((), ()))


def _rms(x, g):
    r = lax.rsqrt(jnp.mean(x * x, axis=-1, keepdims=True) + RMS_EPS)
    return x * r * g


def _dot(a, b):
    return jnp.dot(a, b, preferred_element_type=F32)


def _chunks(n, c):
    return [(s, min(c, n - s)) for s in range(0, n, c)]


def _split3(x):
    hi = x.astype(BF16).astype(F32)
    r = x - hi
    mid = r.astype(BF16).astype(F32)
    lo = (r - mid).astype(BF16).astype(F32)
    return hi, mid, lo


def _pack3(x, n):
    hi, mid, lo = _split3(x)
    return (hi + pltpu.roll(mid, n, 1) + pltpu.roll(lo, 2 * n, 1)).astype(BF16)


def _unpack3(c, n):
    return c + pltpu.roll(c, LANES - n, 1) + pltpu.roll(c, LANES - 2 * n, 1)


def _log_sigmoid(x):
    return jnp.minimum(x, 0.0) - jnp.log1p(jnp.exp(-jnp.abs(x)))


def _const_spec(shape):
    nd = len(shape)
    return pl.BlockSpec(shape, lambda *_: (0,) * nd, pipeline_mode=pl.Buffered(1))


def _params(sem):
    return pltpu.CompilerParams(dimension_semantics=sem, vmem_limit_bytes=VMEM_LIMIT)


def _ffn_value(x, g, wg_ref, wu_ref, wd_ref):
    u = _rms(x, g).astype(BF16)
    acc = None
    for s, n in _chunks(wg_ref.shape[1], FF_CHUNK):
        gate = _dot(u, wg_ref[:, s:s + n])
        up = _dot(u, wu_ref[:, s:s + n])
        a = (gate * jax.nn.sigmoid(gate) * up).astype(BF16)
        d = _dot(a, wd_ref[s:s + n, :])
        acc = d if acc is None else acc + d
    return x + 0.5 * acc


def _ffn_kernel(x_ref, g_ref, wg_ref, wu_ref, wd_ref, o_ref):
    o_ref[...] = _ffn_value(x_ref[...], g_ref[...], wg_ref, wu_ref, wd_ref)


def _ffn(x, g, wg, wu, wd, tm):
    n, d = x.shape
    return pl.pallas_call(
        _ffn_kernel,
        grid=(n // tm,),
        in_specs=[pl.BlockSpec((tm, d), lambda i: (i, 0)),
                  _const_spec(g.shape), _const_spec(wg.shape), _const_spec(wu.shape),
                  _const_spec(wd.shape)],
        out_specs=pl.BlockSpec((tm, d), lambda i: (i, 0)),
        out_shape=jax.ShapeDtypeStruct((n, d), F32),
        compiler_params=_params(("parallel",)),
        name="ffn",
    )(x, g, wg, wu, wd)


def _pool_branch(pooled_groups, wpg_ref):
    outs = [_dot(p.astype(BF16), wpg_ref[g]) for g, p in enumerate(pooled_groups)]
    return jnp.concatenate(outs, axis=1)


def _head_lane_mask(shape, n_heads):
    lane = lax.broadcasted_iota(jnp.int32, shape, 1)
    return lane < n_heads


def _proj_prompt_kernel(h_ref, g_ref, wpool_ref, wqa_ref, wka_ref, wkt_ref, wvt_ref,
                        wf_ref, bf_ref, wgp_ref, wga_ref, wpg_ref, pscale_ref, place_ref,
                        tri_ref, qones_ref, headsel_ref,
                        qa_ref, ka_ref, vt_ref, kt_ref, vto_ref, lf_ref, gpool_ref, sga_ref,
                        zlast_ref, qn_ref, kn_ref, fe_ref, zext_ref, fcarry_ref, *, n_heads):
    i = pl.program_id(1)
    tm = h_ref.shape[0]
    gw = LANES

    @pl.when(i == 0)
    def _():
        zext_ref[0:POOL_HALO, :] = jnp.zeros((POOL_HALO, zext_ref.shape[1]), F32)
        fcarry_ref[...] = jnp.zeros(fcarry_ref.shape, F32)

    u = _rms(h_ref[...], g_ref[...]).astype(BF16)

    z = _dot(u, wpool_ref[...])
    zext_ref[POOL_HALO:POOL_HALO + tm, :] = z
    row = lax.broadcasted_iota(jnp.int32, (tm, gw), 0) + i * tm
    pooled = []
    for g, w in enumerate(POOL_WINDOWS):
        sl = slice(g * gw, (g + 1) * gw)
        zg = z[:, sl]
        acc = zg
        for j in range(1, w):
            acc = acc + zext_ref[POOL_HALO - j:POOL_HALO - j + tm, sl]
        cnt = jnp.minimum(row + 1, w).astype(F32)
        pooled.append(acc / cnt - zg)
    zlast_ref[...] = zext_ref[tm:tm + POOL_HALO, :]
    zext_ref[0:POOL_HALO, :] = zext_ref[tm:tm + POOL_HALO, :]
    branch_pool = _pool_branch(pooled, wpg_ref) * pscale_ref[...]
    gpool_ref[...] = (jax.nn.sigmoid(_dot(u, wgp_ref[...])) * branch_pool).astype(BF16)
    sga_ref[...] = jax.nn.sigmoid(_dot(u, wga_ref[...])).astype(BF16)

    lane_ok = _head_lane_mask((tm, LANES), n_heads)
    logf = jnp.where(lane_ok, _log_sigmoid(_dot(u, wf_ref[...]) + bf_ref[...]), 0.0)
    lf_ref[...] = logf[:, 0:n_heads]
    csum = _unpack3(_dot(tri_ref[...], _pack3(logf, n_heads)), n_heads)
    fcum = jnp.where(lane_ok, csum + fcarry_ref[...], 0.0)
    fcarry_ref[...] = fcum[tm - 1:tm, :]
    fcum2 = fcum * LOG2E
    fe_ref[...] = fcum2[tm - 1:tm, :]

    qa = (_dot(u, wqa_ref[...]) + qones_ref[...]).astype(BF16)
    ka = (_dot(u, wka_ref[...]) + _dot(_pack3(fcum2, n_heads), place_ref[...])).astype(BF16)
    for x, n_ref in ((qa, qn_ref), (ka, kn_ref)):
        xf = x.astype(F32)
        sq = (xf * xf * NORM_SLACK).astype(BF16)
        n_ref[...] = jnp.max(_dot(sq, headsel_ref[...]), axis=0, keepdims=True)
    kt = lax.dot_general(wkt_ref[...], u, _NT, preferred_element_type=F32)
    vt = lax.dot_general(wvt_ref[...], u, _NT, preferred_element_type=F32)
    kt_ref[...] = kt
    vto_ref[...] = vt
    head_dim = kt.shape[0] // n_heads
    rid = lax.broadcasted_iota(jnp.int32, (V_ROWS - head_dim, tm), 0)
    ones_rows = jnp.where(rid == 0, 1.0, 0.0).astype(BF16)
    for h in range(n_heads):
        qa_ref[h] = qa[:, h * HEAD_PAD:(h + 1) * HEAD_PAD]
        ka_ref[h] = ka[:, h * HEAD_PAD:(h + 1) * HEAD_PAD]
        vt_ref[h, 0:head_dim, :] = vt[h * head_dim:(h + 1) * head_dim, :].astype(BF16)
        vt_ref[h, head_dim:V_ROWS, :] = ones_rows


def _proj_prompt(h, w, tm, n_heads):
    b, t, d = h.shape
    aw = w["wkt"].shape[0]
    pw = w["wpool"].shape[1]
    nt = t // tm
    names = ["g_mix", "wpool", "wqa", "wka", "wkt", "wvt", "wf", "bf", "wgp", "wga", "wpg",
             "pscale", "place", "tri", "qones", "headsel"]
    consts = [w[n] for n in names]
    tok = lambda width: pl.BlockSpec((None, tm, width), lambda bi, i: (bi, i, 0))
    tok_t = lambda rows: pl.BlockSpec((None, rows, tm), lambda bi, i: (bi, 0, i))
    head = pl.BlockSpec((None, n_heads, tm, HEAD_PAD), lambda bi, i: (bi, 0, i, 0))
    stat = pl.BlockSpec((None, None, 1, LANES), lambda bi, i: (bi, i, 0, 0))
    stat_shape = jax.ShapeDtypeStruct((b, nt, 1, LANES), F32)
    out_shape = (
        jax.ShapeDtypeStruct((b, n_heads, t, HEAD_PAD), BF16),
        jax.ShapeDtypeStruct((b, n_heads, t, HEAD_PAD), BF16),
        jax.ShapeDtypeStruct((b, n_heads, V_ROWS, t), BF16),
        jax.ShapeDtypeStruct((b, aw, t), F32),
        jax.ShapeDtypeStruct((b, aw, t), F32),
        jax.ShapeDtypeStruct((b, t, n_heads), F32),
        jax.ShapeDtypeStruct((b, t, d), BF16),
        jax.ShapeDtypeStruct((b, t, d), BF16),
        jax.ShapeDtypeStruct((b, POOL_HALO, pw), F32),
        stat_shape, stat_shape, stat_shape,
    )
    out_specs = (
        head, head,
        pl.BlockSpec((None, n_heads, V_ROWS, tm), lambda bi, i: (bi, 0, 0, i)),
        tok_t(aw), tok_t(aw), tok(n_heads), tok(d), tok(d),
        pl.BlockSpec((None, POOL_HALO, pw), lambda bi, i: (bi, 0, 0)),
        stat, stat, stat,
    )
    return pl.pallas_call(
        functools.partial(_proj_prompt_kernel, n_heads=n_heads),
        grid=(b, nt),
        in_specs=[tok(d)] + [_const_spec(c.shape) for c in consts],
        out_specs=out_specs,
        out_shape=out_shape,
        scratch_shapes=[pltpu.VMEM((tm + POOL_HALO, pw), F32), pltpu.VMEM((1, LANES), F32)],
        compiler_params=_params(("arbitrary", "arbitrary")),
        name="proj_prompt",
    )(h, *consts)


def _proj_sample_kernel(h_ref, state_ref, g_ref, wpool_ref, wq_ref, wk_ref, wv_ref, wf_ref,
                        bf_ref, wgp_ref, wga_ref, wpg_ref, pscale_ref,
                        z_ref, q_ref, k_ref, v_ref, lf_ref, fc_ref, gpool_ref, sga_ref,
                        *, n_heads, n_new, n_seq):
    gw = LANES
    n_pre = state_ref.shape[0] // n_seq
    u = _rms(h_ref[...], g_ref[...]).astype(BF16)

    z = _dot(u, wpool_ref[...])
    z_ref[...] = z

    def ext_rows(r, sl):
        if r < n_pre:
            return state_ref[r * n_seq:(r + 1) * n_seq, sl]
        return z[(r - n_pre) * n_seq:(r - n_pre + 1) * n_seq, sl]

    pooled = []
    for g, w in enumerate(POOL_WINDOWS):
        sl = slice(g * gw, (g + 1) * gw)
        steps = []
        for s in range(n_new):
            r = n_pre + s
            acc = ext_rows(r, sl)
            for j in range(1, w):
                if r - j >= 0:
                    acc = acc + ext_rows(r - j, sl)
            steps.append(acc / float(min(r + 1, w)) - ext_rows(r, sl))
        pooled.append(jnp.concatenate(steps, axis=0))
    branch_pool = _pool_branch(pooled, wpg_ref) * pscale_ref[...]
    gpool_ref[...] = (jax.nn.sigmoid(_dot(u, wgp_ref[...])) * branch_pool).astype(BF16)
    sga_ref[...] = jax.nn.sigmoid(_dot(u, wga_ref[...])).astype(BF16)

    tm = h_ref.shape[0]
    lane_ok = _head_lane_mask((tm, LANES), n_heads)
    logf = jnp.where(lane_ok, _log_sigmoid(_dot(u, wf_ref[...]) + bf_ref[...]), 0.0)
    lf_ref[...] = logf[:, 0:n_heads]
    run = None
    sums = []
    for s in range(n_new):
        blk = logf[s * n_seq:(s + 1) * n_seq, :]
        run = blk if run is None else run + blk
        sums.append(run)
    fc_ref[...] = jnp.concatenate(sums, axis=0)[:, 0:n_heads]

    q_ref[...] = _dot(u, wq_ref[...]).astype(BF16)
    k_ref[...] = _dot(u, wk_ref[...])
    v_ref[...] = _dot(u, wv_ref[...])


def _proj_sample(h, state, w, n_heads, n_new, n_seq):
    n, d = h.shape
    aw = w["wk"].shape[1]
    pw = w["wpool"].shape[1]
    names = ["g_mix", "wpool", "wq", "wk", "wv", "wf", "bf", "wgp", "wga", "wpg", "pscale"]
    consts = [w[n_] for n_ in names]
    full = lambda shape: pl.BlockSpec(shape, lambda i: (0,) * len(shape))
    out_shape = (
        jax.ShapeDtypeStruct((n, pw), F32),
        jax.ShapeDtypeStruct((n, aw), BF16),
        jax.ShapeDtypeStruct((n, aw), F32),
        jax.ShapeDtypeStruct((n, aw), F32),
        jax.ShapeDtypeStruct((n, n_heads), F32),
        jax.ShapeDtypeStruct((n, n_heads), F32),
        jax.ShapeDtypeStruct((n, d), BF16),
        jax.ShapeDtypeStruct((n, d), BF16),
    )
    return pl.pallas_call(
        functools.partial(_proj_sample_kernel, n_heads=n_heads, n_new=n_new, n_seq=n_seq),
        grid=(1,),
        in_specs=[full(h.shape), full(state.shape)] + [_const_spec(c.shape) for c in consts],
        out_specs=tuple(full(s.shape) for s in out_shape),
        out_shape=out_shape,
        compiler_params=_params(("arbitrary",)),
        name="proj_sample",
    )(h, state, *consts)


def _attn_kernel(q_ref, k_ref, vt_ref, qn_ref, kn_ref, fe_ref, o_ref, *, head_dim):
    i = pl.program_id(2)
    nh, tq, _ = q_ref.shape
    tk = tq
    nt = fe_ref.shape[1]
    qs = [q_ref[h] for h in range(nh)]

    def scores(h, j):
        start = pl.multiple_of(j * tk, tk)
        return lax.dot_general(k_ref[h, pl.ds(start, tk), :], qs[h], _NT, preferred_element_type=F32)

    def absorb(h, j, s, m, acc):
        start = pl.multiple_of(j * tk, tk)
        m_new = jnp.maximum(m, jnp.max(s, axis=0, keepdims=True))
        alpha = jnp.exp2(m - m_new)
        p = jnp.exp2(s - m_new).astype(BF16)
        acc = alpha * acc + _dot(vt_ref[h, :, pl.ds(start, tk)], p)
        return m_new, acc

    kpos = lax.broadcasted_iota(jnp.int32, (tk, tq), 0)
    qpos = lax.broadcasted_iota(jnp.int32, (tk, tq), 1)
    diag = [jnp.where(kpos <= qpos, scores(h, i), MASKED) for h in range(nh)]
    state = []
    for h in range(nh):
        state += list(absorb(h, i, diag[h], jnp.full((1, tq), MASKED, F32),
                             jnp.zeros((V_ROWS, tq), F32)))

    lane = lax.broadcasted_iota(jnp.int32, (1, nt), 1)
    n_back = None
    for h in range(nh):
        m_min = jnp.min(state[2 * h], axis=1, keepdims=True)
        q2 = jnp.max(jnp.where(lane == i, qn_ref[h:h + 1, :], 0.0), axis=1, keepdims=True)
        k2 = jnp.max(kn_ref[h:h + 1, :], axis=1, keepdims=True)
        bound = jnp.sqrt(q2 * k2) - fe_ref[h:h + 1, :]
        need = (lane < i) & (bound > m_min - SKIP_MARGIN)
        n_h = jnp.sum(need.astype(jnp.int32))
        n_back = n_h if n_back is None else jnp.maximum(n_back, n_h)

    def make_body(group):
        def body(jj, c):
            c = list(c)
            first = c[-1]
            tiles = [first - jj * group - g for g in range(group)]
            ss = [[scores(h, j) for h in range(nh)] for j in tiles]
            for g, j in enumerate(tiles):
                for h in range(nh):
                    c[2 * h], c[2 * h + 1] = absorb(h, j, ss[g][h], c[2 * h], c[2 * h + 1])
            return tuple(c)
        return body

    left, nearest, state = n_back, i - 1, tuple(state)
    for group in ATTN_GROUPS:
        bigger = [x for x in ATTN_GROUPS if x > group]
        trips = (left % min(bigger)) // group if bigger else left // group
        state = lax.fori_loop(0, trips, make_body(group), state + (nearest,))[:-1]
        left, nearest = left - trips * group, nearest - trips * group
    for h in range(nh):
        acc = state[2 * h + 1]
        o_ref[h * head_dim:(h + 1) * head_dim, :] = (
            acc[0:head_dim, :] / acc[head_dim:head_dim + 1, :]).astype(BF16)


def _attention(qa, ka, vt, qn, kn, fe, head_dim, tq):
    b, nh, t, _ = qa.shape
    g = ATTN_HEADS
    nt = t // tq
    stat = pl.BlockSpec((None, None, g, nt), lambda bi, h, i: (bi, h, 0, 0))
    return pl.pallas_call(
        functools.partial(_attn_kernel, head_dim=head_dim),
        grid=(b, nh // g, nt),
        in_specs=[pl.BlockSpec((None, g, tq, HEAD_PAD), lambda bi, h, i: (bi, h, i, 0)),
                  pl.BlockSpec((None, g, t, HEAD_PAD), lambda bi, h, i: (bi, h, 0, 0)),
                  pl.BlockSpec((None, g, V_ROWS, t), lambda bi, h, i: (bi, h, 0, 0)),
                  stat, stat, stat],
        out_specs=pl.BlockSpec((None, g * head_dim, tq), lambda bi, h, i: (bi, h, i)),
        out_shape=jax.ShapeDtypeStruct((b, nh * head_dim, t), BF16),
        compiler_params=_params(("parallel", "parallel", "arbitrary")),
        name="attn_prompt",
    )(qa, ka, vt, qn, kn, fe)


def _page_copies(ck_hbm, cv_hbm, cl_hbm, kbuf, vbuf, lbuf, sems, slot, page_ids):
    page = cl_hbm.shape[2]
    out = []
    for i, pid in enumerate(page_ids):
        lanes = slice(i * page, (i + 1) * page)
        out += [pltpu.make_async_copy(ck_hbm.at[pid], kbuf.at[slot, :, lanes], sems.at[slot, 0]),
                pltpu.make_async_copy(cv_hbm.at[pid], vbuf.at[slot, :, lanes], sems.at[slot, 1]),
                pltpu.make_async_copy(cl_hbm.at[pid], lbuf.at[slot, i], sems.at[slot, 2])]
    return out


def _decode_prologue(refs):
    pt_ref, ck_hbm, cv_hbm, cl_hbm, kbuf, vbuf, lbuf, sems = refs[0], *refs[6:9], *refs[10:14]
    slots = kbuf.shape[0]
    pp = kbuf.shape[2] // cl_hbm.shape[2]
    for s in range(slots - 1):
        for c in _page_copies(ck_hbm, cv_hbm, cl_hbm, kbuf, vbuf, lbuf, sems, s,
                              [pt_ref[s * pp + i] for i in range(pp)]):
            c.start()


def _decode_step(d, n_dsteps, seq, part, n_parts, refs, n_heads, n_new):
    (pt_ref, q_ref, kn_ref, vn_ref, negc_ref, upper_ref, ck_hbm, cv_hbm, cl_hbm, o_ref,
     kbuf, vbuf, lbuf, sems, m_ref, l_ref, acc_ref, fcar_ref) = refs
    slots, width, group_keys = kbuf.shape
    page = cl_hbm.shape[2]
    pp = group_keys // page
    head_dim = width // n_heads
    rows = n_new * n_heads

    def copies(slot, page_ids):
        return _page_copies(ck_hbm, cv_hbm, cl_hbm, kbuf, vbuf, lbuf, sems, slot, page_ids)

    def start(slot, first):
        for c in copies(slot, [pt_ref[first + i] for i in range(pp)]):
            c.start()

    def wait(slot):
        for c in copies(slot, [0] * pp):
            c.wait()

    if part == 0:
        m_ref[...] = jnp.full(m_ref.shape, MASKED, F32)
        l_ref[...] = jnp.zeros(l_ref.shape, F32)
        acc_ref[...] = jnp.zeros(acc_ref.shape, F32)
        fcar_ref[...] = jnp.zeros(fcar_ref.shape, F32)

    q = q_ref[seq].astype(F32)
    qrows = jnp.concatenate([jnp.broadcast_to(q[s:s + 1, :], (n_heads, width)) for s in range(n_new)],
                            axis=0)
    rid = lax.broadcasted_iota(jnp.int32, (rows, width), 0)
    cid = lax.broadcasted_iota(jnp.int32, (rows, width), 1)
    own = (cid // head_dim) == (rid % n_heads)
    qbd = jnp.where(own, qrows, 0.0)

    def update(s, vals_t):
        m_new = jnp.maximum(m_ref[...], jnp.max(s, axis=1, keepdims=True))
        alpha = jnp.exp(m_ref[...] - m_new)
        p = jnp.exp(s - m_new)
        l_ref[...] = alpha * l_ref[...] + jnp.sum(p, axis=1, keepdims=True)
        pv = lax.dot_general(p, vals_t, _NT, preferred_element_type=F32)
        acc_ref[...] = alpha * acc_ref[...] + pv
        m_ref[...] = m_new

    def consume(slot):
        x = jnp.concatenate([lbuf[slot, i] for i in range(pp)], axis=0)
        parts = jnp.concatenate(_split3(x), axis=0).astype(BF16)
        c = _dot(parts, upper_ref[...])
        nr = pp * n_heads
        fin = c[0:nr] + c[nr:2 * nr] + c[2 * nr:3 * nr]
        carry = fcar_ref[...]
        biases = []
        for i in range(pp):
            fi = fin[i * n_heads:(i + 1) * n_heads, :]
            biases.append(jnp.concatenate([-(fi + carry)] * n_new, axis=0))
            carry = carry + jnp.broadcast_to(fi[:, page - 1:page], carry.shape)
        fcar_ref[...] = carry
        update(_dot(qbd, kbuf[slot]) + jnp.concatenate(biases, axis=1), vbuf[slot])

    first = d * (slots * pp)
    for s in range(slots):
        wait(s)
        ahead = first + (s + slots - 1) * pp
        if s == 0:
            start(slots - 1, ahead)
        else:
            @pl.when(d + 1 < n_dsteps)
            def _():
                start(s - 1, ahead)
        consume(s)

    if part == n_parts - 1:
        carry = fcar_ref[...]
        sn = _dot(qbd, kn_ref[seq].astype(F32))
        bias = jnp.concatenate([negc_ref[seq] - carry] * n_new, axis=0)
        r2 = lax.broadcasted_iota(jnp.int32, sn.shape, 0)
        c2 = lax.broadcasted_iota(jnp.int32, sn.shape, 1)
        ok = (c2 < n_new) & (c2 <= r2 // n_heads)
        update(jnp.where(ok, sn + bias, MASKED), vn_ref[seq].astype(F32))
        out = jnp.where(own, acc_ref[...] / l_ref[...], 0.0)
        o_ref[seq] = jnp.sum(out.reshape(n_new, n_heads, width), axis=1)


def _ffn_decode_kernel(pt_ref, x_ref, g_ref, wg_ref, wu_ref, wd_ref, q_ref, kn_ref, vn_ref, negc_ref,
                       upper_ref, ck_hbm, cv_hbm, cl_hbm, h_ref, o_ref, kbuf, vbuf, lbuf, sems,
                       m_ref, l_ref, acc_ref, fcar_ref, *, n_heads, n_new, n_parts):
    i = pl.program_id(0)
    per_tile = o_ref.shape[0] * n_parts
    n_dsteps = pl.num_programs(0) * per_tile
    refs = (pt_ref, q_ref, kn_ref, vn_ref, negc_ref, upper_ref, ck_hbm, cv_hbm, cl_hbm, o_ref,
            kbuf, vbuf, lbuf, sems, m_ref, l_ref, acc_ref, fcar_ref)

    @pl.when(i == 0)
    def _():
        _decode_prologue(refs)

    x = x_ref[...]
    u = _rms(x, g_ref[...]).astype(BF16)
    chunks = _chunks(wg_ref.shape[1], FF_CHUNK)
    acc = None
    done = 0
    for c, (s, n) in enumerate(chunks):
        gate = _dot(u, wg_ref[:, s:s + n])
        up = _dot(u, wu_ref[:, s:s + n])
        a = (gate * jax.nn.sigmoid(gate) * up).astype(BF16)
        part = _dot(a, wd_ref[s:s + n, :])
        acc = part if acc is None else acc + part
        upto = ((c + 1) * per_tile) // len(chunks)
        for k in range(done, upto):
            _decode_step(i * per_tile + k, n_dsteps, k // n_parts, k % n_parts, n_parts, refs,
                         n_heads, n_new)
        done = upto
    h_ref[...] = x + 0.5 * acc


def _ffn_decode(x, g, wg, wu, wd, tm, page_table, q, kn_t, vn_t, negc, upper, cache_kt, cache_vt,
                cache_lft, n_heads):
    n, d = x.shape
    bd, n_new, width = q.shape
    n_pages = page_table.shape[1]
    page = cache_kt.shape[2]
    slots = DECODE_SLOTS
    pp = min(DECODE_PAGES, n_pages // slots)
    n_parts = n_pages // (slots * pp)
    n_tiles = n // tm
    assert n_pages == n_parts * slots * pp and (bd * n_parts) % n_tiles == 0
    per_tile = bd * n_parts // n_tiles
    assert per_tile % n_parts == 0
    seqs = per_tile // n_parts
    rows = n_new * n_heads
    pt = page_table.reshape(-1)

    seq = lambda shape: pl.BlockSpec((seqs,) + shape, lambda i, pt_ref: (i, 0, 0))
    hbm = pl.BlockSpec(memory_space=pl.ANY)
    in_specs = [pl.BlockSpec((tm, d), lambda i, pt_ref: (i, 0)),
                _const_spec(g.shape), _const_spec(wg.shape), _const_spec(wu.shape),
                _const_spec(wd.shape),
                seq((n_new, width)), seq((width, page)), seq((width, page)), seq((n_heads, LANES)),
                _const_spec(upper.shape), hbm, hbm, hbm]
    grid_spec = pltpu.PrefetchScalarGridSpec(
        num_scalar_prefetch=1,
        grid=(n_tiles,),
        in_specs=in_specs,
        out_specs=(pl.BlockSpec((tm, d), lambda i, pt_ref: (i, 0)), seq((n_new, width))),
        scratch_shapes=[pltpu.VMEM((slots, width, pp * page), F32),
                        pltpu.VMEM((slots, width, pp * page), F32),
                        pltpu.VMEM((slots, pp, n_heads, page), F32),
                        pltpu.SemaphoreType.DMA((slots, 3)),
                        pltpu.VMEM((rows, 1), F32), pltpu.VMEM((rows, 1), F32),
                        pltpu.VMEM((rows, width), F32), pltpu.VMEM((n_heads, LANES), F32)],
    )
    return pl.pallas_call(
        functools.partial(_ffn_decode_kernel, n_heads=n_heads, n_new=n_new, n_parts=n_parts),
        grid_spec=grid_spec,
        out_shape=(jax.ShapeDtypeStruct((n, d), F32), jax.ShapeDtypeStruct((bd, n_new, width), F32)),
        compiler_params=_params(("arbitrary",)),
        name="ffn_decode",
    )(pt, x, g, wg, wu, wd, q, kn_t, vn_t, negc, upper, cache_kt, cache_vt, cache_lft)


def _post_kernel(h_ref, gpool_ref, sga_ref, attn_ref, p_ref, wab_ref, wout_ref, g2_ref, wg_ref,
                 wu_ref, wd_ref, gple_ref, wpg_ref, wple_ref, gfin_ref, y_ref, *, attn_transposed):
    if attn_transposed:
        branch_attn = lax.dot_general(attn_ref[...], wab_ref[...], _TN, preferred_element_type=F32)
    else:
        branch_attn = _dot(attn_ref[...].astype(BF16), wab_ref[...])
    merged = gpool_ref[...].astype(F32) + sga_ref[...].astype(F32) * branch_attn
    h = h_ref[...] + _dot(merged.astype(BF16), wout_ref[...])
    h = _ffn_value(h, g2_ref[...], wg_ref, wu_ref, wd_ref)
    gate = jax.nn.sigmoid(_dot(_rms(h, gple_ref[...]).astype(BF16), wpg_ref[...]))
    h = h + _dot(p_ref[...].astype(BF16), wple_ref[...]) * gate
    y_ref[...] = _rms(h, gfin_ref[...])


def _post(h, gpool, sga, attn, p, w, tm, attn_transposed):
    b, t, d = h.shape
    names = ["wab", "wout", "g_ffn2", "wg2", "wu2", "wd2", "g_ple", "wpgate", "wple", "g_final"]
    consts = [w[n] for n in names]
    tok = lambda width: pl.BlockSpec((None, tm, width), lambda bi, i: (bi, i, 0))
    if attn_transposed:
        attn_spec = pl.BlockSpec((None, attn.shape[1], tm), lambda bi, i: (bi, 0, i))
    else:
        attn_spec = tok(attn.shape[2])
    return pl.pallas_call(
        functools.partial(_post_kernel, attn_transposed=attn_transposed),
        grid=(b, t // tm),
        in_specs=[tok(d), tok(d), tok(d), attn_spec, tok(p.shape[2])]
                 + [_const_spec(c.shape) for c in consts],
        out_specs=tok(d),
        out_shape=jax.ShapeDtypeStruct((b, t, d), F32),
        compiler_params=_params(("parallel", "parallel")),
        name="post",
    )(h, gpool, sga, attn, p, *consts)


def _prep_weights(g_ffn1, w_ffn1_gate, w_ffn1_up, w_ffn1_down, g_mix, w_in, b_forget, w_pool_group,
                  pool_scale, w_attn_branch, w_out, g_ffn2, w_ffn2_gate, w_ffn2_up, w_ffn2_down,
                  g_ple, w_ple_gate, w_ple, g_final, n_heads, head_dim, pool_width, tm):
    d = w_in.shape[0]
    aw = n_heads * head_dim
    o = 0
    wpool = w_in[:, o:o + pool_width]; o += pool_width
    wq = w_in[:, o:o + aw] * (head_dim ** -0.5); o += aw
    wk = w_in[:, o:o + aw]; o += aw
    wv = w_in[:, o:o + aw]; o += aw
    wf = w_in[:, o:o + n_heads]; o += n_heads
    wgp = w_in[:, o:o + d]; o += d
    wga = w_in[:, o:o + d]

    def per_head_pad(x):
        x = x.reshape(d, n_heads, head_dim)
        return jnp.pad(x, ((0, 0), (0, 0), (0, HEAD_PAD - head_dim))).reshape(d, n_heads * HEAD_PAD)

    place = np.zeros((LANES, n_heads * HEAD_PAD), np.float32)
    qones = np.zeros((1, n_heads * HEAD_PAD), np.float32)
    headsel = np.zeros((n_heads * HEAD_PAD, LANES), np.float32)
    for h in range(n_heads):
        headsel[h * HEAD_PAD:h * HEAD_PAD + head_dim, h] = 1.0
        for part in range(F_PARTS):
            place[part * n_heads + h, h * HEAD_PAD + head_dim + part] = -1.0
            qones[0, h * HEAD_PAD + head_dim + part] = 1.0
    row = lambda x: x.reshape(1, -1).astype(F32)
    bf = lambda x: x.astype(BF16)
    return {
        "g_ffn1": row(g_ffn1), "wg1": bf(w_ffn1_gate), "wu1": bf(w_ffn1_up), "wd1": bf(w_ffn1_down),
        "g_mix": row(g_mix), "wpool": bf(wpool), "wq": bf(wq), "wk": bf(wk), "wv": bf(wv),
        "wqa": bf(per_head_pad(wq * LOG2E)), "wka": bf(per_head_pad(wk)),
        "wkt": bf(wk.T), "wvt": bf(wv.T),
        "wf": bf(jnp.pad(wf, ((0, 0), (0, LANES - n_heads)))),
        "bf": jnp.pad(row(b_forget), ((0, 0), (0, LANES - n_heads))),
        "wgp": bf(wgp), "wga": bf(wga), "wpg": bf(w_pool_group), "pscale": row(pool_scale),
        "place": jnp.asarray(place, BF16), "qones": jnp.asarray(qones, F32),
        "headsel": jnp.asarray(headsel, BF16),
        "tri": jnp.asarray(np.tril(np.ones((tm, tm), np.float32)), BF16),
        "wab": bf(w_attn_branch), "wout": bf(w_out),
        "g_ffn2": row(g_ffn2), "wg2": bf(w_ffn2_gate), "wu2": bf(w_ffn2_up), "wd2": bf(w_ffn2_down),
        "g_ple": row(g_ple), "wpgate": bf(w_ple_gate), "wple": bf(w_ple), "g_final": row(g_final),
    }


def kernel(x_prompt, x_sample, cache_k, cache_v, cache_logf, state_pool, page_table, p_prompt, p_sample, g_ffn1, w_ffn1_gate, w_ffn1_up, w_ffn1_down, g_mix, w_in, b_forget, w_pool_group, pool_scale, w_attn_branch, w_out, g_ffn2, w_ffn2_gate, w_ffn2_up, w_ffn2_down, g_ple, w_ple_gate, w_ple, g_final):
    depth = cache_k.shape[0]
    assert depth == 1, "one trunk layer"
    b, t, d = x_prompt.shape
    bd, n_new, _ = x_sample.shape
    _, n_pool_pages, page, n_heads, head_dim = cache_k.shape
    aw = n_heads * head_dim
    n_pre, pool_width = state_pool.shape[2], state_pool.shape[3]
    assert head_dim + F_PARTS <= HEAD_PAD and head_dim < V_ROWS and page == LANES
    assert pool_width == len(POOL_WINDOWS) * LANES and n_pre == POOL_WINDOWS[-1] - 1
    assert n_heads % ATTN_HEADS == 0
    tm = min(TOKEN_TILE, t)
    assert t // tm <= LANES
    ns = bd * n_new

    w = _prep_weights(g_ffn1[0], w_ffn1_gate[0], w_ffn1_up[0], w_ffn1_down[0], g_mix[0], w_in[0],
                      b_forget[0], w_pool_group[0], pool_scale[0], w_attn_branch[0], w_out[0],
                      g_ffn2[0], w_ffn2_gate[0], w_ffn2_up[0], w_ffn2_down[0], g_ple[0],
                      w_ple_gate[0], w_ple[0], g_final, n_heads, head_dim, pool_width, tm)

    step_major = lambda x: jnp.swapaxes(x, 0, 1).reshape(ns, x.shape[-1])
    seq_major = lambda x: jnp.swapaxes(x.reshape(n_new, bd, x.shape[-1]), 0, 1)
    hs1 = _ffn(step_major(x_sample), w["g_ffn1"], w["wg1"], w["wu1"], w["wd1"], ns)
    state = jnp.swapaxes(state_pool[0], 0, 1).reshape(n_pre * bd, pool_width)
    z_s, q_s, k_s, v_s, lf_s, fc_s, gpool_s, sga_s = _proj_sample(hs1, state, w, n_heads, n_new, bd)
    k_s, v_s, lf_s = seq_major(k_s), seq_major(v_s), seq_major(lf_s)
    new_page = lambda x: jnp.pad(jnp.swapaxes(x, 1, 2).astype(BF16), ((0, 0), (0, 0), (0, page - n_new)))
    negc = -jnp.swapaxes(seq_major(fc_s), 1, 2)
    negc = jnp.pad(negc, ((0, 0), (0, 0), (0, LANES - n_new)))
    upper = jnp.asarray(np.triu(np.ones((page, page), np.float32)), BF16)
    pages_t = lambda c: jnp.transpose(c[0], (0, 2, 3, 1)).reshape(n_pool_pages, aw, page)

    h1, attn_s = _ffn_decode(x_prompt.reshape(b * t, d), w["g_ffn1"], w["wg1"], w["wu1"], w["wd1"], tm,
                             page_table, seq_major(q_s), new_page(k_s), new_page(v_s), negc, upper,
                             pages_t(cache_k), pages_t(cache_v), jnp.swapaxes(cache_logf[0], 1, 2),
                             n_heads)
    h1 = h1.reshape(b, t, d)
    qa, ka, vt, kt_p, vt_p, lf_p, gpool, sga, zlast, qn, kn, fe = _proj_prompt(h1, w, tm, n_heads)
    stat = lambda x: jnp.swapaxes(x[:, :, 0, :n_heads], 1, 2).reshape(
        b, n_heads // ATTN_HEADS, ATTN_HEADS, t // tm)
    attn_t = _attention(qa, ka, vt, stat(qn), stat(kn), stat(fe), head_dim, tm)
    y_prompt = _post(h1, gpool, sga, attn_t, p_prompt[0], w, tm, True)
    heads_last = lambda x: jnp.transpose(x.reshape(b, n_heads, head_dim, t), (0, 3, 1, 2))[None]

    y_s = _post(hs1[None], gpool_s[None], sga_s[None], step_major(attn_s)[None],
                step_major(p_sample[0])[None], w, ns, False)
    y_sample = seq_major(y_s[0])
    pool_sample = jnp.concatenate([state_pool[0], seq_major(z_s)], axis=1)[:, -n_pre:]

    return (y_prompt, y_sample, heads_last(kt_p), heads_last(vt_p),
            lf_p[None], zlast[None, :, POOL_HALO - n_pre:],
            k_s.reshape(1, bd, n_new, n_heads, head_dim), v_s.reshape(1, bd, n_new, n_heads, head_dim),
            lf_s[None], pool_sample[None])
```

```python
import functools

import numpy as np
import jax
import jax.numpy as jnp
from jax import lax
from jax.experimental import pallas as pl
from jax.experimental.pallas import tpu as pltpu

F32 = jnp.float32
BF16 = jnp.bfloat16

RMS_EPS = 1e-6
MASKED = -1e30
POOL_WINDOWS = (2, 4, 8, 16)
POOL_HALO = 16

LANES = 128
HEAD_PAD = 128
V_ROWS = 80
F_PARTS = 3
FF_CHUNK = 768
TOKEN_TILE = 512
DECODE_PAGES = 8
DECODE_SLOTS = 4
DECODE_MERGE = 2
ATTN_HEADS = 2
ATTN_GROUPS = (1, 2)
VMEM_LIMIT = 56 * 1024 * 1024
LOG2E = 1.4426950408889634
NORM_SLACK = 1.01
SKIP_MARGIN = 140.0

_NT = (((1,), (1,)), ((), ()))
_TN = (((0,), (0,)), ((), ()))


def _rms(x, g):
    r = lax.rsqrt(jnp.mean(x * x, axis=-1, keepdims=True) + RMS_EPS)
    return x * r * g


def _dot(a, b):
    return jnp.dot(a, b, preferred_element_type=F32)


def _chunks(n, c):
    return [(s, min(c, n - s)) for s in range(0, n, c)]


def _split3(x):
    hi = x.astype(BF16).astype(F32)
    r = x - hi
    mid = r.astype(BF16).astype(F32)
    lo = (r - mid).astype(BF16).astype(F32)
    return hi, mid, lo


def _pack3(x, n):
    hi, mid, lo = _split3(x)
    return (hi + pltpu.roll(mid, n, 1) + pltpu.roll(lo, 2 * n, 1)).astype(BF16)


def _unpack3(c, n):
    return c + pltpu.roll(c, LANES - n, 1) + pltpu.roll(c, LANES - 2 * n, 1)


def _log_sigmoid(x):
    return jnp.minimum(x, 0.0) - jnp.log1p(jnp.exp(-jnp.abs(x)))


def _const_spec(shape):
    nd = len(shape)
    return pl.BlockSpec(shape, lambda *_: (0,) * nd, pipeline_mode=pl.Buffered(1))


def _params(sem):
    return pltpu.CompilerParams(dimension_semantics=sem, vmem_limit_bytes=VMEM_LIMIT)


def _ffn_value(x, g, wg_ref, wu_ref, wd_ref):
    u = _rms(x, g).astype(BF16)
    acc = None
    for s, n in _chunks(wg_ref.shape[1], FF_CHUNK):
        gate = _dot(u, wg_ref[:, s:s + n])
        up = _dot(u, wu_ref[:, s:s + n])
        a = (gate * jax.nn.sigmoid(gate) * up).astype(BF16)
        d = _dot(a, wd_ref[s:s + n, :])
        acc = d if acc is None else acc + d
    return x + 0.5 * acc


def _ffn_kernel(x_ref, g_ref, wg_ref, wu_ref, wd_ref, o_ref):
    o_ref[...] = _ffn_value(x_ref[...], g_ref[...], wg_ref, wu_ref, wd_ref)


def _ffn(x, g, wg, wu, wd, tm):
    n, d = x.shape
    return pl.pallas_call(
        _ffn_kernel,
        grid=(n // tm,),
        in_specs=[pl.BlockSpec((tm, d), lambda i: (i, 0)),
                  _const_spec(g.shape), _const_spec(wg.shape), _const_spec(wu.shape),
                  _const_spec(wd.shape)],
        out_specs=pl.BlockSpec((tm, d), lambda i: (i, 0)),
        out_shape=jax.ShapeDtypeStruct((n, d), F32),
        compiler_params=_params(("parallel",)),
        name="ffn",
    )(x, g, wg, wu, wd)


def _pool_branch(pooled_groups, wpg_ref):
    outs = [_dot(p.astype(BF16), wpg_ref[g]) for g, p in enumerate(pooled_groups)]
    return jnp.concatenate(outs, axis=1)


def _head_lane_mask(shape, n_heads):
    lane = lax.broadcasted_iota(jnp.int32, shape, 1)
    return lane < n_heads


def _proj_prompt_kernel(h_ref, g_ref, wpool_ref, wq_ref, wk_ref, wkt_ref, wvt_ref,
                        wf_ref, bf_ref, wgp_ref, wga_ref, wpg_ref, pscale_ref, place_ref,
                        tri_ref, qones_ref, headsel_ref,
                        qa_ref, ka_ref, vt_ref, kt_ref, vto_ref, lf_ref, gpool_ref, sga_ref,
                        zlast_ref, qn_ref, kn_ref, fe_ref, zext_ref, fcarry_ref, *, n_heads):
    i = pl.program_id(1)
    tm = h_ref.shape[0]
    gw = LANES

    @pl.when(i == 0)
    def _():
        zext_ref[0:POOL_HALO, :] = jnp.zeros((POOL_HALO, zext_ref.shape[1]), F32)
        fcarry_ref[...] = jnp.zeros(fcarry_ref.shape, F32)

    u = _rms(h_ref[...], g_ref[...]).astype(BF16)

    z = _dot(u, wpool_ref[...])
    zext_ref[POOL_HALO:POOL_HALO + tm, :] = z
    row = lax.broadcasted_iota(jnp.int32, (tm, gw), 0) + i * tm
    pooled = []
    for g, w in enumerate(POOL_WINDOWS):
        sl = slice(g * gw, (g + 1) * gw)
        zg = z[:, sl]
        acc = zg
        for j in range(1, w):
            acc = acc + zext_ref[POOL_HALO - j:POOL_HALO - j + tm, sl]
        cnt = jnp.minimum(row + 1, w).astype(F32)
        pooled.append(acc / cnt - zg)
    zlast_ref[...] = zext_ref[tm:tm + POOL_HALO, :]
    zext_ref[0:POOL_HALO, :] = zext_ref[tm:tm + POOL_HALO, :]
    branch_pool = _pool_branch(pooled, wpg_ref) * pscale_ref[...]
    gpool_ref[...] = (jax.nn.sigmoid(_dot(u, wgp_ref[...])) * branch_pool).astype(BF16)
    sga_ref[...] = jax.nn.sigmoid(_dot(u, wga_ref[...])).astype(BF16)

    lane_ok = _head_lane_mask((tm, LANES), n_heads)
    logf = jnp.where(lane_ok, _log_sigmoid(_dot(u, wf_ref[...]) + bf_ref[...]), 0.0)
    lf_ref[...] = logf[:, 0:n_heads]
    csum = _unpack3(_dot(tri_ref[...], _pack3(logf, n_heads)), n_heads)
    fcum = jnp.where(lane_ok, csum + fcarry_ref[...], 0.0)
    fcarry_ref[...] = fcum[tm - 1:tm, :]
    fcum2 = fcum * LOG2E
    fe_ref[...] = fcum2[tm - 1:tm, :]

    q = _dot(u, wq_ref[...]).astype(BF16).astype(F32)
    k = _dot(u, wk_ref[...]).astype(BF16).astype(F32)
    for x, n_ref in ((q, qn_ref), (k, kn_ref)):
        sq = (x * x * NORM_SLACK).astype(BF16)
        n_ref[...] = jnp.max(_dot(sq, headsel_ref[...]), axis=0, keepdims=True)
    kt = lax.dot_general(wkt_ref[...], u, _NT, preferred_element_type=F32)
    vt = lax.dot_general(wvt_ref[...], u, _NT, preferred_element_type=F32)
    kt_ref[...] = kt
    vto_ref[...] = vt

    head_dim = kt.shape[0] // n_heads
    per_slab = LANES // head_dim
    fcols = _dot(_pack3(fcum2, n_heads), place_ref[...])
    low = lax.broadcasted_iota(jnp.int32, (tm, LANES), 1) < head_dim
    rid = lax.broadcasted_iota(jnp.int32, (V_ROWS - head_dim, tm), 0)
    ones_rows = jnp.where(rid == 0, 1.0, 0.0).astype(BF16)
    for h in range(n_heads):
        col = (h // per_slab) * LANES
        shift = (h % per_slab) * head_dim
        qs, ks = q[:, col:col + LANES], k[:, col:col + LANES]
        if shift:
            qs, ks = pltpu.roll(qs, LANES - shift, 1), pltpu.roll(ks, LANES - shift, 1)
        qa_ref[h] = jnp.where(low, qs, qones_ref[...]).astype(BF16)
        ka_ref[h] = jnp.where(low, ks, fcols[:, h * HEAD_PAD:(h + 1) * HEAD_PAD]).astype(BF16)
        vt_ref[h, 0:head_dim, :] = vt[h * head_dim:(h + 1) * head_dim, :].astype(BF16)
        vt_ref[h, head_dim:V_ROWS, :] = ones_rows


def _proj_prompt(h, w, tm, n_heads):
    b, t, d = h.shape
    aw = w["wkt"].shape[0]
    pw = w["wpool"].shape[1]
    nt = t // tm
    names = ["g_mix", "wpool", "wq2", "wk", "wkt", "wvt", "wf", "bf", "wgp", "wga", "wpg",
             "pscale", "place", "tri", "qones", "headsel"]
    consts = [w[n] for n in names]
    tok = lambda width: pl.BlockSpec((None, tm, width), lambda bi, i: (bi, i, 0))
    tok_t = lambda rows: pl.BlockSpec((None, rows, tm), lambda bi, i: (bi, 0, i))
    head = pl.BlockSpec((None, n_heads, tm, HEAD_PAD), lambda bi, i: (bi, 0, i, 0))
    stat = pl.BlockSpec((None, None, 1, LANES), lambda bi, i: (bi, i, 0, 0))
    stat_shape = jax.ShapeDtypeStruct((b, nt, 1, LANES), F32)
    out_shape = (
        jax.ShapeDtypeStruct((b, n_heads, t, HEAD_PAD), BF16),
        jax.ShapeDtypeStruct((b, n_heads, t, HEAD_PAD), BF16),
        jax.ShapeDtypeStruct((b, n_heads, V_ROWS, t), BF16),
        jax.ShapeDtypeStruct((b, aw, t), F32),
        jax.ShapeDtypeStruct((b, aw, t), F32),
        jax.ShapeDtypeStruct((b, t, n_heads), F32),
        jax.ShapeDtypeStruct((b, t, d), BF16),
        jax.ShapeDtypeStruct((b, t, d), BF16),
        jax.ShapeDtypeStruct((b, POOL_HALO, pw), F32),
        stat_shape, stat_shape, stat_shape,
    )
    out_specs = (
        head, head,
        pl.BlockSpec((None, n_heads, V_ROWS, tm), lambda bi, i: (bi, 0, 0, i)),
        tok_t(aw), tok_t(aw), tok(n_heads), tok(d), tok(d),
        pl.BlockSpec((None, POOL_HALO, pw), lambda bi, i: (bi, 0, 0)),
        stat, stat, stat,
    )
    return pl.pallas_call(
        functools.partial(_proj_prompt_kernel, n_heads=n_heads),
        grid=(b, nt),
        in_specs=[tok(d)] + [_const_spec(c.shape) for c in consts],
        out_specs=out_specs,
        out_shape=out_shape,
        scratch_shapes=[pltpu.VMEM((tm + POOL_HALO, pw), F32), pltpu.VMEM((1, LANES), F32)],
        compiler_params=_params(("arbitrary", "arbitrary")),
        name="proj_prompt",
    )(h, *consts)


def _proj_sample_kernel(h_ref, state_ref, g_ref, wpool_ref, wq_ref, wk_ref, wv_ref, wf_ref,
                        bf_ref, wgp_ref, wga_ref, wpg_ref, pscale_ref,
                        z_ref, q_ref, k_ref, v_ref, lf_ref, fc_ref, gpool_ref, sga_ref,
                        *, n_heads, n_new, n_seq):
    gw = LANES
    n_pre = state_ref.shape[0] // n_seq
    u = _rms(h_ref[...], g_ref[...]).astype(BF16)

    z = _dot(u, wpool_ref[...])
    z_ref[...] = z

    def ext_rows(r, sl):
        if r < n_pre:
            return state_ref[r * n_seq:(r + 1) * n_seq, sl]
        return z[(r - n_pre) * n_seq:(r - n_pre + 1) * n_seq, sl]

    pooled = []
    for g, w in enumerate(POOL_WINDOWS):
        sl = slice(g * gw, (g + 1) * gw)
        steps = []
        for s in range(n_new):
            r = n_pre + s
            acc = ext_rows(r, sl)
            for j in range(1, w):
                if r - j >= 0:
                    acc = acc + ext_rows(r - j, sl)
            steps.append(acc / float(min(r + 1, w)) - ext_rows(r, sl))
        pooled.append(jnp.concatenate(steps, axis=0))
    branch_pool = _pool_branch(pooled, wpg_ref) * pscale_ref[...]
    gpool_ref[...] = (jax.nn.sigmoid(_dot(u, wgp_ref[...])) * branch_pool).astype(BF16)
    sga_ref[...] = jax.nn.sigmoid(_dot(u, wga_ref[...])).astype(BF16)

    tm = h_ref.shape[0]
    lane_ok = _head_lane_mask((tm, LANES), n_heads)
    logf = jnp.where(lane_ok, _log_sigmoid(_dot(u, wf_ref[...]) + bf_ref[...]), 0.0)
    lf_ref[...] = logf[:, 0:n_heads]
    run = None
    sums = []
    for s in range(n_new):
        blk = logf[s * n_seq:(s + 1) * n_seq, :]
        run = blk if run is None else run + blk
        sums.append(run)
    fc_ref[...] = jnp.concatenate(sums, axis=0)[:, 0:n_heads]

    q_ref[...] = _dot(u, wq_ref[...]).astype(BF16)
    k_ref[...] = _dot(u, wk_ref[...])
    v_ref[...] = _dot(u, wv_ref[...])


def _proj_sample(h, state, w, n_heads, n_new, n_seq):
    n, d = h.shape
    aw = w["wk"].shape[1]
    pw = w["wpool"].shape[1]
    names = ["g_mix", "wpool", "wq", "wk", "wv", "wf", "bf", "wgp", "wga", "wpg", "pscale"]
    consts = [w[n_] for n_ in names]
    full = lambda shape: pl.BlockSpec(shape, lambda i: (0,) * len(shape))
    out_shape = (
        jax.ShapeDtypeStruct((n, pw), F32),
        jax.ShapeDtypeStruct((n, aw), BF16),
        jax.ShapeDtypeStruct((n, aw), F32),
        jax.ShapeDtypeStruct((n, aw), F32),
        jax.ShapeDtypeStruct((n, n_heads), F32),
        jax.ShapeDtypeStruct((n, n_heads), F32),
        jax.ShapeDtypeStruct((n, d), BF16),
        jax.ShapeDtypeStruct((n, d), BF16),
    )
    return pl.pallas_call(
        functools.partial(_proj_sample_kernel, n_heads=n_heads, n_new=n_new, n_seq=n_seq),
        grid=(1,),
        in_specs=[full(h.shape), full(state.shape)] + [_const_spec(c.shape) for c in consts],
        out_specs=tuple(full(s.shape) for s in out_shape),
        out_shape=out_shape,
        compiler_params=_params(("arbitrary",)),
        name="proj_sample",
    )(h, state, *consts)


def _attn_kernel(q_ref, k_ref, vt_ref, qn_ref, kn_ref, fe_ref, o_ref, *, head_dim):
    i = pl.program_id(2)
    nh, tq, _ = q_ref.shape
    tk = tq
    nt = fe_ref.shape[1]
    qs = [q_ref[h] for h in range(nh)]

    def scores(h, j):
        start = pl.multiple_of(j * tk, tk)
        return lax.dot_general(k_ref[h, pl.ds(start, tk), :], qs[h], _NT, preferred_element_type=F32)

    def absorb(h, j, s, m, acc):
        start = pl.multiple_of(j * tk, tk)
        m_new = jnp.maximum(m, jnp.max(s, axis=0, keepdims=True))
        alpha = jnp.exp2(m - m_new)
        p = jnp.exp2(s - m_new).astype(BF16)
        acc = alpha * acc + _dot(vt_ref[h, :, pl.ds(start, tk)], p)
        return m_new, acc

    kpos = lax.broadcasted_iota(jnp.int32, (tk, tq), 0)
    qpos = lax.broadcasted_iota(jnp.int32, (tk, tq), 1)
    diag = [jnp.where(kpos <= qpos, scores(h, i), MASKED) for h in range(nh)]
    state = []
    for h in range(nh):
        state += list(absorb(h, i, diag[h], jnp.full((1, tq), MASKED, F32),
                             jnp.zeros((V_ROWS, tq), F32)))

    lane = lax.broadcasted_iota(jnp.int32, (1, nt), 1)
    n_back = None
    for h in range(nh):
        m_min = jnp.min(state[2 * h], axis=1, keepdims=True)
        q2 = jnp.max(jnp.where(lane == i, qn_ref[h:h + 1, :], 0.0), axis=1, keepdims=True)
        k2 = jnp.max(kn_ref[h:h + 1, :], axis=1, keepdims=True)
        bound = jnp.sqrt(q2 * k2) - fe_ref[h:h + 1, :]
        need = (lane < i) & (bound > m_min - SKIP_MARGIN)
        n_h = jnp.sum(need.astype(jnp.int32))
        n_back = n_h if n_back is None else jnp.maximum(n_back, n_h)

    def make_body(group):
        def body(jj, c):
            c = list(c)
            first = c[-1]
            tiles = [first - jj * group - g for g in range(group)]
            ss = [[scores(h, j) for h in range(nh)] for j in tiles]
            for g, j in enumerate(tiles):
                for h in range(nh):
                    c[2 * h], c[2 * h + 1] = absorb(h, j, ss[g][h], c[2 * h], c[2 * h + 1])
            return tuple(c)
        return body

    left, nearest, state = n_back, i - 1, tuple(state)
    for group in ATTN_GROUPS:
        bigger = [x for x in ATTN_GROUPS if x > group]
        trips = (left % min(bigger)) // group if bigger else left // group
        state = lax.fori_loop(0, trips, make_body(group), state + (nearest,))[:-1]
        left, nearest = left - trips * group, nearest - trips * group
    for h in range(nh):
        acc = state[2 * h + 1]
        o_ref[h * head_dim:(h + 1) * head_dim, :] = (
            acc[0:head_dim, :] / acc[head_dim:head_dim + 1, :]).astype(BF16)


def _attention(qa, ka, vt, qn, kn, fe, head_dim, tq):
    b, nh, t, _ = qa.shape
    g = ATTN_HEADS
    nt = t // tq
    stat = pl.BlockSpec((None, None, g, nt), lambda bi, h, i: (bi, h, 0, 0))
    return pl.pallas_call(
        functools.partial(_attn_kernel, head_dim=head_dim),
        grid=(b, nh // g, nt),
        in_specs=[pl.BlockSpec((None, g, tq, HEAD_PAD), lambda bi, h, i: (bi, h, i, 0)),
                  pl.BlockSpec((None, g, t, HEAD_PAD), lambda bi, h, i: (bi, h, 0, 0)),
                  pl.BlockSpec((None, g, V_ROWS, t), lambda bi, h, i: (bi, h, 0, 0)),
                  stat, stat, stat],
        out_specs=pl.BlockSpec((None, g * head_dim, tq), lambda bi, h, i: (bi, h, i)),
        out_shape=jax.ShapeDtypeStruct((b, nh * head_dim, t), BF16),
        compiler_params=_params(("parallel", "parallel", "arbitrary")),
        name="attn_prompt",
    )(qa, ka, vt, qn, kn, fe)


def _page_copies(ck_hbm, cv_hbm, cl_hbm, kbuf, vbuf, lbuf, sems, slot, page_ids):
    page = cl_hbm.shape[2]
    out = []
    for i, pid in enumerate(page_ids):
        lanes = slice(i * page, (i + 1) * page)
        out += [pltpu.make_async_copy(ck_hbm.at[pid], kbuf.at[slot, :, lanes], sems.at[slot, 0]),
                pltpu.make_async_copy(cv_hbm.at[pid], vbuf.at[slot, :, lanes], sems.at[slot, 1]),
                pltpu.make_async_copy(cl_hbm.at[pid], lbuf.at[slot, i], sems.at[slot, 2])]
    return out


def _decode_prologue(refs):
    pt_ref, ck_hbm, cv_hbm, cl_hbm, kbuf, vbuf, lbuf, sems = refs[0], *refs[6:9], *refs[10:14]
    slots = kbuf.shape[0]
    pp = kbuf.shape[2] // cl_hbm.shape[2]
    for s in range(slots):
        for c in _page_copies(ck_hbm, cv_hbm, cl_hbm, kbuf, vbuf, lbuf, sems, s,
                              [pt_ref[s * pp + i] for i in range(pp)]):
            c.start()


def _decode_step(d, n_dsteps, seq, part, n_parts, refs, n_heads, n_new):
    (pt_ref, q_ref, kn_ref, vn_ref, negc_ref, upper_ref, ck_hbm, cv_hbm, cl_hbm, o_ref,
     kbuf, vbuf, lbuf, sems, m_ref, l_ref, acc_ref, fcar_ref) = refs
    slots, width, group_keys = kbuf.shape
    page = cl_hbm.shape[2]
    pp = group_keys // page
    head_dim = width // n_heads
    rows = n_new * n_heads

    def copies(slot, page_ids):
        return _page_copies(ck_hbm, cv_hbm, cl_hbm, kbuf, vbuf, lbuf, sems, slot, page_ids)

    def start(slot, first):
        for c in copies(slot, [pt_ref[first + i] for i in range(pp)]):
            c.start()

    def wait(slot):
        for c in copies(slot, [0] * pp):
            c.wait()

    if part == 0:
        m_ref[...] = jnp.full(m_ref.shape, MASKED, F32)
        l_ref[...] = jnp.zeros(l_ref.shape, F32)
        acc_ref[...] = jnp.zeros(acc_ref.shape, F32)
        fcar_ref[...] = jnp.zeros(fcar_ref.shape, F32)

    q = q_ref[seq].astype(F32)
    qrows = jnp.concatenate([jnp.broadcast_to(q[s:s + 1, :], (n_heads, width)) for s in range(n_new)],
                            axis=0)
    rid = lax.broadcasted_iota(jnp.int32, (rows, width), 0)
    cid = lax.broadcasted_iota(jnp.int32, (rows, width), 1)
    own = (cid // head_dim) == (rid % n_heads)
    qbd = jnp.where(own, qrows, 0.0)

    def update(s, vals_t):
        m_new = jnp.maximum(m_ref[...], jnp.max(s, axis=1, keepdims=True))
        alpha = jnp.exp(m_ref[...] - m_new)
        p = jnp.exp(s - m_new)
        l_ref[...] = alpha * l_ref[...] + jnp.sum(p, axis=1, keepdims=True)
        pv, at = None, 0
        for v in vals_t:
            part = lax.dot_general(p[:, at:at + v.shape[1]], v, _NT, preferred_element_type=F32)
            pv = part if pv is None else pv + part
            at += v.shape[1]
        acc_ref[...] = alpha * acc_ref[...] + pv
        m_ref[...] = m_new

    def consume(group):
        x = jnp.concatenate([lbuf[slot, i] for slot in group for i in range(pp)], axis=0)
        parts = jnp.concatenate(_split3(x), axis=0).astype(BF16)
        c = _dot(parts, upper_ref[...])
        nr = len(group) * pp * n_heads
        fin = c[0:nr] + c[nr:2 * nr] + c[2 * nr:3 * nr]
        carry = fcar_ref[...]
        biases = []
        for i in range(len(group) * pp):
            fi = fin[i * n_heads:(i + 1) * n_heads, :]
            biases.append(jnp.concatenate([-(fi + carry)] * n_new, axis=0))
            carry = carry + jnp.broadcast_to(fi[:, page - 1:page], carry.shape)
        fcar_ref[...] = carry
        scores = jnp.concatenate([_dot(qbd, kbuf[slot]) for slot in group], axis=1)
        update(scores + jnp.concatenate(biases, axis=1), [vbuf[slot] for slot in group])

    first = d * (slots * pp)
    for s0 in range(0, slots, DECODE_MERGE):
        group = list(range(s0, s0 + DECODE_MERGE))
        for s in group:
            wait(s)
        consume(group)

        @pl.when(d + 1 < n_dsteps)
        def _():
            for s in group:
                start(s, first + (slots + s) * pp)

    if part == n_parts - 1:
        carry = fcar_ref[...]
        sn = _dot(qbd, kn_ref[seq].astype(F32))
        bias = jnp.concatenate([negc_ref[seq] - carry] * n_new, axis=0)
        r2 = lax.broadcasted_iota(jnp.int32, sn.shape, 0)
        c2 = lax.broadcasted_iota(jnp.int32, sn.shape, 1)
        ok = (c2 < n_new) & (c2 <= r2 // n_heads)
        update(jnp.where(ok, sn + bias, MASKED), [vn_ref[seq].astype(F32)])
        out = jnp.where(own, acc_ref[...] / l_ref[...], 0.0)
        o_ref[seq] = jnp.sum(out.reshape(n_new, n_heads, width), axis=1)


def _ffn_decode_kernel(pt_ref, x_ref, g_ref, wg_ref, wu_ref, wd_ref, q_ref, kn_ref, vn_ref, negc_ref,
                       upper_ref, ck_hbm, cv_hbm, cl_hbm, h_ref, o_ref, kbuf, vbuf, lbuf, sems,
                       m_ref, l_ref, acc_ref, fcar_ref, *, n_heads, n_new, n_parts):
    i = pl.program_id(0)
    per_tile = o_ref.shape[0] * n_parts
    n_dsteps = pl.num_programs(0) * per_tile
    refs = (pt_ref, q_ref, kn_ref, vn_ref, negc_ref, upper_ref, ck_hbm, cv_hbm, cl_hbm, o_ref,
            kbuf, vbuf, lbuf, sems, m_ref, l_ref, acc_ref, fcar_ref)

    @pl.when(i == 0)
    def _():
        _decode_prologue(refs)

    x = x_ref[...]
    u = _rms(x, g_ref[...]).astype(BF16)
    chunks = _chunks(wg_ref.shape[1], FF_CHUNK)
    acc = None
    done = 0
    for c, (s, n) in enumerate(chunks):
        gate = _dot(u, wg_ref[:, s:s + n])
        up = _dot(u, wu_ref[:, s:s + n])
        a = (gate * jax.nn.sigmoid(gate) * up).astype(BF16)
        part = _dot(a, wd_ref[s:s + n, :])
        acc = part if acc is None else acc + part
        upto = ((c + 1) * per_tile) // len(chunks)
        for k in range(done, upto):
            _decode_step(i * per_tile + k, n_dsteps, k // n_parts, k % n_parts, n_parts, refs,
                         n_heads, n_new)
        done = upto
    h_ref[...] = x + 0.5 * acc


def _ffn_decode(x, g, wg, wu, wd, tm, page_table, q, kn_t, vn_t, negc, upper, cache_kt, cache_vt,
                cache_lft, n_heads):
    n, d = x.shape
    bd, n_new, width = q.shape
    n_pages = page_table.shape[1]
    page = cache_kt.shape[2]
    slots = DECODE_SLOTS
    pp = min(DECODE_PAGES, n_pages // slots)
    n_parts = n_pages // (slots * pp)
    n_tiles = n // tm
    assert n_pages == n_parts * slots * pp and (bd * n_parts) % n_tiles == 0
    per_tile = bd * n_parts // n_tiles
    assert per_tile % n_parts == 0
    seqs = per_tile // n_parts
    rows = n_new * n_heads
    pt = page_table.reshape(-1)

    seq = lambda shape: pl.BlockSpec((seqs,) + shape, lambda i, pt_ref: (i, 0, 0))
    hbm = pl.BlockSpec(memory_space=pl.ANY)
    in_specs = [pl.BlockSpec((tm, d), lambda i, pt_ref: (i, 0)),
                _const_spec(g.shape), _const_spec(wg.shape), _const_spec(wu.shape),
                _const_spec(wd.shape),
                seq((n_new, width)), seq((width, page)), seq((width, page)), seq((n_heads, LANES)),
                _const_spec(upper.shape), hbm, hbm, hbm]
    grid_spec = pltpu.PrefetchScalarGridSpec(
        num_scalar_prefetch=1,
        grid=(n_tiles,),
        in_specs=in_specs,
        out_specs=(pl.BlockSpec((tm, d), lambda i, pt_ref: (i, 0)), seq((n_new, width))),
        scratch_shapes=[pltpu.VMEM((slots, width, pp * page), F32),
                        pltpu.VMEM((slots, width, pp * page), F32),
                        pltpu.VMEM((slots, pp, n_heads, page), F32),
                        pltpu.SemaphoreType.DMA((slots, 3)),
                        pltpu.VMEM((rows, 1), F32), pltpu.VMEM((rows, 1), F32),
                        pltpu.VMEM((rows, width), F32), pltpu.VMEM((n_heads, LANES), F32)],
    )
    return pl.pallas_call(
        functools.partial(_ffn_decode_kernel, n_heads=n_heads, n_new=n_new, n_parts=n_parts),
        grid_spec=grid_spec,
        out_shape=(jax.ShapeDtypeStruct((n, d), F32), jax.ShapeDtypeStruct((bd, n_new, width), F32)),
        compiler_params=_params(("arbitrary",)),
        name="ffn_decode",
    )(pt, x, g, wg, wu, wd, q, kn_t, vn_t, negc, upper, cache_kt, cache_vt, cache_lft)


def _post_kernel(h_ref, gpool_ref, sga_ref, attn_ref, p_ref, wab_ref, wout_ref, g2_ref, wg_ref,
                 wu_ref, wd_ref, gple_ref, wpg_ref, wple_ref, gfin_ref, y_ref, *, attn_transposed):
    if attn_transposed:
        branch_attn = lax.dot_general(attn_ref[...], wab_ref[...], _TN, preferred_element_type=F32)
    else:
        branch_attn = _dot(attn_ref[...].astype(BF16), wab_ref[...])
    merged = gpool_ref[...].astype(F32) + sga_ref[...].astype(F32) * branch_attn
    h = h_ref[...] + _dot(merged.astype(BF16), wout_ref[...])
    h = _ffn_value(h, g2_ref[...], wg_ref, wu_ref, wd_ref)
    gate = jax.nn.sigmoid(_dot(_rms(h, gple_ref[...]).astype(BF16), wpg_ref[...]))
    h = h + _dot(p_ref[...].astype(BF16), wple_ref[...]) * gate
    y_ref[...] = _rms(h, gfin_ref[...])


def _post(h, gpool, sga, attn, p, w, tm, attn_transposed):
    b, t, d = h.shape
    names = ["wab", "wout", "g_ffn2", "wg2", "wu2", "wd2", "g_ple", "wpgate", "wple", "g_final"]
    consts = [w[n] for n in names]
    tok = lambda width: pl.BlockSpec((None, tm, width), lambda bi, i: (bi, i, 0))
    if attn_transposed:
        attn_spec = pl.BlockSpec((None, attn.shape[1], tm), lambda bi, i: (bi, 0, i))
    else:
        attn_spec = tok(attn.shape[2])
    return pl.pallas_call(
        functools.partial(_post_kernel, attn_transposed=attn_transposed),
        grid=(b, t // tm),
        in_specs=[tok(d), tok(d), tok(d), attn_spec, tok(p.shape[2])]
                 + [_const_spec(c.shape) for c in consts],
        out_specs=tok(d),
        out_shape=jax.ShapeDtypeStruct((b, t, d), F32),
        compiler_params=_params(("parallel", "parallel")),
        name="post",
    )(h, gpool, sga, attn, p, *consts)


def _prep_weights(g_ffn1, w_ffn1_gate, w_ffn1_up, w_ffn1_down, g_mix, w_in, b_forget, w_pool_group,
                  pool_scale, w_attn_branch, w_out, g_ffn2, w_ffn2_gate, w_ffn2_up, w_ffn2_down,
                  g_ple, w_ple_gate, w_ple, g_final, n_heads, head_dim, pool_width, tm):
    d = w_in.shape[0]
    aw = n_heads * head_dim
    o = 0
    wpool = w_in[:, o:o + pool_width]; o += pool_width
    wq = w_in[:, o:o + aw] * (head_dim ** -0.5); o += aw
    wk = w_in[:, o:o + aw]; o += aw
    wv = w_in[:, o:o + aw]; o += aw
    wf = w_in[:, o:o + n_heads]; o += n_heads
    wgp = w_in[:, o:o + d]; o += d
    wga = w_in[:, o:o + d]

    place = np.zeros((LANES, n_heads * HEAD_PAD), np.float32)
    qones = np.zeros((1, HEAD_PAD), np.float32)
    qones[0, head_dim:head_dim + F_PARTS] = 1.0
    headsel = np.zeros((aw, LANES), np.float32)
    for h in range(n_heads):
        headsel[h * head_dim:(h + 1) * head_dim, h] = 1.0
        for part in range(F_PARTS):
            place[part * n_heads + h, h * HEAD_PAD + head_dim + part] = -1.0
    row = lambda x: x.reshape(1, -1).astype(F32)
    bf = lambda x: x.astype(BF16)
    return {
        "g_ffn1": row(g_ffn1), "wg1": bf(w_ffn1_gate), "wu1": bf(w_ffn1_up), "wd1": bf(w_ffn1_down),
        "g_mix": row(g_mix), "wpool": bf(wpool), "wq": bf(wq), "wk": bf(wk), "wv": bf(wv),
        "wq2": bf(wq * LOG2E),
        "wkt": bf(wk.T), "wvt": bf(wv.T),
        "wf": bf(jnp.pad(wf, ((0, 0), (0, LANES - n_heads)))),
        "bf": jnp.pad(row(b_forget), ((0, 0), (0, LANES - n_heads))),
        "wgp": bf(wgp), "wga": bf(wga), "wpg": bf(w_pool_group), "pscale": row(pool_scale),
        "place": jnp.asarray(place, BF16), "qones": jnp.asarray(qones, F32),
        "headsel": jnp.asarray(headsel, BF16),
        "tri": jnp.asarray(np.tril(np.ones((tm, tm), np.float32)), BF16),
        "wab": bf(w_attn_branch), "wout": bf(w_out),
        "g_ffn2": row(g_ffn2), "wg2": bf(w_ffn2_gate), "wu2": bf(w_ffn2_up), "wd2": bf(w_ffn2_down),
        "g_ple": row(g_ple), "wpgate": bf(w_ple_gate), "wple": bf(w_ple), "g_final": row(g_final),
    }


def kernel(x_prompt, x_sample, cache_k, cache_v, cache_logf, state_pool, page_table, p_prompt, p_sample, g_ffn1, w_ffn1_gate, w_ffn1_up, w_ffn1_down, g_mix, w_in, b_forget, w_pool_group, pool_scale, w_attn_branch, w_out, g_ffn2, w_ffn2_gate, w_ffn2_up, w_ffn2_down, g_ple, w_ple_gate, w_ple, g_final):
    depth = cache_k.shape[0]
    assert depth == 1, "one trunk layer"
    b, t, d = x_prompt.shape
    bd, n_new, _ = x_sample.shape
    _, n_pool_pages, page, n_heads, head_dim = cache_k.shape
    aw = n_heads * head_dim
    n_pre, pool_width = state_pool.shape[2], state_pool.shape[3]
    assert head_dim + F_PARTS <= HEAD_PAD and head_dim < V_ROWS and page == LANES
    assert pool_width == len(POOL_WINDOWS) * LANES and n_pre == POOL_WINDOWS[-1] - 1
    assert n_heads % ATTN_HEADS == 0
    tm = min(TOKEN_TILE, t)
    assert t // tm <= LANES
    ns = bd * n_new

    w = _prep_weights(g_ffn1[0], w_ffn1_gate[0], w_ffn1_up[0], w_ffn1_down[0], g_mix[0], w_in[0],
                      b_forget[0], w_pool_group[0], pool_scale[0], w_attn_branch[0], w_out[0],
                      g_ffn2[0], w_ffn2_gate[0], w_ffn2_up[0], w_ffn2_down[0], g_ple[0],
                      w_ple_gate[0], w_ple[0], g_final, n_heads, head_dim, pool_width, tm)

    step_major = lambda x: jnp.swapaxes(x, 0, 1).reshape(ns, x.shape[-1])
    seq_major = lambda x: jnp.swapaxes(x.reshape(n_new, bd, x.shape[-1]), 0, 1)
    hs1 = _ffn(step_major(x_sample), w["g_ffn1"], w["wg1"], w["wu1"], w["wd1"], ns)
    state = jnp.swapaxes(state_pool[0], 0, 1).reshape(n_pre * bd, pool_width)
    z_s, q_s, k_s, v_s, lf_s, fc_s, gpool_s, sga_s = _proj_sample(hs1, state, w, n_heads, n_new, bd)
    k_s, v_s, lf_s = seq_major(k_s), seq_major(v_s), seq_major(lf_s)
    new_page = lambda x: jnp.pad(jnp.swapaxes(x, 1, 2).astype(BF16), ((0, 0), (0, 0), (0, page - n_new)))
    negc = -jnp.swapaxes(seq_major(fc_s), 1, 2)
    negc = jnp.pad(negc, ((0, 0), (0, 0), (0, LANES - n_new)))
    upper = jnp.asarray(np.triu(np.ones((page, page), np.float32)), BF16)
    pages_t = lambda c: jnp.transpose(c[0], (0, 2, 3, 1)).reshape(n_pool_pages, aw, page)

    h1, attn_s = _ffn_decode(x_prompt.reshape(b * t, d), w["g_ffn1"], w["wg1"], w["wu1"], w["wd1"], tm,
                             page_table, seq_major(q_s), new_page(k_s), new_page(v_s), negc, upper,
                             pages_t(cache_k), pages_t(cache_v), jnp.swapaxes(cache_logf[0], 1, 2),
                             n_heads)
    h1 = h1.reshape(b, t, d)
    qa, ka, vt, kt_p, vt_p, lf_p, gpool, sga, zlast, qn, kn, fe = _proj_prompt(h1, w, tm, n_heads)
    stat = lambda x: jnp.swapaxes(x[:, :, 0, :n_heads], 1, 2).reshape(
        b, n_heads // ATTN_HEADS, ATTN_HEADS, t // tm)
    attn_t = _attention(qa, ka, vt, stat(qn), stat(kn), stat(fe), head_dim, tm)
    y_prompt = _post(h1, gpool, sga, attn_t, p_prompt[0], w, tm, True)
    heads_last = lambda x: jnp.transpose(x.reshape(b, n_heads, head_dim, t), (0, 3, 1, 2))[None]

    y_s = _post(hs1[None], gpool_s[None], sga_s[None], step_major(attn_s)[None],
                step_major(p_sample[0])[None], w, ns, False)
    y_sample = seq_major(y_s[0])
    pool_sample = jnp.concatenate([state_pool[0], seq_major(z_s)], axis=1)[:, -n_pre:]

    return (y_prompt, y_sample, heads_last(kt_p), heads_last(vt_p),
            lf_p[None], zlast[None, :, POOL_HALO - n_pre:],
            k_s.reshape(1, bd, n_new, n_heads, head_dim), v_s.reshape(1, bd, n_new, n_heads, head_dim),
            lf_s[None], pool_sample[None])
```

```python
import functools

import numpy as np
import jax
import jax.numpy as jnp
from jax import lax
from jax.experimental import pallas as pl
from jax.experimental.pallas import tpu as pltpu

F32 = jnp.float32
BF16 = jnp.bfloat16

RMS_EPS = 1e-6
MASKED = -1e30
POOL_WINDOWS = (2, 4, 8, 16)
POOL_HALO = 16

LANES = 128
HEAD_PAD = 128
V_ROWS = 80
F_PARTS = 3
FF_CHUNK = 768
TOKEN_TILE = 512
DECODE_PAGES = 8
DECODE_SLOTS = 4
DECODE_MERGE = 2
ATTN_HEADS = 2
ATTN_GROUPS = (1, 2)
VMEM_LIMIT = 56 * 1024 * 1024
POST_DECODE_VMEM_LIMIT = 60 * 1024 * 1024
LOG2E = 1.4426950408889634
NORM_SLACK = 1.01
SKIP_MARGIN = 140.0

_NT = (((1,), (1,)), ((), ()))
_TN = (((0,), (0,)), ((), ()))


def _rms(x, g):
    r = lax.rsqrt(jnp.mean(x * x, axis=-1, keepdims=True) + RMS_EPS)
    return x * r * g


def _dot(a, b):
    return jnp.dot(a, b, preferred_element_type=F32)


def _chunks(n, c):
    return [(s, min(c, n - s)) for s in range(0, n, c)]


def _split3(x):
    hi = x.astype(BF16).astype(F32)
    r = x - hi
    mid = r.astype(BF16).astype(F32)
    lo = (r - mid).astype(BF16).astype(F32)
    return hi, mid, lo


def _pack3(x, n):
    hi, mid, lo = _split3(x)
    return (hi + pltpu.roll(mid, n, 1) + pltpu.roll(lo, 2 * n, 1)).astype(BF16)


def _unpack3(c, n):
    return c + pltpu.roll(c, LANES - n, 1) + pltpu.roll(c, LANES - 2 * n, 1)


def _log_sigmoid(x):
    return jnp.minimum(x, 0.0) - jnp.log1p(jnp.exp(-jnp.abs(x)))


def _const_spec(shape):
    nd = len(shape)
    return pl.BlockSpec(shape, lambda *_: (0,) * nd, pipeline_mode=pl.Buffered(1))


def _params(sem, vmem_limit=None):
    return pltpu.CompilerParams(dimension_semantics=sem, vmem_limit_bytes=vmem_limit or VMEM_LIMIT)


def _ffn_value(x, g, wg_ref, wu_ref, wd_ref, after_chunk=None):
    u = _rms(x, g).astype(BF16)
    acc = None
    chunks = _chunks(wg_ref.shape[1], FF_CHUNK)
    for c, (s, n) in enumerate(chunks):
        gate = _dot(u, wg_ref[:, s:s + n])
        up = _dot(u, wu_ref[:, s:s + n])
        a = (gate * jax.nn.sigmoid(gate) * up).astype(BF16)
        d = _dot(a, wd_ref[s:s + n, :])
        acc = d if acc is None else acc + d
        if after_chunk is not None:
            after_chunk(c, len(chunks))
    return x + 0.5 * acc


def _ffn_kernel(x_ref, g_ref, wg_ref, wu_ref, wd_ref, o_ref):
    o_ref[...] = _ffn_value(x_ref[...], g_ref[...], wg_ref, wu_ref, wd_ref)


def _ffn(x, g, wg, wu, wd, tm):
    n, d = x.shape
    return pl.pallas_call(
        _ffn_kernel,
        grid=(n // tm,),
        in_specs=[pl.BlockSpec((tm, d), lambda i: (i, 0)),
                  _const_spec(g.shape), _const_spec(wg.shape), _const_spec(wu.shape),
                  _const_spec(wd.shape)],
        out_specs=pl.BlockSpec((tm, d), lambda i: (i, 0)),
        out_shape=jax.ShapeDtypeStruct((n, d), F32),
        compiler_params=_params(("parallel",)),
        name="ffn",
    )(x, g, wg, wu, wd)


def _pool_branch(pooled_groups, wpg_ref):
    outs = [_dot(p.astype(BF16), wpg_ref[g]) for g, p in enumerate(pooled_groups)]
    return jnp.concatenate(outs, axis=1)


def _head_lane_mask(shape, n_heads):
    lane = lax.broadcasted_iota(jnp.int32, shape, 1)
    return lane < n_heads


def _proj_prompt_kernel(h_ref, g_ref, wpool_ref, wq_ref, wk_ref, wkt_ref, wvt_ref,
                        wf_ref, bf_ref, wgp_ref, wga_ref, wpg_ref, pscale_ref, place_ref,
                        tri_ref, qones_ref, headsel_ref,
                        qa_ref, ka_ref, vt_ref, kt_ref, vto_ref, lf_ref, gpool_ref, sga_ref,
                        zlast_ref, qn_ref, kn_ref, fe_ref, zext_ref, fcarry_ref, *, n_heads):
    i = pl.program_id(1)
    tm = h_ref.shape[0]
    gw = LANES

    @pl.when(i == 0)
    def _():
        zext_ref[0:POOL_HALO, :] = jnp.zeros((POOL_HALO, zext_ref.shape[1]), F32)
        fcarry_ref[...] = jnp.zeros(fcarry_ref.shape, F32)

    u = _rms(h_ref[...], g_ref[...]).astype(BF16)

    z = _dot(u, wpool_ref[...])
    zext_ref[POOL_HALO:POOL_HALO + tm, :] = z
    row = lax.broadcasted_iota(jnp.int32, (tm, gw), 0) + i * tm
    pooled = []
    for g, w in enumerate(POOL_WINDOWS):
        sl = slice(g * gw, (g + 1) * gw)
        zg = z[:, sl]
        acc = zg
        for j in range(1, w):
            acc = acc + zext_ref[POOL_HALO - j:POOL_HALO - j + tm, sl]
        cnt = jnp.minimum(row + 1, w).astype(F32)
        pooled.append(acc / cnt - zg)
    zlast_ref[...] = zext_ref[tm:tm + POOL_HALO, :]
    zext_ref[0:POOL_HALO, :] = zext_ref[tm:tm + POOL_HALO, :]
    branch_pool = _pool_branch(pooled, wpg_ref) * pscale_ref[...]
    gpool_ref[...] = (jax.nn.sigmoid(_dot(u, wgp_ref[...])) * branch_pool).astype(BF16)
    sga_ref[...] = jax.nn.sigmoid(_dot(u, wga_ref[...])).astype(BF16)

    lane_ok = _head_lane_mask((tm, LANES), n_heads)
    logf = jnp.where(lane_ok, _log_sigmoid(_dot(u, wf_ref[...]) + bf_ref[...]), 0.0)
    lf_ref[...] = logf[:, 0:n_heads]
    csum = _unpack3(_dot(tri_ref[...], _pack3(logf, n_heads)), n_heads)
    fcum = jnp.where(lane_ok, csum + fcarry_ref[...], 0.0)
    fcarry_ref[...] = fcum[tm - 1:tm, :]
    fcum2 = fcum * LOG2E
    fe_ref[...] = fcum2[tm - 1:tm, :]

    q = _dot(u, wq_ref[...]).astype(BF16).astype(F32)
    k = _dot(u, wk_ref[...]).astype(BF16).astype(F32)
    for x, n_ref in ((q, qn_ref), (k, kn_ref)):
        sq = (x * x * NORM_SLACK).astype(BF16)
        n_ref[...] = jnp.max(_dot(sq, headsel_ref[...]), axis=0, keepdims=True)
    kt = lax.dot_general(wkt_ref[...], u, _NT, preferred_element_type=F32)
    vt = lax.dot_general(wvt_ref[...], u, _NT, preferred_element_type=F32)
    kt_ref[...] = kt
    vto_ref[...] = vt

    head_dim = kt.shape[0] // n_heads
    per_slab = LANES // head_dim
    fcols = _dot(_pack3(fcum2, n_heads), place_ref[...])
    low = lax.broadcasted_iota(jnp.int32, (tm, LANES), 1) < head_dim
    rid = lax.broadcasted_iota(jnp.int32, (V_ROWS - head_dim, tm), 0)
    ones_rows = jnp.where(rid == 0, 1.0, 0.0).astype(BF16)
    for h in range(n_heads):
        col = (h // per_slab) * LANES
        shift = (h % per_slab) * head_dim
        qs, ks = q[:, col:col + LANES], k[:, col:col + LANES]
        if shift:
            qs, ks = pltpu.roll(qs, LANES - shift, 1), pltpu.roll(ks, LANES - shift, 1)
        qa_ref[h] = jnp.where(low, qs, qones_ref[...]).astype(BF16)
        ka_ref[h] = jnp.where(low, ks, fcols[:, h * HEAD_PAD:(h + 1) * HEAD_PAD]).astype(BF16)
        vt_ref[h, 0:head_dim, :] = vt[h * head_dim:(h + 1) * head_dim, :].astype(BF16)
        vt_ref[h, head_dim:V_ROWS, :] = ones_rows


def _proj_prompt(h, w, tm, n_heads):
    b, t, d = h.shape
    aw = w["wkt"].shape[0]
    pw = w["wpool"].shape[1]
    nt = t // tm
    names = ["g_mix", "wpool", "wq2", "wk", "wkt", "wvt", "wf", "bf", "wgp", "wga", "wpg",
             "pscale", "place", "tri", "qones", "headsel"]
    consts = [w[n] for n in names]
    tok = lambda width: pl.BlockSpec((None, tm, width), lambda bi, i: (bi, i, 0))
    tok_t = lambda rows: pl.BlockSpec((None, rows, tm), lambda bi, i: (bi, 0, i))
    head = pl.BlockSpec((None, n_heads, tm, HEAD_PAD), lambda bi, i: (bi, 0, i, 0))
    stat = pl.BlockSpec((None, None, 1, LANES), lambda bi, i: (bi, i, 0, 0))
    stat_shape = jax.ShapeDtypeStruct((b, nt, 1, LANES), F32)
    out_shape = (
        jax.ShapeDtypeStruct((b, n_heads, t, HEAD_PAD), BF16),
        jax.ShapeDtypeStruct((b, n_heads, t, HEAD_PAD), BF16),
        jax.ShapeDtypeStruct((b, n_heads, V_ROWS, t), BF16),
        jax.ShapeDtypeStruct((b, aw, t), F32),
        jax.ShapeDtypeStruct((b, aw, t), F32),
        jax.ShapeDtypeStruct((b, t, n_heads), F32),
        jax.ShapeDtypeStruct((b, t, d), BF16),
        jax.ShapeDtypeStruct((b, t, d), BF16),
        jax.ShapeDtypeStruct((b, POOL_HALO, pw), F32),
        stat_shape, stat_shape, stat_shape,
    )
    out_specs = (
        head, head,
        pl.BlockSpec((None, n_heads, V_ROWS, tm), lambda bi, i: (bi, 0, 0, i)),
        tok_t(aw), tok_t(aw), tok(n_heads), tok(d), tok(d),
        pl.BlockSpec((None, POOL_HALO, pw), lambda bi, i: (bi, 0, 0)),
        stat, stat, stat,
    )
    return pl.pallas_call(
        functools.partial(_proj_prompt_kernel, n_heads=n_heads),
        grid=(b, nt),
        in_specs=[tok(d)] + [_const_spec(c.shape) for c in consts],
        out_specs=out_specs,
        out_shape=out_shape,
        scratch_shapes=[pltpu.VMEM((tm + POOL_HALO, pw), F32), pltpu.VMEM((1, LANES), F32)],
        compiler_params=_params(("arbitrary", "arbitrary")),
        name="proj_prompt",
    )(h, *consts)


def _proj_sample_kernel(h_ref, state_ref, g_ref, wpool_ref, wq_ref, wk_ref, wv_ref, wf_ref,
                        bf_ref, wgp_ref, wga_ref, wpg_ref, pscale_ref,
                        z_ref, q_ref, k_ref, v_ref, lf_ref, fc_ref, gpool_ref, sga_ref,
                        *, n_heads, n_new, n_seq):
    gw = LANES
    n_pre = state_ref.shape[0] // n_seq
    u = _rms(h_ref[...], g_ref[...]).astype(BF16)

    z = _dot(u, wpool_ref[...])
    z_ref[...] = z

    def ext_rows(r, sl):
        if r < n_pre:
            return state_ref[r * n_seq:(r + 1) * n_seq, sl]
        return z[(r - n_pre) * n_seq:(r - n_pre + 1) * n_seq, sl]

    pooled = []
    for g, w in enumerate(POOL_WINDOWS):
        sl = slice(g * gw, (g + 1) * gw)
        steps = []
        for s in range(n_new):
            r = n_pre + s
            acc = ext_rows(r, sl)
            for j in range(1, w):
                if r - j >= 0:
                    acc = acc + ext_rows(r - j, sl)
            steps.append(acc / float(min(r + 1, w)) - ext_rows(r, sl))
        pooled.append(jnp.concatenate(steps, axis=0))
    branch_pool = _pool_branch(pooled, wpg_ref) * pscale_ref[...]
    gpool_ref[...] = (jax.nn.sigmoid(_dot(u, wgp_ref[...])) * branch_pool).astype(BF16)
    sga_ref[...] = jax.nn.sigmoid(_dot(u, wga_ref[...])).astype(BF16)

    tm = h_ref.shape[0]
    lane_ok = _head_lane_mask((tm, LANES), n_heads)
    logf = jnp.where(lane_ok, _log_sigmoid(_dot(u, wf_ref[...]) + bf_ref[...]), 0.0)
    lf_ref[...] = logf[:, 0:n_heads]
    run = None
    sums = []
    for s in range(n_new):
        blk = logf[s * n_seq:(s + 1) * n_seq, :]
        run = blk if run is None else run + blk
        sums.append(run)
    fc_ref[...] = jnp.concatenate(sums, axis=0)[:, 0:n_heads]

    q_ref[...] = _dot(u, wq_ref[...]).astype(BF16)
    k_ref[...] = _dot(u, wk_ref[...])
    v_ref[...] = _dot(u, wv_ref[...])


def _proj_sample(h, state, w, n_heads, n_new, n_seq):
    n, d = h.shape
    aw = w["wk"].shape[1]
    pw = w["wpool"].shape[1]
    names = ["g_mix", "wpool", "wq", "wk", "wv", "wf", "bf", "wgp", "wga", "wpg", "pscale"]
    consts = [w[n_] for n_ in names]
    full = lambda shape: pl.BlockSpec(shape, lambda i: (0,) * len(shape))
    out_shape = (
        jax.ShapeDtypeStruct((n, pw), F32),
        jax.ShapeDtypeStruct((n, aw), BF16),
        jax.ShapeDtypeStruct((n, aw), F32),
        jax.ShapeDtypeStruct((n, aw), F32),
        jax.ShapeDtypeStruct((n, n_heads), F32),
        jax.ShapeDtypeStruct((n, n_heads), F32),
        jax.ShapeDtypeStruct((n, d), BF16),
        jax.ShapeDtypeStruct((n, d), BF16),
    )
    return pl.pallas_call(
        functools.partial(_proj_sample_kernel, n_heads=n_heads, n_new=n_new, n_seq=n_seq),
        grid=(1,),
        in_specs=[full(h.shape), full(state.shape)] + [_const_spec(c.shape) for c in consts],
        out_specs=tuple(full(s.shape) for s in out_shape),
        out_shape=out_shape,
        compiler_params=_params(("arbitrary",)),
        name="proj_sample",
    )(h, state, *consts)


def _attn_kernel(q_ref, k_ref, vt_ref, qn_ref, kn_ref, fe_ref, o_ref, *, head_dim):
    i = pl.program_id(2)
    nh, tq, _ = q_ref.shape
    tk = tq
    nt = fe_ref.shape[1]
    qs = [q_ref[h] for h in range(nh)]

    def scores(h, j):
        start = pl.multiple_of(j * tk, tk)
        return lax.dot_general(k_ref[h, pl.ds(start, tk), :], qs[h], _NT, preferred_element_type=F32)

    def absorb(h, j, s, m, acc):
        start = pl.multiple_of(j * tk, tk)
        m_new = jnp.maximum(m, jnp.max(s, axis=0, keepdims=True))
        alpha = jnp.exp2(m - m_new)
        p = jnp.exp2(s - m_new).astype(BF16)
        acc = alpha * acc + _dot(vt_ref[h, :, pl.ds(start, tk)], p)
        return m_new, acc

    kpos = lax.broadcasted_iota(jnp.int32, (tk, tq), 0)
    qpos = lax.broadcasted_iota(jnp.int32, (tk, tq), 1)
    diag = [jnp.where(kpos <= qpos, scores(h, i), MASKED) for h in range(nh)]
    state = []
    for h in range(nh):
        state += list(absorb(h, i, diag[h], jnp.full((1, tq), MASKED, F32),
                             jnp.zeros((V_ROWS, tq), F32)))

    lane = lax.broadcasted_iota(jnp.int32, (1, nt), 1)
    n_back = None
    for h in range(nh):
        m_min = jnp.min(state[2 * h], axis=1, keepdims=True)
        q2 = jnp.max(jnp.where(lane == i, qn_ref[h:h + 1, :], 0.0), axis=1, keepdims=True)
        k2 = jnp.max(kn_ref[h:h + 1, :], axis=1, keepdims=True)
        bound = jnp.sqrt(q2 * k2) - fe_ref[h:h + 1, :]
        need = (lane < i) & (bound > m_min - SKIP_MARGIN)
        n_h = jnp.sum(need.astype(jnp.int32))
        n_back = n_h if n_back is None else jnp.maximum(n_back, n_h)

    def make_body(group):
        def body(jj, c):
            c = list(c)
            first = c[-1]
            tiles = [first - jj * group - g for g in range(group)]
            ss = [[scores(h, j) for h in range(nh)] for j in tiles]
            for g, j in enumerate(tiles):
                for h in range(nh):
                    c[2 * h], c[2 * h + 1] = absorb(h, j, ss[g][h], c[2 * h], c[2 * h + 1])
            return tuple(c)
        return body

    left, nearest, state = n_back, i - 1, tuple(state)
    for group in ATTN_GROUPS:
        bigger = [x for x in ATTN_GROUPS if x > group]
        trips = (left % min(bigger)) // group if bigger else left // group
        state = lax.fori_loop(0, trips, make_body(group), state + (nearest,))[:-1]
        left, nearest = left - trips * group, nearest - trips * group
    for h in range(nh):
        acc = state[2 * h + 1]
        o_ref[h * head_dim:(h + 1) * head_dim, :] = (
            acc[0:head_dim, :] / acc[head_dim:head_dim + 1, :]).astype(BF16)


def _attention(qa, ka, vt, qn, kn, fe, head_dim, tq):
    b, nh, t, _ = qa.shape
    g = ATTN_HEADS
    nt = t // tq
    stat = pl.BlockSpec((None, None, g, nt), lambda bi, h, i: (bi, h, 0, 0))
    return pl.pallas_call(
        functools.partial(_attn_kernel, head_dim=head_dim),
        grid=(b, nh // g, nt),
        in_specs=[pl.BlockSpec((None, g, tq, HEAD_PAD), lambda bi, h, i: (bi, h, i, 0)),
                  pl.BlockSpec((None, g, t, HEAD_PAD), lambda bi, h, i: (bi, h, 0, 0)),
                  pl.BlockSpec((None, g, V_ROWS, t), lambda bi, h, i: (bi, h, 0, 0)),
                  stat, stat, stat],
        out_specs=pl.BlockSpec((None, g * head_dim, tq), lambda bi, h, i: (bi, h, i)),
        out_shape=jax.ShapeDtypeStruct((b, nh * head_dim, t), BF16),
        compiler_params=_params(("parallel", "parallel", "arbitrary")),
        name="attn_prompt",
    )(qa, ka, vt, qn, kn, fe)


def _page_copies(ck_hbm, cv_hbm, cl_hbm, kbuf, vbuf, lbuf, sems, slot, page_ids):
    page = cl_hbm.shape[2]
    out = []
    for i, pid in enumerate(page_ids):
        lanes = slice(i * page, (i + 1) * page)
        out += [pltpu.make_async_copy(ck_hbm.at[pid], kbuf.at[slot, :, lanes], sems.at[slot, 0]),
                pltpu.make_async_copy(cv_hbm.at[pid], vbuf.at[slot, :, lanes], sems.at[slot, 1]),
                pltpu.make_async_copy(cl_hbm.at[pid], lbuf.at[slot, i], sems.at[slot, 2])]
    return out


def _decode_prologue(refs):
    pt_ref, ck_hbm, cv_hbm, cl_hbm, kbuf, vbuf, lbuf, sems = refs[0], *refs[6:9], *refs[10:14]
    slots = kbuf.shape[0]
    pp = kbuf.shape[2] // cl_hbm.shape[2]
    for s in range(slots):
        for c in _page_copies(ck_hbm, cv_hbm, cl_hbm, kbuf, vbuf, lbuf, sems, s,
                              [pt_ref[s * pp + i] for i in range(pp)]):
            c.start()


def _decode_step(d, n_dsteps, seq, part, n_parts, refs, n_heads, n_new):
    (pt_ref, q_ref, kn_ref, vn_ref, negc_ref, upper_ref, ck_hbm, cv_hbm, cl_hbm, o_ref,
     kbuf, vbuf, lbuf, sems, m_ref, l_ref, acc_ref, fcar_ref) = refs
    slots, width, group_keys = kbuf.shape
    page = cl_hbm.shape[2]
    pp = group_keys // page
    head_dim = width // n_heads
    rows = n_new * n_heads

    def copies(slot, page_ids):
        return _page_copies(ck_hbm, cv_hbm, cl_hbm, kbuf, vbuf, lbuf, sems, slot, page_ids)

    def start(slot, first):
        for c in copies(slot, [pt_ref[first + i] for i in range(pp)]):
            c.start()

    def wait(slot):
        for c in copies(slot, [0] * pp):
            c.wait()

    if part == 0:
        m_ref[...] = jnp.full(m_ref.shape, MASKED, F32)
        l_ref[...] = jnp.zeros(l_ref.shape, F32)
        acc_ref[...] = jnp.zeros(acc_ref.shape, F32)
        fcar_ref[...] = jnp.zeros(fcar_ref.shape, F32)

    q = q_ref[seq].astype(F32)
    qrows = jnp.concatenate([jnp.broadcast_to(q[s:s + 1, :], (n_heads, width)) for s in range(n_new)],
                            axis=0)
    rid = lax.broadcasted_iota(jnp.int32, (rows, width), 0)
    cid = lax.broadcasted_iota(jnp.int32, (rows, width), 1)
    own = (cid // head_dim) == (rid % n_heads)
    qbd = jnp.where(own, qrows, 0.0)

    def update(s, vals_t):
        m_new = jnp.maximum(m_ref[...], jnp.max(s, axis=1, keepdims=True))
        alpha = jnp.exp(m_ref[...] - m_new)
        p = jnp.exp(s - m_new)
        l_ref[...] = alpha * l_ref[...] + jnp.sum(p, axis=1, keepdims=True)
        pv, at = None, 0
        for v in vals_t:
            part = lax.dot_general(p[:, at:at + v.shape[1]], v, _NT, preferred_element_type=F32)
            pv = part if pv is None else pv + part
            at += v.shape[1]
        acc_ref[...] = alpha * acc_ref[...] + pv
        m_ref[...] = m_new

    def consume(group):
        x = jnp.concatenate([lbuf[slot, i] for slot in group for i in range(pp)], axis=0)
        parts = jnp.concatenate(_split3(x), axis=0).astype(BF16)
        c = _dot(parts, upper_ref[...])
        nr = len(group) * pp * n_heads
        fin = c[0:nr] + c[nr:2 * nr] + c[2 * nr:3 * nr]
        carry = fcar_ref[...]
        biases = []
        for i in range(len(group) * pp):
            fi = fin[i * n_heads:(i + 1) * n_heads, :]
            biases.append(jnp.concatenate([-(fi + carry)] * n_new, axis=0))
            carry = carry + jnp.broadcast_to(fi[:, page - 1:page], carry.shape)
        fcar_ref[...] = carry
        scores = jnp.concatenate([_dot(qbd, kbuf[slot]) for slot in group], axis=1)
        update(scores + jnp.concatenate(biases, axis=1), [vbuf[slot] for slot in group])

    first = d * (slots * pp)
    for s0 in range(0, slots, DECODE_MERGE):
        group = list(range(s0, s0 + DECODE_MERGE))
        for s in group:
            wait(s)
        consume(group)

        @pl.when(d + 1 < n_dsteps)
        def _():
            for s in group:
                start(s, first + (slots + s) * pp)

    if part == n_parts - 1:
        carry = fcar_ref[...]
        sn = _dot(qbd, kn_ref[seq].astype(F32))
        bias = jnp.concatenate([negc_ref[seq] - carry] * n_new, axis=0)
        r2 = lax.broadcasted_iota(jnp.int32, sn.shape, 0)
        c2 = lax.broadcasted_iota(jnp.int32, sn.shape, 1)
        ok = (c2 < n_new) & (c2 <= r2 // n_heads)
        update(jnp.where(ok, sn + bias, MASKED), [vn_ref[seq].astype(F32)])
        out = jnp.where(own, acc_ref[...] / l_ref[...], 0.0)
        o_ref[seq] = jnp.sum(out.reshape(n_new, n_heads, width), axis=1)


N_DECODE_IN = 8


def _decode_hook(pt_ref, dec_in, o_ref, scratch, tile, n_tiles, n_parts, n_heads, n_new):
    refs = (pt_ref,) + tuple(dec_in) + (o_ref,) + tuple(scratch)
    per_tile = o_ref.shape[0] * n_parts
    n_dsteps = n_tiles * per_tile

    @pl.when(tile == 0)
    def _():
        _decode_prologue(refs)

    done = [0]

    def after_chunk(c, n_chunks):
        upto = ((c + 1) * per_tile) // n_chunks
        for k in range(done[0], upto):
            _decode_step(tile * per_tile + k, n_dsteps, k // n_parts, k % n_parts, n_parts, refs,
                         n_heads, n_new)
        done[0] = upto

    return after_chunk


def _decode_operands(page_table, q, kn_t, vn_t, negc, upper, cache_kt, cache_vt, cache_lft, n_tiles,
                     flat_tile, n_heads):
    bd, n_new, width = q.shape
    n_pages = page_table.shape[1]
    page = cache_kt.shape[2]
    slots = DECODE_SLOTS
    pp = min(DECODE_PAGES, n_pages // slots)
    n_parts = n_pages // (slots * pp)
    assert n_pages == n_parts * slots * pp and (bd * n_parts) % n_tiles == 0
    per_tile = bd * n_parts // n_tiles
    assert per_tile % n_parts == 0
    seqs = per_tile // n_parts
    rows = n_new * n_heads
    seq = lambda shape: pl.BlockSpec((seqs,) + shape, lambda *idx: (flat_tile(*idx[:-1]), 0, 0))
    hbm = pl.BlockSpec(memory_space=pl.ANY)
    return {
        "n_parts": n_parts,
        "page_table": page_table.reshape(-1),
        "inputs": [q, kn_t, vn_t, negc, upper, cache_kt, cache_vt, cache_lft],
        "in_specs": [seq((n_new, width)), seq((width, page)), seq((width, page)),
                     seq((n_heads, LANES)), _const_spec(upper.shape), hbm, hbm, hbm],
        "out_spec": seq((n_new, width)),
        "out_shape": jax.ShapeDtypeStruct((bd, n_new, width), F32),
        "scratch": [pltpu.VMEM((slots, width, pp * page), F32),
                    pltpu.VMEM((slots, width, pp * page), F32),
                    pltpu.VMEM((slots, pp, n_heads, page), F32),
                    pltpu.SemaphoreType.DMA((slots, 3)),
                    pltpu.VMEM((rows, 1), F32), pltpu.VMEM((rows, 1), F32),
                    pltpu.VMEM((rows, width), F32), pltpu.VMEM((n_heads, LANES), F32)],
    }


def _ffn_decode_kernel(pt_ref, x_ref, g_ref, wg_ref, wu_ref, wd_ref, *refs, n_heads, n_new, n_parts):
    dec_in, (h_ref, o_ref), scratch = refs[:N_DECODE_IN], refs[N_DECODE_IN:N_DECODE_IN + 2], \
        refs[N_DECODE_IN + 2:]
    hook = _decode_hook(pt_ref, dec_in, o_ref, scratch, pl.program_id(0), pl.num_programs(0), n_parts,
                        n_heads, n_new)
    h_ref[...] = _ffn_value(x_ref[...], g_ref[...], wg_ref, wu_ref, wd_ref, hook)


def _ffn_decode(x, g, wg, wu, wd, tm, n_heads, *decode_args):
    n, d = x.shape
    dec = _decode_operands(*decode_args, n // tm, lambda i: i, n_heads)
    tok = pl.BlockSpec((tm, d), lambda i, pt_ref: (i, 0))
    grid_spec = pltpu.PrefetchScalarGridSpec(
        num_scalar_prefetch=1,
        grid=(n // tm,),
        in_specs=[tok, _const_spec(g.shape), _const_spec(wg.shape), _const_spec(wu.shape),
                  _const_spec(wd.shape)] + dec["in_specs"],
        out_specs=(tok, dec["out_spec"]),
        scratch_shapes=dec["scratch"],
    )
    n_new = dec["out_shape"].shape[1]
    return pl.pallas_call(
        functools.partial(_ffn_decode_kernel, n_heads=n_heads, n_new=n_new, n_parts=dec["n_parts"]),
        grid_spec=grid_spec,
        out_shape=(jax.ShapeDtypeStruct((n, d), F32), dec["out_shape"]),
        compiler_params=_params(("arbitrary",)),
        name="ffn_decode",
    )(dec["page_table"], x, g, wg, wu, wd, *dec["inputs"])


N_POST_IN = 15


def _post_value(h_ref, gpool_ref, sga_ref, attn_ref, p_ref, wab_ref, wout_ref, g2_ref, wg_ref,
                wu_ref, wd_ref, gple_ref, wpg_ref, wple_ref, gfin_ref, attn_transposed, after_chunk):
    if attn_transposed:
        branch_attn = lax.dot_general(attn_ref[...], wab_ref[...], _TN, preferred_element_type=F32)
    else:
        branch_attn = _dot(attn_ref[...].astype(BF16), wab_ref[...])
    merged = gpool_ref[...].astype(F32) + sga_ref[...].astype(F32) * branch_attn
    h = h_ref[...] + _dot(merged.astype(BF16), wout_ref[...])
    h = _ffn_value(h, g2_ref[...], wg_ref, wu_ref, wd_ref, after_chunk)
    gate = jax.nn.sigmoid(_dot(_rms(h, gple_ref[...]).astype(BF16), wpg_ref[...]))
    h = h + _dot(p_ref[...].astype(BF16), wple_ref[...]) * gate
    return _rms(h, gfin_ref[...])


def _post_kernel(*refs, attn_transposed):
    y_ref = refs[N_POST_IN]
    y_ref[...] = _post_value(*refs[:N_POST_IN], attn_transposed, None)


def _post_decode_kernel(pt_ref, *refs, attn_transposed, n_heads, n_new, n_parts):
    ins, refs = refs[:N_POST_IN], refs[N_POST_IN:]
    dec_in, (y_ref, o_ref), scratch = refs[:N_DECODE_IN], refs[N_DECODE_IN:N_DECODE_IN + 2], \
        refs[N_DECODE_IN + 2:]
    tile = pl.program_id(0) * pl.num_programs(1) + pl.program_id(1)
    hook = _decode_hook(pt_ref, dec_in, o_ref, scratch, tile, pl.num_programs(0) * pl.num_programs(1),
                        n_parts, n_heads, n_new)
    y_ref[...] = _post_value(*ins, attn_transposed, hook)


def _post(h, gpool, sga, attn, p, w, tm, attn_transposed, n_heads=None, decode_args=None):
    b, t, d = h.shape
    nt = t // tm
    names = ["wab", "wout", "g_ffn2", "wg2", "wu2", "wd2", "g_ple", "wpgate", "wple", "g_final"]
    consts = [w[n] for n in names]
    tok = lambda width: pl.BlockSpec((None, tm, width), lambda bi, i, *_: (bi, i, 0))
    if attn_transposed:
        attn_spec = pl.BlockSpec((None, attn.shape[1], tm), lambda bi, i, *_: (bi, 0, i))
    else:
        attn_spec = tok(attn.shape[2])
    in_specs = [tok(d), tok(d), tok(d), attn_spec, tok(p.shape[2])] + [_const_spec(c.shape) for c in consts]
    y_shape = jax.ShapeDtypeStruct((b, t, d), F32)
    if decode_args is None:
        return pl.pallas_call(
            functools.partial(_post_kernel, attn_transposed=attn_transposed),
            grid=(b, nt),
            in_specs=in_specs,
            out_specs=tok(d),
            out_shape=y_shape,
            compiler_params=_params(("parallel", "parallel")),
            name="post",
        )(h, gpool, sga, attn, p, *consts)
    dec = _decode_operands(*decode_args, b * nt, lambda bi, i: bi * nt + i, n_heads)
    grid_spec = pltpu.PrefetchScalarGridSpec(
        num_scalar_prefetch=1,
        grid=(b, nt),
        in_specs=in_specs + dec["in_specs"],
        out_specs=(tok(d), dec["out_spec"]),
        scratch_shapes=dec["scratch"],
    )
    return pl.pallas_call(
        functools.partial(_post_decode_kernel, attn_transposed=attn_transposed, n_heads=n_heads,
                          n_new=dec["out_shape"].shape[1], n_parts=dec["n_parts"]),
        grid_spec=grid_spec,
        out_shape=(y_shape, dec["out_shape"]),
        compiler_params=_params(("arbitrary", "arbitrary"), POST_DECODE_VMEM_LIMIT),
        name="post_decode",
    )(dec["page_table"], h, gpool, sga, attn, p, *consts, *dec["inputs"])


def _prep_weights(g_ffn1, w_ffn1_gate, w_ffn1_up, w_ffn1_down, g_mix, w_in, b_forget, w_pool_group,
                  pool_scale, w_attn_branch, w_out, g_ffn2, w_ffn2_gate, w_ffn2_up, w_ffn2_down,
                  g_ple, w_ple_gate, w_ple, g_final, n_heads, head_dim, pool_width, tm):
    d = w_in.shape[0]
    aw = n_heads * head_dim
    o = 0
    wpool = w_in[:, o:o + pool_width]; o += pool_width
    wq = w_in[:, o:o + aw] * (head_dim ** -0.5); o += aw
    wk = w_in[:, o:o + aw]; o += aw
    wv = w_in[:, o:o + aw]; o += aw
    wf = w_in[:, o:o + n_heads]; o += n_heads
    wgp = w_in[:, o:o + d]; o += d
    wga = w_in[:, o:o + d]

    place = np.zeros((LANES, n_heads * HEAD_PAD), np.float32)
    qones = np.zeros((1, HEAD_PAD), np.float32)
    qones[0, head_dim:head_dim + F_PARTS] = 1.0
    headsel = np.zeros((aw, LANES), np.float32)
    for h in range(n_heads):
        headsel[h * head_dim:(h + 1) * head_dim, h] = 1.0
        for part in range(F_PARTS):
            place[part * n_heads + h, h * HEAD_PAD + head_dim + part] = -1.0
    row = lambda x: x.reshape(1, -1).astype(F32)
    bf = lambda x: x.astype(BF16)
    return {
        "g_ffn1": row(g_ffn1), "wg1": bf(w_ffn1_gate), "wu1": bf(w_ffn1_up), "wd1": bf(w_ffn1_down),
        "g_mix": row(g_mix), "wpool": bf(wpool), "wq": bf(wq), "wk": bf(wk), "wv": bf(wv),
        "wq2": bf(wq * LOG2E),
        "wkt": bf(wk.T), "wvt": bf(wv.T),
        "wf": bf(jnp.pad(wf, ((0, 0), (0, LANES - n_heads)))),
        "bf": jnp.pad(row(b_forget), ((0, 0), (0, LANES - n_heads))),
        "wgp": bf(wgp), "wga": bf(wga), "wpg": bf(w_pool_group), "pscale": row(pool_scale),
        "place": jnp.asarray(place, BF16), "qones": jnp.asarray(qones, F32),
        "headsel": jnp.asarray(headsel, BF16),
        "tri": jnp.asarray(np.tril(np.ones((tm, tm), np.float32)), BF16),
        "wab": bf(w_attn_branch), "wout": bf(w_out),
        "g_ffn2": row(g_ffn2), "wg2": bf(w_ffn2_gate), "wu2": bf(w_ffn2_up), "wd2": bf(w_ffn2_down),
        "g_ple": row(g_ple), "wpgate": bf(w_ple_gate), "wple": bf(w_ple), "g_final": row(g_final),
    }


def kernel(x_prompt, x_sample, cache_k, cache_v, cache_logf, state_pool, page_table, p_prompt, p_sample, g_ffn1, w_ffn1_gate, w_ffn1_up, w_ffn1_down, g_mix, w_in, b_forget, w_pool_group, pool_scale, w_attn_branch, w_out, g_ffn2, w_ffn2_gate, w_ffn2_up, w_ffn2_down, g_ple, w_ple_gate, w_ple, g_final):
    depth = cache_k.shape[0]
    assert depth == 1, "one trunk layer"
    b, t, d = x_prompt.shape
    bd, n_new, _ = x_sample.shape
    _, n_pool_pages, page, n_heads, head_dim = cache_k.shape
    aw = n_heads * head_dim
    n_pre, pool_width = state_pool.shape[2], state_pool.shape[3]
    assert head_dim + F_PARTS <= HEAD_PAD and head_dim < V_ROWS and page == LANES
    assert pool_width == len(POOL_WINDOWS) * LANES and n_pre == POOL_WINDOWS[-1] - 1
    assert n_heads % ATTN_HEADS == 0
    tm = min(TOKEN_TILE, t)
    assert t // tm <= LANES
    ns = bd * n_new

    w = _prep_weights(g_ffn1[0], w_ffn1_gate[0], w_ffn1_up[0], w_ffn1_down[0], g_mix[0], w_in[0],
                      b_forget[0], w_pool_group[0], pool_scale[0], w_attn_branch[0], w_out[0],
                      g_ffn2[0], w_ffn2_gate[0], w_ffn2_up[0], w_ffn2_down[0], g_ple[0],
                      w_ple_gate[0], w_ple[0], g_final, n_heads, head_dim, pool_width, tm)

    step_major = lambda x: jnp.swapaxes(x, 0, 1).reshape(ns, x.shape[-1])
    seq_major = lambda x: jnp.swapaxes(x.reshape(n_new, bd, x.shape[-1]), 0, 1)
    hs1 = _ffn(step_major(x_sample), w["g_ffn1"], w["wg1"], w["wu1"], w["wd1"], ns)
    state = jnp.swapaxes(state_pool[0], 0, 1).reshape(n_pre * bd, pool_width)
    z_s, q_s, k_s, v_s, lf_s, fc_s, gpool_s, sga_s = _proj_sample(hs1, state, w, n_heads, n_new, bd)
    k_s, v_s, lf_s = seq_major(k_s), seq_major(v_s), seq_major(lf_s)
    new_page = lambda x: jnp.pad(jnp.swapaxes(x, 1, 2).astype(BF16), ((0, 0), (0, 0), (0, page - n_new)))
    negc = -jnp.swapaxes(seq_major(fc_s), 1, 2)
    negc = jnp.pad(negc, ((0, 0), (0, 0), (0, LANES - n_new)))
    upper = jnp.asarray(np.triu(np.ones((page, page), np.float32)), BF16)
    pages_t = lambda c: jnp.transpose(c[0], (0, 2, 3, 1)).reshape(n_pool_pages, aw, page)

    dec_args = (page_table, seq_major(q_s), new_page(k_s), new_page(v_s), negc)
    caches = (upper, pages_t(cache_k), pages_t(cache_v), jnp.swapaxes(cache_logf[0], 1, 2))
    half = bd // 2
    h1, attn_a = _ffn_decode(x_prompt.reshape(b * t, d), w["g_ffn1"], w["wg1"], w["wu1"], w["wd1"], tm,
                             n_heads, *(a[:half] for a in dec_args), *caches)
    h1 = h1.reshape(b, t, d)
    qa, ka, vt, kt_p, vt_p, lf_p, gpool, sga, zlast, qn, kn, fe = _proj_prompt(h1, w, tm, n_heads)
    stat = lambda x: jnp.swapaxes(x[:, :, 0, :n_heads], 1, 2).reshape(
        b, n_heads // ATTN_HEADS, ATTN_HEADS, t // tm)
    attn_t = _attention(qa, ka, vt, stat(qn), stat(kn), stat(fe), head_dim, tm)
    y_prompt, attn_b = _post(h1, gpool, sga, attn_t, p_prompt[0], w, tm, True, n_heads,
                             tuple(a[half:] for a in dec_args) + caches)
    heads_last = lambda x: jnp.transpose(x.reshape(b, n_heads, head_dim, t), (0, 3, 1, 2))[None]

    attn_s = jnp.concatenate([attn_a, attn_b], axis=0)
    y_s = _post(hs1[None], gpool_s[None], sga_s[None], step_major(attn_s)[None],
                step_major(p_sample[0])[None], w, ns, False)
    y_sample = seq_major(y_s[0])
    pool_sample = jnp.concatenate([state_pool[0], seq_major(z_s)], axis=1)[:, -n_pre:]

    return (y_prompt, y_sample, heads_last(kt_p), heads_last(vt_p),
            lf_p[None], zlast[None, :, POOL_HALO - n_pre:],
            k_s.reshape(1, bd, n_new, n_heads, head_dim), v_s.reshape(1, bd, n_new, n_heads, head_dim),
            lf_s[None], pool_sample[None])
```

```python
import functools

import numpy as np
import jax
import jax.numpy as jnp
from jax import lax
from jax.experimental import pallas as pl
from jax.experimental.pallas import tpu as pltpu

F32 = jnp.float32
BF16 = jnp.bfloat16

RMS_EPS = 1e-6
MASKED = -1e30
POOL_WINDOWS = (2, 4, 8, 16)
POOL_HALO = 16

LANES = 128
HEAD_PAD = 128
V_ROWS = 80
F_PARTS = 3
FF_CHUNK = 768
TOKEN_TILE = 512
DECODE_PAGES = 8
DECODE_SLOTS = 4
DECODE_MERGE = 2
ATTN_HEADS = 2
ATTN_GROUPS = (1, 2)
VMEM_LIMIT = 56 * 1024 * 1024
POST_DECODE_VMEM_LIMIT = 60 * 1024 * 1024
LOG2E = 1.4426950408889634
NORM_SLACK = 1.01
SKIP_MARGIN = 140.0

_NT = (((1,), (1,)), ((), ()))
_TN = (((0,), (0,)), ((), ()))


def _rms(x, g):
    r = lax.rsqrt(jnp.mean(x * x, axis=-1, keepdims=True) + RMS_EPS)
    return x * r * g


def _dot(a, b):
    return jnp.dot(a, b, preferred_element_type=F32)


def _chunks(n, c):
    return [(s, min(c, n - s)) for s in range(0, n, c)]


def _split3(x):
    hi = x.astype(BF16).astype(F32)
    r = x - hi
    mid = r.astype(BF16).astype(F32)
    lo = (r - mid).astype(BF16).astype(F32)
    return hi, mid, lo


def _pack3(x, n):
    hi, mid, lo = _split3(x)
    return (hi + pltpu.roll(mid, n, 1) + pltpu.roll(lo, 2 * n, 1)).astype(BF16)


def _unpack3(c, n):
    return c + pltpu.roll(c, LANES - n, 1) + pltpu.roll(c, LANES - 2 * n, 1)


def _log_sigmoid(x):
    return jnp.minimum(x, 0.0) - jnp.log1p(jnp.exp(-jnp.abs(x)))


def _const_spec(shape):
    nd = len(shape)
    return pl.BlockSpec(shape, lambda *_: (0,) * nd, pipeline_mode=pl.Buffered(1))


def _params(sem, vmem_limit=None):
    return pltpu.CompilerParams(dimension_semantics=sem, vmem_limit_bytes=vmem_limit or VMEM_LIMIT)


def _ffn_value(x, g, wg_ref, wu_ref, wd_ref, after_chunk=None):
    u = _rms(x, g).astype(BF16)
    acc = None
    chunks = _chunks(wg_ref.shape[1], FF_CHUNK)
    for c, (s, n) in enumerate(chunks):
        gate = _dot(u, wg_ref[:, s:s + n])
        up = _dot(u, wu_ref[:, s:s + n])
        a = (gate * jax.nn.sigmoid(gate) * up).astype(BF16)
        d = _dot(a, wd_ref[s:s + n, :])
        acc = d if acc is None else acc + d
        if after_chunk is not None:
            after_chunk(c, len(chunks))
    return x + 0.5 * acc


def _ffn_kernel(x_ref, g_ref, wg_ref, wu_ref, wd_ref, o_ref):
    o_ref[...] = _ffn_value(x_ref[...], g_ref[...], wg_ref, wu_ref, wd_ref)


def _ffn(x, g, wg, wu, wd, tm):
    n, d = x.shape
    return pl.pallas_call(
        _ffn_kernel,
        grid=(n // tm,),
        in_specs=[pl.BlockSpec((tm, d), lambda i: (i, 0)),
                  _const_spec(g.shape), _const_spec(wg.shape), _const_spec(wu.shape),
                  _const_spec(wd.shape)],
        out_specs=pl.BlockSpec((tm, d), lambda i: (i, 0)),
        out_shape=jax.ShapeDtypeStruct((n, d), F32),
        compiler_params=_params(("parallel",)),
        name="ffn",
    )(x, g, wg, wu, wd)


def _pool_branch(pooled_groups, wpg_ref):
    outs = [_dot(p.astype(BF16), wpg_ref[g]) for g, p in enumerate(pooled_groups)]
    return jnp.concatenate(outs, axis=1)


def _head_lane_mask(shape, n_heads):
    lane = lax.broadcasted_iota(jnp.int32, shape, 1)
    return lane < n_heads


def _proj_prompt_kernel(order_ref, h_ref, g_ref, wpool_ref, wq_ref, wk_ref, wkt_ref, wvt_ref,
                        wf_ref, bf_ref, wgp_ref, wga_ref, wpg_ref, pscale_ref, place_ref,
                        tri_ref, qones_ref, headsel_ref,
                        qa_ref, ka_ref, vt_ref, kt_ref, vto_ref, lf_ref, gpool_ref, sga_ref,
                        zlast_ref, qn_ref, kn_ref, fe_ref, zext_ref, fcarry_ref, *, n_heads):
    i = pl.program_id(1)
    tm = h_ref.shape[0]
    gw = LANES

    @pl.when(i == 0)
    def _():
        zext_ref[0:POOL_HALO, :] = jnp.zeros((POOL_HALO, zext_ref.shape[1]), F32)
        fcarry_ref[...] = jnp.zeros(fcarry_ref.shape, F32)

    u = _rms(h_ref[...], g_ref[...]).astype(BF16)

    z = _dot(u, wpool_ref[...])
    zext_ref[POOL_HALO:POOL_HALO + tm, :] = z
    row = lax.broadcasted_iota(jnp.int32, (tm, gw), 0) + i * tm
    pooled = []
    for g, w in enumerate(POOL_WINDOWS):
        sl = slice(g * gw, (g + 1) * gw)
        zg = z[:, sl]
        acc = zg
        for j in range(1, w):
            acc = acc + zext_ref[POOL_HALO - j:POOL_HALO - j + tm, sl]
        cnt = jnp.minimum(row + 1, w).astype(F32)
        pooled.append(acc / cnt - zg)
    zlast_ref[...] = zext_ref[tm:tm + POOL_HALO, :]
    zext_ref[0:POOL_HALO, :] = zext_ref[tm:tm + POOL_HALO, :]
    branch_pool = _pool_branch(pooled, wpg_ref) * pscale_ref[...]
    gpool_ref[...] = (jax.nn.sigmoid(_dot(u, wgp_ref[...])) * branch_pool).astype(BF16)
    sga_ref[...] = jax.nn.sigmoid(_dot(u, wga_ref[...])).astype(BF16)

    lane_ok = _head_lane_mask((tm, LANES), n_heads)
    logf = jnp.where(lane_ok, _log_sigmoid(_dot(u, wf_ref[...]) + bf_ref[...]), 0.0)
    lf_ref[...] = logf[:, 0:n_heads]
    csum = _unpack3(_dot(tri_ref[...], _pack3(logf, n_heads)), n_heads)
    fcum = jnp.where(lane_ok, csum + fcarry_ref[...], 0.0)
    fcarry_ref[...] = fcum[tm - 1:tm, :]
    fcum2 = fcum * LOG2E
    fe_ref[...] = fcum2[tm - 1:tm, :]

    q = _dot(u, wq_ref[...]).astype(BF16).astype(F32)
    k = _dot(u, wk_ref[...]).astype(BF16).astype(F32)
    for x, n_ref in ((q, qn_ref), (k, kn_ref)):
        sq = (x * x * NORM_SLACK).astype(BF16)
        n_ref[...] = jnp.max(_dot(sq, headsel_ref[...]), axis=0, keepdims=True)
    kt = lax.dot_general(wkt_ref[...], u, _NT, preferred_element_type=F32)
    vt = lax.dot_general(wvt_ref[...], u, _NT, preferred_element_type=F32)
    kt_ref[...] = kt
    vto_ref[...] = vt

    head_dim = kt.shape[0] // n_heads
    per_slab = LANES // head_dim
    fcols = _dot(_pack3(fcum2, n_heads), place_ref[...])
    low = lax.broadcasted_iota(jnp.int32, (tm, LANES), 1) < head_dim
    rid = lax.broadcasted_iota(jnp.int32, (V_ROWS - head_dim, tm), 0)
    ones_rows = jnp.where(rid == 0, 1.0, 0.0).astype(BF16)
    for h in range(n_heads):
        col = (h // per_slab) * LANES
        shift = (h % per_slab) * head_dim
        qs, ks = q[:, col:col + LANES], k[:, col:col + LANES]
        if shift:
            qs, ks = pltpu.roll(qs, LANES - shift, 1), pltpu.roll(ks, LANES - shift, 1)
        qa_ref[h] = jnp.where(low, qs, qones_ref[...]).astype(BF16)
        ka_ref[h] = jnp.where(low, ks, fcols[:, h * HEAD_PAD:(h + 1) * HEAD_PAD]).astype(BF16)
        v_at = pl.multiple_of(order_ref[h] * head_dim, head_dim)
        vt_ref[h, 0:head_dim, :] = vto_ref[pl.ds(v_at, head_dim), :].astype(BF16)
        vt_ref[h, head_dim:V_ROWS, :] = ones_rows


def _proj_prompt(h, w, tm, n_heads, order):
    b, t, d = h.shape
    aw = w["wkt"].shape[0]
    pw = w["wpool"].shape[1]
    nt = t // tm
    names = ["g_mix", "wpool", "wq2", "wk2", "wkt", "wvt", "wf", "bf", "wgp", "wga", "wpg",
             "pscale", "place", "tri", "qones", "headsel"]
    consts = [w[n] for n in names]
    tok = lambda width: pl.BlockSpec((None, tm, width), lambda bi, i, *_: (bi, i, 0))
    tok_t = lambda rows: pl.BlockSpec((None, rows, tm), lambda bi, i, *_: (bi, 0, i))
    head = pl.BlockSpec((None, n_heads, tm, HEAD_PAD), lambda bi, i, *_: (bi, 0, i, 0))
    stat = pl.BlockSpec((None, None, 1, LANES), lambda bi, i, *_: (bi, i, 0, 0))
    stat_shape = jax.ShapeDtypeStruct((b, nt, 1, LANES), F32)
    out_shape = (
        jax.ShapeDtypeStruct((b, n_heads, t, HEAD_PAD), BF16),
        jax.ShapeDtypeStruct((b, n_heads, t, HEAD_PAD), BF16),
        jax.ShapeDtypeStruct((b, n_heads, V_ROWS, t), BF16),
        jax.ShapeDtypeStruct((b, aw, t), F32),
        jax.ShapeDtypeStruct((b, aw, t), F32),
        jax.ShapeDtypeStruct((b, t, n_heads), F32),
        jax.ShapeDtypeStruct((b, t, d), BF16),
        jax.ShapeDtypeStruct((b, t, d), BF16),
        jax.ShapeDtypeStruct((b, POOL_HALO, pw), F32),
        stat_shape, stat_shape, stat_shape,
    )
    out_specs = (
        head, head,
        pl.BlockSpec((None, n_heads, V_ROWS, tm), lambda bi, i, *_: (bi, 0, 0, i)),
        tok_t(aw), tok_t(aw), tok(n_heads), tok(d), tok(d),
        pl.BlockSpec((None, POOL_HALO, pw), lambda bi, i, *_: (bi, 0, 0)),
        stat, stat, stat,
    )
    grid_spec = pltpu.PrefetchScalarGridSpec(
        num_scalar_prefetch=1,
        grid=(b, nt),
        in_specs=[tok(d)] + [_const_spec(c.shape) for c in consts],
        out_specs=out_specs,
        scratch_shapes=[pltpu.VMEM((tm + POOL_HALO, pw), F32), pltpu.VMEM((1, LANES), F32)],
    )
    return pl.pallas_call(
        functools.partial(_proj_prompt_kernel, n_heads=n_heads),
        grid_spec=grid_spec,
        out_shape=out_shape,
        compiler_params=_params(("arbitrary", "arbitrary")),
        name="proj_prompt",
    )(order, h, *consts)


def _proj_sample_kernel(h_ref, state_ref, g_ref, wpool_ref, wq_ref, wk_ref, wv_ref, wf_ref,
                        bf_ref, wgp_ref, wga_ref, wpg_ref, pscale_ref,
                        z_ref, q_ref, k_ref, v_ref, lf_ref, fc_ref, gpool_ref, sga_ref,
                        *, n_heads, n_new, n_seq):
    gw = LANES
    n_pre = state_ref.shape[0] // n_seq
    u = _rms(h_ref[...], g_ref[...]).astype(BF16)

    z = _dot(u, wpool_ref[...])
    z_ref[...] = z

    def ext_rows(r, sl):
        if r < n_pre:
            return state_ref[r * n_seq:(r + 1) * n_seq, sl]
        return z[(r - n_pre) * n_seq:(r - n_pre + 1) * n_seq, sl]

    pooled = []
    for g, w in enumerate(POOL_WINDOWS):
        sl = slice(g * gw, (g + 1) * gw)
        steps = []
        for s in range(n_new):
            r = n_pre + s
            acc = ext_rows(r, sl)
            for j in range(1, w):
                if r - j >= 0:
                    acc = acc + ext_rows(r - j, sl)
            steps.append(acc / float(min(r + 1, w)) - ext_rows(r, sl))
        pooled.append(jnp.concatenate(steps, axis=0))
    branch_pool = _pool_branch(pooled, wpg_ref) * pscale_ref[...]
    gpool_ref[...] = (jax.nn.sigmoid(_dot(u, wgp_ref[...])) * branch_pool).astype(BF16)
    sga_ref[...] = jax.nn.sigmoid(_dot(u, wga_ref[...])).astype(BF16)

    tm = h_ref.shape[0]
    lane_ok = _head_lane_mask((tm, LANES), n_heads)
    logf = jnp.where(lane_ok, _log_sigmoid(_dot(u, wf_ref[...]) + bf_ref[...]), 0.0)
    lf_ref[...] = logf[:, 0:n_heads]
    run = None
    sums = []
    for s in range(n_new):
        blk = logf[s * n_seq:(s + 1) * n_seq, :]
        run = blk if run is None else run + blk
        sums.append(run)
    fc_ref[...] = jnp.concatenate(sums, axis=0)[:, 0:n_heads]

    q_ref[...] = _dot(u, wq_ref[...]).astype(BF16)
    k_ref[...] = _dot(u, wk_ref[...])
    v_ref[...] = _dot(u, wv_ref[...])


def _proj_sample(h, state, w, n_heads, n_new, n_seq):
    n, d = h.shape
    aw = w["wk"].shape[1]
    pw = w["wpool"].shape[1]
    names = ["g_mix", "wpool", "wq", "wk", "wv", "wf", "bf", "wgp", "wga", "wpg", "pscale"]
    consts = [w[n_] for n_ in names]
    full = lambda shape: pl.BlockSpec(shape, lambda i: (0,) * len(shape))
    out_shape = (
        jax.ShapeDtypeStruct((n, pw), F32),
        jax.ShapeDtypeStruct((n, aw), BF16),
        jax.ShapeDtypeStruct((n, aw), F32),
        jax.ShapeDtypeStruct((n, aw), F32),
        jax.ShapeDtypeStruct((n, n_heads), F32),
        jax.ShapeDtypeStruct((n, n_heads), F32),
        jax.ShapeDtypeStruct((n, d), BF16),
        jax.ShapeDtypeStruct((n, d), BF16),
    )
    return pl.pallas_call(
        functools.partial(_proj_sample_kernel, n_heads=n_heads, n_new=n_new, n_seq=n_seq),
        grid=(1,),
        in_specs=[full(h.shape), full(state.shape)] + [_const_spec(c.shape) for c in consts],
        out_specs=tuple(full(s.shape) for s in out_shape),
        out_shape=out_shape,
        compiler_params=_params(("arbitrary",)),
        name="proj_sample",
    )(h, state, *consts)


def _attn_kernel(q_ref, k_ref, vt_ref, qn_ref, kn_ref, fe_ref, o_ref, *, head_dim):
    i = pl.program_id(2)
    nh, tq, _ = q_ref.shape
    tk = tq
    nt = fe_ref.shape[1]
    qs = [q_ref[h] for h in range(nh)]

    def scores(h, j):
        start = pl.multiple_of(j * tk, tk)
        return lax.dot_general(k_ref[h, pl.ds(start, tk), :], qs[h], _NT, preferred_element_type=F32)

    def absorb(h, j, s, m, acc):
        start = pl.multiple_of(j * tk, tk)
        m_new = jnp.maximum(m, jnp.max(s, axis=0, keepdims=True))
        alpha = jnp.exp2(m - m_new)
        p = jnp.exp2(s - m_new).astype(BF16)
        acc = alpha * acc + _dot(vt_ref[h, :, pl.ds(start, tk)], p)
        return m_new, acc

    kpos = lax.broadcasted_iota(jnp.int32, (tk, tq), 0)
    qpos = lax.broadcasted_iota(jnp.int32, (tk, tq), 1)
    diag = [jnp.where(kpos <= qpos, scores(h, i), MASKED) for h in range(nh)]
    state = []
    for h in range(nh):
        state += list(absorb(h, i, diag[h], jnp.full((1, tq), MASKED, F32),
                             jnp.zeros((V_ROWS, tq), F32)))

    lane = lax.broadcasted_iota(jnp.int32, (1, nt), 1)
    n_back = None
    for h in range(nh):
        m_min = jnp.min(state[2 * h], axis=1, keepdims=True)
        q2 = jnp.max(jnp.where(lane == i, qn_ref[h:h + 1, :], 0.0), axis=1, keepdims=True)
        k2 = jnp.max(kn_ref[h:h + 1, :], axis=1, keepdims=True)
        bound = jnp.sqrt(q2 * k2) - fe_ref[h:h + 1, :]
        need = (lane < i) & (bound > m_min - SKIP_MARGIN)
        n_h = jnp.sum(need.astype(jnp.int32))
        n_back = n_h if n_back is None else jnp.maximum(n_back, n_h)

    def make_body(group):
        def body(jj, c):
            c = list(c)
            first = c[-1]
            tiles = [first - jj * group - g for g in range(group)]
            ss = [[scores(h, j) for h in range(nh)] for j in tiles]
            for g, j in enumerate(tiles):
                for h in range(nh):
                    c[2 * h], c[2 * h + 1] = absorb(h, j, ss[g][h], c[2 * h], c[2 * h + 1])
            return tuple(c)
        return body

    left, nearest, state = n_back, i - 1, tuple(state)
    for group in ATTN_GROUPS:
        bigger = [x for x in ATTN_GROUPS if x > group]
        trips = (left % min(bigger)) // group if bigger else left // group
        state = lax.fori_loop(0, trips, make_body(group), state + (nearest,))[:-1]
        left, nearest = left - trips * group, nearest - trips * group
    for h in range(nh):
        acc = state[2 * h + 1]
        o_ref[h * head_dim:(h + 1) * head_dim, :] = (
            acc[0:head_dim, :] / acc[head_dim:head_dim + 1, :]).astype(BF16)


def _attention(qa, ka, vt, qn, kn, fe, head_dim, tq):
    b, nh, t, _ = qa.shape
    g = ATTN_HEADS
    nt = t // tq
    stat = pl.BlockSpec((None, None, g, nt), lambda bi, h, i: (bi, h, 0, 0))
    return pl.pallas_call(
        functools.partial(_attn_kernel, head_dim=head_dim),
        grid=(b, nh // g, nt),
        in_specs=[pl.BlockSpec((None, g, tq, HEAD_PAD), lambda bi, h, i: (bi, h, i, 0)),
                  pl.BlockSpec((None, g, t, HEAD_PAD), lambda bi, h, i: (bi, h, 0, 0)),
                  pl.BlockSpec((None, g, V_ROWS, t), lambda bi, h, i: (bi, h, 0, 0)),
                  stat, stat, stat],
        out_specs=pl.BlockSpec((None, g * head_dim, tq), lambda bi, h, i: (bi, h, i)),
        out_shape=jax.ShapeDtypeStruct((b, nh * head_dim, t), BF16),
        compiler_params=_params(("parallel", "parallel", "arbitrary")),
        name="attn_prompt",
    )(qa, ka, vt, qn, kn, fe)


def _page_copies(ck_hbm, cv_hbm, cl_hbm, kbuf, vbuf, lbuf, sems, slot, page_ids):
    page = cl_hbm.shape[2]
    out = []
    for i, pid in enumerate(page_ids):
        lanes = slice(i * page, (i + 1) * page)
        out += [pltpu.make_async_copy(ck_hbm.at[pid], kbuf.at[slot, :, lanes], sems.at[slot, 0]),
                pltpu.make_async_copy(cv_hbm.at[pid], vbuf.at[slot, :, lanes], sems.at[slot, 1]),
                pltpu.make_async_copy(cl_hbm.at[pid], lbuf.at[slot, i], sems.at[slot, 2])]
    return out


def _decode_prologue(refs):
    pt_ref, ck_hbm, cv_hbm, cl_hbm, kbuf, vbuf, lbuf, sems = refs[0], *refs[6:9], *refs[10:14]
    slots = kbuf.shape[0]
    pp = kbuf.shape[2] // cl_hbm.shape[2]
    for s in range(slots):
        for c in _page_copies(ck_hbm, cv_hbm, cl_hbm, kbuf, vbuf, lbuf, sems, s,
                              [pt_ref[s * pp + i] for i in range(pp)]):
            c.start()


def _decode_step(d, n_dsteps, seq, part, n_parts, refs, n_heads, n_new):
    (pt_ref, q_ref, kn_ref, vn_ref, negc_ref, upper_ref, ck_hbm, cv_hbm, cl_hbm, o_ref,
     kbuf, vbuf, lbuf, sems, m_ref, l_ref, acc_ref, fcar_ref) = refs
    slots, width, group_keys = kbuf.shape
    page = cl_hbm.shape[2]
    pp = group_keys // page
    head_dim = width // n_heads
    rows = n_new * n_heads

    def copies(slot, page_ids):
        return _page_copies(ck_hbm, cv_hbm, cl_hbm, kbuf, vbuf, lbuf, sems, slot, page_ids)

    def start(slot, first):
        for c in copies(slot, [pt_ref[first + i] for i in range(pp)]):
            c.start()

    def wait(slot):
        for c in copies(slot, [0] * pp):
            c.wait()

    if part == 0:
        m_ref[...] = jnp.full(m_ref.shape, MASKED, F32)
        l_ref[...] = jnp.zeros(l_ref.shape, F32)
        acc_ref[...] = jnp.zeros(acc_ref.shape, F32)
        fcar_ref[...] = jnp.zeros(fcar_ref.shape, F32)

    q = q_ref[seq].astype(F32)
    qrows = jnp.concatenate([jnp.broadcast_to(q[s:s + 1, :], (n_heads, width)) for s in range(n_new)],
                            axis=0)
    rid = lax.broadcasted_iota(jnp.int32, (rows, width), 0)
    cid = lax.broadcasted_iota(jnp.int32, (rows, width), 1)
    own = (cid // head_dim) == (rid % n_heads)
    qbd = jnp.where(own, qrows, 0.0)

    def update(s, vals_t):
        m_new = jnp.maximum(m_ref[...], jnp.max(s, axis=1, keepdims=True))
        alpha = jnp.exp(m_ref[...] - m_new)
        p = jnp.exp(s - m_new)
        l_ref[...] = alpha * l_ref[...] + jnp.sum(p, axis=1, keepdims=True)
        pv, at = None, 0
        for v in vals_t:
            part = lax.dot_general(p[:, at:at + v.shape[1]], v, _NT, preferred_element_type=F32)
            pv = part if pv is None else pv + part
            at += v.shape[1]
        acc_ref[...] = alpha * acc_ref[...] + pv
        m_ref[...] = m_new

    def consume(group):
        x = jnp.concatenate([lbuf[slot, i] for slot in group for i in range(pp)], axis=0)
        parts = jnp.concatenate(_split3(x), axis=0).astype(BF16)
        c = _dot(parts, upper_ref[...])
        nr = len(group) * pp * n_heads
        fin = c[0:nr] + c[nr:2 * nr] + c[2 * nr:3 * nr]
        carry = fcar_ref[...]
        biases = []
        for i in range(len(group) * pp):
            fi = fin[i * n_heads:(i + 1) * n_heads, :]
            biases.append(jnp.concatenate([-(fi + carry)] * n_new, axis=0))
            carry = carry + jnp.broadcast_to(fi[:, page - 1:page], carry.shape)
        fcar_ref[...] = carry
        scores = jnp.concatenate([_dot(qbd, kbuf[slot]) for slot in group], axis=1)
        update(scores + jnp.concatenate(biases, axis=1), [vbuf[slot] for slot in group])

    first = d * (slots * pp)
    for s0 in range(0, slots, DECODE_MERGE):
        group = list(range(s0, s0 + DECODE_MERGE))
        for s in group:
            wait(s)
        consume(group)

        @pl.when(d + 1 < n_dsteps)
        def _():
            for s in group:
                start(s, first + (slots + s) * pp)

    if part == n_parts - 1:
        carry = fcar_ref[...]
        sn = _dot(qbd, kn_ref[seq].astype(F32))
        bias = jnp.concatenate([negc_ref[seq] - carry] * n_new, axis=0)
        r2 = lax.broadcasted_iota(jnp.int32, sn.shape, 0)
        c2 = lax.broadcasted_iota(jnp.int32, sn.shape, 1)
        ok = (c2 < n_new) & (c2 <= r2 // n_heads)
        update(jnp.where(ok, sn + bias, MASKED), [vn_ref[seq].astype(F32)])
        out = jnp.where(own, acc_ref[...] / l_ref[...], 0.0)
        o_ref[seq] = jnp.sum(out.reshape(n_new, n_heads, width), axis=1)


N_DECODE_IN = 8


def _decode_hook(pt_ref, dec_in, o_ref, scratch, tile, n_tiles, n_parts, n_heads, n_new):
    refs = (pt_ref,) + tuple(dec_in) + (o_ref,) + tuple(scratch)
    per_tile = o_ref.shape[0] * n_parts
    n_dsteps = n_tiles * per_tile

    @pl.when(tile == 0)
    def _():
        _decode_prologue(refs)

    done = [0]

    def after_chunk(c, n_chunks):
        upto = ((c + 1) * per_tile) // n_chunks
        for k in range(done[0], upto):
            _decode_step(tile * per_tile + k, n_dsteps, k // n_parts, k % n_parts, n_parts, refs,
                         n_heads, n_new)
        done[0] = upto

    return after_chunk


def _decode_operands(page_table, q, kn_t, vn_t, negc, upper, cache_kt, cache_vt, cache_lft, n_tiles,
                     flat_tile, n_heads):
    bd, n_new, width = q.shape
    n_pages = page_table.shape[1]
    page = cache_kt.shape[2]
    slots = DECODE_SLOTS
    pp = min(DECODE_PAGES, n_pages // slots)
    n_parts = n_pages // (slots * pp)
    assert n_pages == n_parts * slots * pp and (bd * n_parts) % n_tiles == 0
    per_tile = bd * n_parts // n_tiles
    assert per_tile % n_parts == 0
    seqs = per_tile // n_parts
    rows = n_new * n_heads
    seq = lambda shape: pl.BlockSpec((seqs,) + shape, lambda *idx: (flat_tile(*idx[:-1]), 0, 0))
    hbm = pl.BlockSpec(memory_space=pl.ANY)
    return {
        "n_parts": n_parts,
        "page_table": page_table.reshape(-1),
        "inputs": [q, kn_t, vn_t, negc, upper, cache_kt, cache_vt, cache_lft],
        "in_specs": [seq((n_new, width)), seq((width, page)), seq((width, page)),
                     seq((n_heads, LANES)), _const_spec(upper.shape), hbm, hbm, hbm],
        "out_spec": seq((n_new, width)),
        "out_shape": jax.ShapeDtypeStruct((bd, n_new, width), F32),
        "scratch": [pltpu.VMEM((slots, width, pp * page), F32),
                    pltpu.VMEM((slots, width, pp * page), F32),
                    pltpu.VMEM((slots, pp, n_heads, page), F32),
                    pltpu.SemaphoreType.DMA((slots, 3)),
                    pltpu.VMEM((rows, 1), F32), pltpu.VMEM((rows, 1), F32),
                    pltpu.VMEM((rows, width), F32), pltpu.VMEM((n_heads, LANES), F32)],
    }


def _ffn_decode_kernel(pt_ref, x_ref, g_ref, wg_ref, wu_ref, wd_ref, *refs, n_heads, n_new, n_parts):
    dec_in, (h_ref, o_ref), scratch = refs[:N_DECODE_IN], refs[N_DECODE_IN:N_DECODE_IN + 2], \
        refs[N_DECODE_IN + 2:]
    hook = _decode_hook(pt_ref, dec_in, o_ref, scratch, pl.program_id(0), pl.num_programs(0), n_parts,
                        n_heads, n_new)
    h_ref[...] = _ffn_value(x_ref[...], g_ref[...], wg_ref, wu_ref, wd_ref, hook)


def _ffn_decode(x, g, wg, wu, wd, tm, n_heads, *decode_args):
    n, d = x.shape
    dec = _decode_operands(*decode_args, n // tm, lambda i: i, n_heads)
    tok = pl.BlockSpec((tm, d), lambda i, pt_ref: (i, 0))
    grid_spec = pltpu.PrefetchScalarGridSpec(
        num_scalar_prefetch=1,
        grid=(n // tm,),
        in_specs=[tok, _const_spec(g.shape), _const_spec(wg.shape), _const_spec(wu.shape),
                  _const_spec(wd.shape)] + dec["in_specs"],
        out_specs=(tok, dec["out_spec"]),
        scratch_shapes=dec["scratch"],
    )
    n_new = dec["out_shape"].shape[1]
    return pl.pallas_call(
        functools.partial(_ffn_decode_kernel, n_heads=n_heads, n_new=n_new, n_parts=dec["n_parts"]),
        grid_spec=grid_spec,
        out_shape=(jax.ShapeDtypeStruct((n, d), F32), dec["out_shape"]),
        compiler_params=_params(("arbitrary",)),
        name="ffn_decode",
    )(dec["page_table"], x, g, wg, wu, wd, *dec["inputs"])


N_POST_IN = 15


def _post_value(h_ref, gpool_ref, sga_ref, attn_ref, p_ref, wab_ref, wout_ref, g2_ref, wg_ref,
                wu_ref, wd_ref, gple_ref, wpg_ref, wple_ref, gfin_ref, attn_transposed, after_chunk):
    if attn_transposed:
        branch_attn = lax.dot_general(attn_ref[...], wab_ref[...], _TN, preferred_element_type=F32)
    else:
        branch_attn = _dot(attn_ref[...].astype(BF16), wab_ref[...])
    merged = gpool_ref[...].astype(F32) + sga_ref[...].astype(F32) * branch_attn
    h = h_ref[...] + _dot(merged.astype(BF16), wout_ref[...])
    h = _ffn_value(h, g2_ref[...], wg_ref, wu_ref, wd_ref, after_chunk)
    gate = jax.nn.sigmoid(_dot(_rms(h, gple_ref[...]).astype(BF16), wpg_ref[...]))
    h = h + _dot(p_ref[...].astype(BF16), wple_ref[...]) * gate
    return _rms(h, gfin_ref[...])


def _post_kernel(*refs, attn_transposed):
    y_ref = refs[N_POST_IN]
    y_ref[...] = _post_value(*refs[:N_POST_IN], attn_transposed, None)


def _post_decode_kernel(pt_ref, *refs, attn_transposed, n_heads, n_new, n_parts):
    ins, refs = refs[:N_POST_IN], refs[N_POST_IN:]
    dec_in, (y_ref, o_ref), scratch = refs[:N_DECODE_IN], refs[N_DECODE_IN:N_DECODE_IN + 2], \
        refs[N_DECODE_IN + 2:]
    tile = pl.program_id(0) * pl.num_programs(1) + pl.program_id(1)
    hook = _decode_hook(pt_ref, dec_in, o_ref, scratch, tile, pl.num_programs(0) * pl.num_programs(1),
                        n_parts, n_heads, n_new)
    y_ref[...] = _post_value(*ins, attn_transposed, hook)


def _post(h, gpool, sga, attn, p, w, tm, attn_transposed, n_heads=None, decode_args=None):
    b, t, d = h.shape
    nt = t // tm
    names = ["wab2" if attn_transposed else "wab", "wout", "g_ffn2", "wg2", "wu2", "wd2", "g_ple",
             "wpgate", "wple", "g_final"]
    consts = [w[n] for n in names]
    tok = lambda width: pl.BlockSpec((None, tm, width), lambda bi, i, *_: (bi, i, 0))
    if attn_transposed:
        attn_spec = pl.BlockSpec((None, attn.shape[1], tm), lambda bi, i, *_: (bi, 0, i))
    else:
        attn_spec = tok(attn.shape[2])
    in_specs = [tok(d), tok(d), tok(d), attn_spec, tok(p.shape[2])] + [_const_spec(c.shape) for c in consts]
    y_shape = jax.ShapeDtypeStruct((b, t, d), F32)
    if decode_args is None:
        return pl.pallas_call(
            functools.partial(_post_kernel, attn_transposed=attn_transposed),
            grid=(b, nt),
            in_specs=in_specs,
            out_specs=tok(d),
            out_shape=y_shape,
            compiler_params=_params(("parallel", "parallel")),
            name="post",
        )(h, gpool, sga, attn, p, *consts)
    dec = _decode_operands(*decode_args, b * nt, lambda bi, i: bi * nt + i, n_heads)
    grid_spec = pltpu.PrefetchScalarGridSpec(
        num_scalar_prefetch=1,
        grid=(b, nt),
        in_specs=in_specs + dec["in_specs"],
        out_specs=(tok(d), dec["out_spec"]),
        scratch_shapes=dec["scratch"],
    )
    return pl.pallas_call(
        functools.partial(_post_decode_kernel, attn_transposed=attn_transposed, n_heads=n_heads,
                          n_new=dec["out_shape"].shape[1], n_parts=dec["n_parts"]),
        grid_spec=grid_spec,
        out_shape=(y_shape, dec["out_shape"]),
        compiler_params=_params(("arbitrary", "arbitrary"), POST_DECODE_VMEM_LIMIT),
        name="post_decode",
    )(dec["page_table"], h, gpool, sga, attn, p, *consts, *dec["inputs"])


def _prep_weights(g_ffn1, w_ffn1_gate, w_ffn1_up, w_ffn1_down, g_mix, w_in, b_forget, w_pool_group,
                  pool_scale, w_attn_branch, w_out, g_ffn2, w_ffn2_gate, w_ffn2_up, w_ffn2_down,
                  g_ple, w_ple_gate, w_ple, g_final, n_heads, head_dim, pool_width, tm, order):
    d = w_in.shape[0]
    aw = n_heads * head_dim
    o = 0
    wpool = w_in[:, o:o + pool_width]; o += pool_width
    wq = w_in[:, o:o + aw] * (head_dim ** -0.5); o += aw
    wk = w_in[:, o:o + aw]; o += aw
    wv = w_in[:, o:o + aw]; o += aw
    wf = w_in[:, o:o + n_heads]; o += n_heads
    wgp = w_in[:, o:o + d]; o += d
    wga = w_in[:, o:o + d]

    qones = np.zeros((1, HEAD_PAD), np.float32)
    qones[0, head_dim:head_dim + F_PARTS] = 1.0
    headsel = np.zeros((aw, LANES), np.float32)
    for h in range(n_heads):
        headsel[h * head_dim:(h + 1) * head_dim, h] = 1.0
    parts, slots = np.arange(F_PARTS)[:, None], np.arange(n_heads)[None, :]
    place = jnp.zeros((LANES, n_heads * HEAD_PAD), F32).at[
        parts * n_heads + order[None, :], slots * HEAD_PAD + head_dim + parts].set(-1.0)
    by_slot_cols = lambda x: x.reshape(d, n_heads, head_dim)[:, order].reshape(d, aw)
    by_slot_rows = lambda x: x.reshape(n_heads, head_dim, -1)[order].reshape(aw, -1)
    row = lambda x: x.reshape(1, -1).astype(F32)
    bf = lambda x: x.astype(BF16)
    return {
        "g_ffn1": row(g_ffn1), "wg1": bf(w_ffn1_gate), "wu1": bf(w_ffn1_up), "wd1": bf(w_ffn1_down),
        "g_mix": row(g_mix), "wpool": bf(wpool), "wq": bf(wq), "wk": bf(wk), "wv": bf(wv),
        "wq2": bf(by_slot_cols(wq * LOG2E)), "wk2": bf(by_slot_cols(wk)),
        "wab2": bf(by_slot_rows(w_attn_branch)),
        "wkt": bf(wk.T), "wvt": bf(wv.T),
        "wf": bf(jnp.pad(wf, ((0, 0), (0, LANES - n_heads)))),
        "bf": jnp.pad(row(b_forget), ((0, 0), (0, LANES - n_heads))),
        "wgp": bf(wgp), "wga": bf(wga), "wpg": bf(w_pool_group), "pscale": row(pool_scale),
        "place": bf(place), "qones": jnp.asarray(qones, F32),
        "headsel": jnp.asarray(headsel, BF16),
        "tri": jnp.asarray(np.tril(np.ones((tm, tm), np.float32)), BF16),
        "wab": bf(w_attn_branch), "wout": bf(w_out),
        "g_ffn2": row(g_ffn2), "wg2": bf(w_ffn2_gate), "wu2": bf(w_ffn2_up), "wd2": bf(w_ffn2_down),
        "g_ple": row(g_ple), "wpgate": bf(w_ple_gate), "wple": bf(w_ple), "g_final": row(g_final),
    }


def kernel(x_prompt, x_sample, cache_k, cache_v, cache_logf, state_pool, page_table, p_prompt, p_sample, g_ffn1, w_ffn1_gate, w_ffn1_up, w_ffn1_down, g_mix, w_in, b_forget, w_pool_group, pool_scale, w_attn_branch, w_out, g_ffn2, w_ffn2_gate, w_ffn2_up, w_ffn2_down, g_ple, w_ple_gate, w_ple, g_final):
    depth = cache_k.shape[0]
    assert depth == 1, "one trunk layer"
    b, t, d = x_prompt.shape
    bd, n_new, _ = x_sample.shape
    _, n_pool_pages, page, n_heads, head_dim = cache_k.shape
    aw = n_heads * head_dim
    n_pre, pool_width = state_pool.shape[2], state_pool.shape[3]
    assert head_dim + F_PARTS <= HEAD_PAD and head_dim < V_ROWS and page == LANES
    assert pool_width == len(POOL_WINDOWS) * LANES and n_pre == POOL_WINDOWS[-1] - 1
    assert n_heads % ATTN_HEADS == 0
    tm = min(TOKEN_TILE, t)
    assert t // tm <= LANES
    ns = bd * n_new

    order = jnp.argsort(b_forget[0]).astype(jnp.int32)
    w = _prep_weights(g_ffn1[0], w_ffn1_gate[0], w_ffn1_up[0], w_ffn1_down[0], g_mix[0], w_in[0],
                      b_forget[0], w_pool_group[0], pool_scale[0], w_attn_branch[0], w_out[0],
                      g_ffn2[0], w_ffn2_gate[0], w_ffn2_up[0], w_ffn2_down[0], g_ple[0],
                      w_ple_gate[0], w_ple[0], g_final, n_heads, head_dim, pool_width, tm, order)

    step_major = lambda x: jnp.swapaxes(x, 0, 1).reshape(ns, x.shape[-1])
    seq_major = lambda x: jnp.swapaxes(x.reshape(n_new, bd, x.shape[-1]), 0, 1)
    hs1 = _ffn(step_major(x_sample), w["g_ffn1"], w["wg1"], w["wu1"], w["wd1"], ns)
    state = jnp.swapaxes(state_pool[0], 0, 1).reshape(n_pre * bd, pool_width)
    z_s, q_s, k_s, v_s, lf_s, fc_s, gpool_s, sga_s = _proj_sample(hs1, state, w, n_heads, n_new, bd)
    k_s, v_s, lf_s = seq_major(k_s), seq_major(v_s), seq_major(lf_s)
    new_page = lambda x: jnp.pad(jnp.swapaxes(x, 1, 2).astype(BF16), ((0, 0), (0, 0), (0, page - n_new)))
    negc = -jnp.swapaxes(seq_major(fc_s), 1, 2)
    negc = jnp.pad(negc, ((0, 0), (0, 0), (0, LANES - n_new)))
    upper = jnp.asarray(np.triu(np.ones((page, page), np.float32)), BF16)
    pages_t = lambda c: jnp.transpose(c[0], (0, 2, 3, 1)).reshape(n_pool_pages, aw, page)

    dec_args = (page_table, seq_major(q_s), new_page(k_s), new_page(v_s), negc)
    caches = (upper, pages_t(cache_k), pages_t(cache_v), jnp.swapaxes(cache_logf[0], 1, 2))
    half = bd // 2
    h1, attn_a = _ffn_decode(x_prompt.reshape(b * t, d), w["g_ffn1"], w["wg1"], w["wu1"], w["wd1"], tm,
                             n_heads, *(a[:half] for a in dec_args), *caches)
    h1 = h1.reshape(b, t, d)
    qa, ka, vt, kt_p, vt_p, lf_p, gpool, sga, zlast, qn, kn, fe = _proj_prompt(h1, w, tm, n_heads,
                                                                               order)
    stat = lambda x: jnp.swapaxes(x[:, :, 0, :n_heads], 1, 2).reshape(
        b, n_heads // ATTN_HEADS, ATTN_HEADS, t // tm)
    attn_t = _attention(qa, ka, vt, stat(qn), stat(kn), stat(fe[..., order]), head_dim, tm)
    y_prompt, attn_b = _post(h1, gpool, sga, attn_t, p_prompt[0], w, tm, True, n_heads,
                             tuple(a[half:] for a in dec_args) + caches)
    heads_last = lambda x: jnp.transpose(x.reshape(b, n_heads, head_dim, t), (0, 3, 1, 2))[None]

    attn_s = jnp.concatenate([attn_a, attn_b], axis=0)
    y_s = _post(hs1[None], gpool_s[None], sga_s[None], step_major(attn_s)[None],
                step_major(p_sample[0])[None], w, ns, False)
    y_sample = seq_major(y_s[0])
    pool_sample = jnp.concatenate([state_pool[0], seq_major(z_s)], axis=1)[:, -n_pre:]

    return (y_prompt, y_sample, heads_last(kt_p), heads_last(vt_p),
            lf_p[None], zlast[None, :, POOL_HALO - n_pre:],
            k_s.reshape(1, bd, n_new, n_heads, head_dim), v_s.reshape(1, bd, n_new, n_heads, head_dim),
            lf_s[None], pool_sample[None])
```

```python
import functools

import numpy as np
import jax
import jax.numpy as jnp
from jax import lax
from jax.experimental import pallas as pl
from jax.experimental.pallas import tpu as pltpu

F32 = jnp.float32
BF16 = jnp.bfloat16

RMS_EPS = 1e-6
MASKED = -1e30
POOL_WINDOWS = (2, 4, 8, 16)
POOL_HALO = 16

LANES = 128
HEAD_PAD = 128
V_ROWS = 80
F_PARTS = 3
FF_CHUNK = 768
TOKEN_TILE = 512
DECODE_PAGES = 8
DECODE_SLOTS = 4
DECODE_MERGE_FFN = 2
DECODE_MERGE_POST = 4
ATTN_HEADS = 2
ATTN_GROUPS = (1, 2)
VMEM_LIMIT = 56 * 1024 * 1024
POST_DECODE_VMEM_LIMIT = 60 * 1024 * 1024
LOG2E = 1.4426950408889634
NORM_SLACK = 1.01
SKIP_MARGIN = 140.0

_NT = (((1,), (1,)), ((), ()))
_TN = (((0,), (0,)), ((), ()))


def _rms(x, g):
    r = lax.rsqrt(jnp.mean(x * x, axis=-1, keepdims=True) + RMS_EPS)
    return x * r * g


def _dot(a, b):
    return jnp.dot(a, b, preferred_element_type=F32)


def _chunks(n, c):
    return [(s, min(c, n - s)) for s in range(0, n, c)]


def _split3(x):
    hi = x.astype(BF16).astype(F32)
    r = x - hi
    mid = r.astype(BF16).astype(F32)
    lo = (r - mid).astype(BF16).astype(F32)
    return hi, mid, lo


def _pack3(x, n):
    hi, mid, lo = _split3(x)
    return (hi + pltpu.roll(mid, n, 1) + pltpu.roll(lo, 2 * n, 1)).astype(BF16)


def _unpack3(c, n):
    return c + pltpu.roll(c, LANES - n, 1) + pltpu.roll(c, LANES - 2 * n, 1)


def _log_sigmoid(x):
    return jnp.minimum(x, 0.0) - jnp.log1p(jnp.exp(-jnp.abs(x)))


def _const_spec(shape):
    nd = len(shape)
    return pl.BlockSpec(shape, lambda *_: (0,) * nd, pipeline_mode=pl.Buffered(1))


def _params(sem, vmem_limit=None):
    return pltpu.CompilerParams(dimension_semantics=sem, vmem_limit_bytes=vmem_limit or VMEM_LIMIT)


def _ffn_value(x, g, wg_ref, wu_ref, wd_ref, after_chunk=None):
    u = _rms(x, g).astype(BF16)
    acc = None
    chunks = _chunks(wg_ref.shape[1], FF_CHUNK)
    for c, (s, n) in enumerate(chunks):
        gate = _dot(u, wg_ref[:, s:s + n])
        up = _dot(u, wu_ref[:, s:s + n])
        a = (gate * jax.nn.sigmoid(gate) * up).astype(BF16)
        d = _dot(a, wd_ref[s:s + n, :])
        acc = d if acc is None else acc + d
        if after_chunk is not None:
            after_chunk(c, len(chunks))
    return x + 0.5 * acc


def _ffn_kernel(x_ref, g_ref, wg_ref, wu_ref, wd_ref, o_ref):
    o_ref[...] = _ffn_value(x_ref[...], g_ref[...], wg_ref, wu_ref, wd_ref)


def _ffn(x, g, wg, wu, wd, tm):
    n, d = x.shape
    return pl.pallas_call(
        _ffn_kernel,
        grid=(n // tm,),
        in_specs=[pl.BlockSpec((tm, d), lambda i: (i, 0)),
                  _const_spec(g.shape), _const_spec(wg.shape), _const_spec(wu.shape),
                  _const_spec(wd.shape)],
        out_specs=pl.BlockSpec((tm, d), lambda i: (i, 0)),
        out_shape=jax.ShapeDtypeStruct((n, d), F32),
        compiler_params=_params(("parallel",)),
        name="ffn",
    )(x, g, wg, wu, wd)


def _pool_branch(pooled_groups, wpg_ref):
    outs = [_dot(p.astype(BF16), wpg_ref[g]) for g, p in enumerate(pooled_groups)]
    return jnp.concatenate(outs, axis=1)


def _head_lane_mask(shape, n_heads):
    lane = lax.broadcasted_iota(jnp.int32, shape, 1)
    return lane < n_heads


def _proj_prompt_kernel(order_ref, h_ref, g_ref, wpool_ref, wq_ref, wk_ref, wkt_ref, wvt_ref,
                        wf_ref, bf_ref, wgp_ref, wga_ref, wpg_ref, pscale_ref, place_ref,
                        tri_ref, qones_ref, headsel_ref,
                        qa_ref, ka_ref, vt_ref, kt_ref, vto_ref, lf_ref, gpool_ref, sga_ref,
                        zlast_ref, qn_ref, kn_ref, fe_ref, zext_ref, fcarry_ref, *, n_heads):
    i = pl.program_id(1)
    tm = h_ref.shape[0]
    gw = LANES

    @pl.when(i == 0)
    def _():
        zext_ref[0:POOL_HALO, :] = jnp.zeros((POOL_HALO, zext_ref.shape[1]), F32)
        fcarry_ref[...] = jnp.zeros(fcarry_ref.shape, F32)

    u = _rms(h_ref[...], g_ref[...]).astype(BF16)

    z = _dot(u, wpool_ref[...])
    zext_ref[POOL_HALO:POOL_HALO + tm, :] = z
    row = lax.broadcasted_iota(jnp.int32, (tm, gw), 0) + i * tm
    pooled = []
    for g, w in enumerate(POOL_WINDOWS):
        sl = slice(g * gw, (g + 1) * gw)
        zg = z[:, sl]
        acc = zg
        for j in range(1, w):
            acc = acc + zext_ref[POOL_HALO - j:POOL_HALO - j + tm, sl]
        cnt = jnp.minimum(row + 1, w).astype(F32)
        pooled.append(acc / cnt - zg)
    zlast_ref[...] = zext_ref[tm:tm + POOL_HALO, :]
    zext_ref[0:POOL_HALO, :] = zext_ref[tm:tm + POOL_HALO, :]
    branch_pool = _pool_branch(pooled, wpg_ref) * pscale_ref[...]
    gpool_ref[...] = (jax.nn.sigmoid(_dot(u, wgp_ref[...])) * branch_pool).astype(BF16)
    sga_ref[...] = jax.nn.sigmoid(_dot(u, wga_ref[...])).astype(BF16)

    lane_ok = _head_lane_mask((tm, LANES), n_heads)
    logf = jnp.where(lane_ok, _log_sigmoid(_dot(u, wf_ref[...]) + bf_ref[...]), 0.0)
    lf_ref[...] = logf[:, 0:n_heads]
    csum = _unpack3(_dot(tri_ref[...], _pack3(logf, n_heads)), n_heads)
    fcum = jnp.where(lane_ok, csum + fcarry_ref[...], 0.0)
    fcarry_ref[...] = fcum[tm - 1:tm, :]
    fcum2 = fcum * LOG2E
    fe_ref[...] = fcum2[tm - 1:tm, :]

    q = _dot(u, wq_ref[...]).astype(BF16).astype(F32)
    k = _dot(u, wk_ref[...]).astype(BF16).astype(F32)
    for x, n_ref in ((q, qn_ref), (k, kn_ref)):
        sq = (x * x * NORM_SLACK).astype(BF16)
        n_ref[...] = jnp.max(_dot(sq, headsel_ref[...]), axis=0, keepdims=True)
    kt = lax.dot_general(wkt_ref[...], u, _NT, preferred_element_type=F32)
    vt = lax.dot_general(wvt_ref[...], u, _NT, preferred_element_type=F32)
    kt_ref[...] = kt
    vto_ref[...] = vt

    head_dim = kt.shape[0] // n_heads
    per_slab = LANES // head_dim
    fcols = _dot(_pack3(fcum2, n_heads), place_ref[...])
    low = lax.broadcasted_iota(jnp.int32, (tm, LANES), 1) < head_dim
    rid = lax.broadcasted_iota(jnp.int32, (V_ROWS - head_dim, tm), 0)
    ones_rows = jnp.where(rid == 0, 1.0, 0.0).astype(BF16)
    for h in range(n_heads):
        col = (h // per_slab) * LANES
        shift = (h % per_slab) * head_dim
        qs, ks = q[:, col:col + LANES], k[:, col:col + LANES]
        if shift:
            qs, ks = pltpu.roll(qs, LANES - shift, 1), pltpu.roll(ks, LANES - shift, 1)
        qa_ref[h] = jnp.where(low, qs, qones_ref[...]).astype(BF16)
        ka_ref[h] = jnp.where(low, ks, fcols[:, h * HEAD_PAD:(h + 1) * HEAD_PAD]).astype(BF16)
        v_at = pl.multiple_of(order_ref[h] * head_dim, head_dim)
        vt_ref[h, 0:head_dim, :] = vto_ref[pl.ds(v_at, head_dim), :].astype(BF16)
        vt_ref[h, head_dim:V_ROWS, :] = ones_rows


def _proj_prompt(h, w, tm, n_heads, order):
    b, t, d = h.shape
    aw = w["wkt"].shape[0]
    pw = w["wpool"].shape[1]
    nt = t // tm
    names = ["g_mix", "wpool", "wq2", "wk2", "wkt", "wvt", "wf", "bf", "wgp", "wga", "wpg",
             "pscale", "place", "tri", "qones", "headsel"]
    consts = [w[n] for n in names]
    tok = lambda width: pl.BlockSpec((None, tm, width), lambda bi, i, *_: (bi, i, 0))
    tok_t = lambda rows: pl.BlockSpec((None, rows, tm), lambda bi, i, *_: (bi, 0, i))
    head = pl.BlockSpec((None, n_heads, tm, HEAD_PAD), lambda bi, i, *_: (bi, 0, i, 0))
    stat = pl.BlockSpec((None, None, 1, LANES), lambda bi, i, *_: (bi, i, 0, 0))
    stat_shape = jax.ShapeDtypeStruct((b, nt, 1, LANES), F32)
    out_shape = (
        jax.ShapeDtypeStruct((b, n_heads, t, HEAD_PAD), BF16),
        jax.ShapeDtypeStruct((b, n_heads, t, HEAD_PAD), BF16),
        jax.ShapeDtypeStruct((b, n_heads, V_ROWS, t), BF16),
        jax.ShapeDtypeStruct((b, aw, t), F32),
        jax.ShapeDtypeStruct((b, aw, t), F32),
        jax.ShapeDtypeStruct((b, t, n_heads), F32),
        jax.ShapeDtypeStruct((b, t, d), BF16),
        jax.ShapeDtypeStruct((b, t, d), BF16),
        jax.ShapeDtypeStruct((b, POOL_HALO, pw), F32),
        stat_shape, stat_shape, stat_shape,
    )
    out_specs = (
        head, head,
        pl.BlockSpec((None, n_heads, V_ROWS, tm), lambda bi, i, *_: (bi, 0, 0, i)),
        tok_t(aw), tok_t(aw), tok(n_heads), tok(d), tok(d),
        pl.BlockSpec((None, POOL_HALO, pw), lambda bi, i, *_: (bi, 0, 0)),
        stat, stat, stat,
    )
    grid_spec = pltpu.PrefetchScalarGridSpec(
        num_scalar_prefetch=1,
        grid=(b, nt),
        in_specs=[tok(d)] + [_const_spec(c.shape) for c in consts],
        out_specs=out_specs,
        scratch_shapes=[pltpu.VMEM((tm + POOL_HALO, pw), F32), pltpu.VMEM((1, LANES), F32)],
    )
    return pl.pallas_call(
        functools.partial(_proj_prompt_kernel, n_heads=n_heads),
        grid_spec=grid_spec,
        out_shape=out_shape,
        compiler_params=_params(("arbitrary", "arbitrary")),
        name="proj_prompt",
    )(order, h, *consts)


def _proj_sample_kernel(h_ref, state_ref, g_ref, wpool_ref, wq_ref, wk_ref, wv_ref, wf_ref,
                        bf_ref, wgp_ref, wga_ref, wpg_ref, pscale_ref,
                        z_ref, q_ref, k_ref, v_ref, lf_ref, fc_ref, gpool_ref, sga_ref,
                        *, n_heads, n_new, n_seq):
    gw = LANES
    n_pre = state_ref.shape[0] // n_seq
    u = _rms(h_ref[...], g_ref[...]).astype(BF16)

    z = _dot(u, wpool_ref[...])
    z_ref[...] = z

    def ext_rows(r, sl):
        if r < n_pre:
            return state_ref[r * n_seq:(r + 1) * n_seq, sl]
        return z[(r - n_pre) * n_seq:(r - n_pre + 1) * n_seq, sl]

    pooled = []
    for g, w in enumerate(POOL_WINDOWS):
        sl = slice(g * gw, (g + 1) * gw)
        steps = []
        for s in range(n_new):
            r = n_pre + s
            acc = ext_rows(r, sl)
            for j in range(1, w):
                if r - j >= 0:
                    acc = acc + ext_rows(r - j, sl)
            steps.append(acc / float(min(r + 1, w)) - ext_rows(r, sl))
        pooled.append(jnp.concatenate(steps, axis=0))
    branch_pool = _pool_branch(pooled, wpg_ref) * pscale_ref[...]
    gpool_ref[...] = (jax.nn.sigmoid(_dot(u, wgp_ref[...])) * branch_pool).astype(BF16)
    sga_ref[...] = jax.nn.sigmoid(_dot(u, wga_ref[...])).astype(BF16)

    tm = h_ref.shape[0]
    lane_ok = _head_lane_mask((tm, LANES), n_heads)
    logf = jnp.where(lane_ok, _log_sigmoid(_dot(u, wf_ref[...]) + bf_ref[...]), 0.0)
    lf_ref[...] = logf[:, 0:n_heads]
    run = None
    sums = []
    for s in range(n_new):
        blk = logf[s * n_seq:(s + 1) * n_seq, :]
        run = blk if run is None else run + blk
        sums.append(run)
    fc_ref[...] = jnp.concatenate(sums, axis=0)[:, 0:n_heads]

    q_ref[...] = _dot(u, wq_ref[...]).astype(BF16)
    k_ref[...] = _dot(u, wk_ref[...])
    v_ref[...] = _dot(u, wv_ref[...])


def _proj_sample(h, state, w, n_heads, n_new, n_seq):
    n, d = h.shape
    aw = w["wk"].shape[1]
    pw = w["wpool"].shape[1]
    names = ["g_mix", "wpool", "wq", "wk", "wv", "wf", "bf", "wgp", "wga", "wpg", "pscale"]
    consts = [w[n_] for n_ in names]
    full = lambda shape: pl.BlockSpec(shape, lambda i: (0,) * len(shape))
    out_shape = (
        jax.ShapeDtypeStruct((n, pw), F32),
        jax.ShapeDtypeStruct((n, aw), BF16),
        jax.ShapeDtypeStruct((n, aw), F32),
        jax.ShapeDtypeStruct((n, aw), F32),
        jax.ShapeDtypeStruct((n, n_heads), F32),
        jax.ShapeDtypeStruct((n, n_heads), F32),
        jax.ShapeDtypeStruct((n, d), BF16),
        jax.ShapeDtypeStruct((n, d), BF16),
    )
    return pl.pallas_call(
        functools.partial(_proj_sample_kernel, n_heads=n_heads, n_new=n_new, n_seq=n_seq),
        grid=(1,),
        in_specs=[full(h.shape), full(state.shape)] + [_const_spec(c.shape) for c in consts],
        out_specs=tuple(full(s.shape) for s in out_shape),
        out_shape=out_shape,
        compiler_params=_params(("arbitrary",)),
        name="proj_sample",
    )(h, state, *consts)


def _attn_kernel(q_ref, k_ref, vt_ref, qn_ref, kn_ref, fe_ref, o_ref, *, head_dim):
    i = pl.program_id(2)
    nh, tq, _ = q_ref.shape
    tk = tq
    nt = fe_ref.shape[1]
    qs = [q_ref[h] for h in range(nh)]

    def scores(h, j):
        start = pl.multiple_of(j * tk, tk)
        return lax.dot_general(k_ref[h, pl.ds(start, tk), :], qs[h], _NT, preferred_element_type=F32)

    def absorb(h, j, s, m, acc):
        start = pl.multiple_of(j * tk, tk)
        m_new = jnp.maximum(m, jnp.max(s, axis=0, keepdims=True))
        alpha = jnp.exp2(m - m_new)
        p = jnp.exp2(s - m_new).astype(BF16)
        acc = alpha * acc + _dot(vt_ref[h, :, pl.ds(start, tk)], p)
        return m_new, acc

    kpos = lax.broadcasted_iota(jnp.int32, (tk, tq), 0)
    qpos = lax.broadcasted_iota(jnp.int32, (tk, tq), 1)
    diag = [jnp.where(kpos <= qpos, scores(h, i), MASKED) for h in range(nh)]
    state = []
    for h in range(nh):
        state += list(absorb(h, i, diag[h], jnp.full((1, tq), MASKED, F32),
                             jnp.zeros((V_ROWS, tq), F32)))

    lane = lax.broadcasted_iota(jnp.int32, (1, nt), 1)
    n_back = None
    for h in range(nh):
        m_min = jnp.min(state[2 * h], axis=1, keepdims=True)
        q2 = jnp.max(jnp.where(lane == i, qn_ref[h:h + 1, :], 0.0), axis=1, keepdims=True)
        k2 = jnp.max(kn_ref[h:h + 1, :], axis=1, keepdims=True)
        bound = jnp.sqrt(q2 * k2) - fe_ref[h:h + 1, :]
        need = (lane < i) & (bound > m_min - SKIP_MARGIN)
        n_h = jnp.sum(need.astype(jnp.int32))
        n_back = n_h if n_back is None else jnp.maximum(n_back, n_h)

    def make_body(group):
        def body(jj, c):
            c = list(c)
            first = c[-1]
            tiles = [first - jj * group - g for g in range(group)]
            ss = [[scores(h, j) for h in range(nh)] for j in tiles]
            for g, j in enumerate(tiles):
                for h in range(nh):
                    c[2 * h], c[2 * h + 1] = absorb(h, j, ss[g][h], c[2 * h], c[2 * h + 1])
            return tuple(c)
        return body

    left, nearest, state = n_back, i - 1, tuple(state)
    for group in ATTN_GROUPS:
        bigger = [x for x in ATTN_GROUPS if x > group]
        trips = (left % min(bigger)) // group if bigger else left // group
        state = lax.fori_loop(0, trips, make_body(group), state + (nearest,))[:-1]
        left, nearest = left - trips * group, nearest - trips * group
    for h in range(nh):
        acc = state[2 * h + 1]
        o_ref[h * head_dim:(h + 1) * head_dim, :] = (
            acc[0:head_dim, :] / acc[head_dim:head_dim + 1, :]).astype(BF16)


def _attention(qa, ka, vt, qn, kn, fe, head_dim, tq):
    b, nh, t, _ = qa.shape
    g = ATTN_HEADS
    nt = t // tq
    stat = pl.BlockSpec((None, None, g, nt), lambda bi, h, i: (bi, h, 0, 0))
    return pl.pallas_call(
        functools.partial(_attn_kernel, head_dim=head_dim),
        grid=(b, nh // g, nt),
        in_specs=[pl.BlockSpec((None, g, tq, HEAD_PAD), lambda bi, h, i: (bi, h, i, 0)),
                  pl.BlockSpec((None, g, t, HEAD_PAD), lambda bi, h, i: (bi, h, 0, 0)),
                  pl.BlockSpec((None, g, V_ROWS, t), lambda bi, h, i: (bi, h, 0, 0)),
                  stat, stat, stat],
        out_specs=pl.BlockSpec((None, g * head_dim, tq), lambda bi, h, i: (bi, h, i)),
        out_shape=jax.ShapeDtypeStruct((b, nh * head_dim, t), BF16),
        compiler_params=_params(("parallel", "parallel", "arbitrary")),
        name="attn_prompt",
    )(qa, ka, vt, qn, kn, fe)


def _page_copies(ck_hbm, cv_hbm, cl_hbm, kbuf, vbuf, lbuf, sems, slot, page_ids):
    page = cl_hbm.shape[2]
    out = []
    for i, pid in enumerate(page_ids):
        lanes = slice(i * page, (i + 1) * page)
        out += [pltpu.make_async_copy(ck_hbm.at[pid], kbuf.at[slot, :, lanes], sems.at[slot, 0]),
                pltpu.make_async_copy(cv_hbm.at[pid], vbuf.at[slot, :, lanes], sems.at[slot, 1]),
                pltpu.make_async_copy(cl_hbm.at[pid], lbuf.at[slot, i], sems.at[slot, 2])]
    return out


def _decode_prologue(refs):
    pt_ref, ck_hbm, cv_hbm, cl_hbm, kbuf, vbuf, lbuf, sems = refs[0], *refs[6:9], *refs[10:14]
    slots = kbuf.shape[0]
    pp = kbuf.shape[2] // cl_hbm.shape[2]
    for s in range(slots):
        for c in _page_copies(ck_hbm, cv_hbm, cl_hbm, kbuf, vbuf, lbuf, sems, s,
                              [pt_ref[s * pp + i] for i in range(pp)]):
            c.start()


def _decode_step(d, n_dsteps, seq, part, n_parts, refs, n_heads, n_new, merge):
    (pt_ref, q_ref, kn_ref, vn_ref, negc_ref, upper_ref, ck_hbm, cv_hbm, cl_hbm, o_ref,
     kbuf, vbuf, lbuf, sems, m_ref, l_ref, acc_ref, fcar_ref) = refs
    slots, width, group_keys = kbuf.shape
    page = cl_hbm.shape[2]
    pp = group_keys // page
    head_dim = width // n_heads
    rows = n_new * n_heads

    def copies(slot, page_ids):
        return _page_copies(ck_hbm, cv_hbm, cl_hbm, kbuf, vbuf, lbuf, sems, slot, page_ids)

    def start(slot, first):
        for c in copies(slot, [pt_ref[first + i] for i in range(pp)]):
            c.start()

    def wait(slot):
        for c in copies(slot, [0] * pp):
            c.wait()

    if part == 0:
        m_ref[...] = jnp.full(m_ref.shape, MASKED, F32)
        l_ref[...] = jnp.zeros(l_ref.shape, F32)
        acc_ref[...] = jnp.zeros(acc_ref.shape, F32)
        fcar_ref[...] = jnp.zeros(fcar_ref.shape, F32)

    q = q_ref[seq].astype(F32)
    qrows = jnp.concatenate([jnp.broadcast_to(q[s:s + 1, :], (n_heads, width)) for s in range(n_new)],
                            axis=0)
    rid = lax.broadcasted_iota(jnp.int32, (rows, width), 0)
    cid = lax.broadcasted_iota(jnp.int32, (rows, width), 1)
    own = (cid // head_dim) == (rid % n_heads)
    qbd = jnp.where(own, qrows, 0.0)

    def update(s, vals_t):
        m_new = jnp.maximum(m_ref[...], jnp.max(s, axis=1, keepdims=True))
        alpha = jnp.exp(m_ref[...] - m_new)
        p = jnp.exp(s - m_new)
        l_ref[...] = alpha * l_ref[...] + jnp.sum(p, axis=1, keepdims=True)
        pv, at = None, 0
        for v in vals_t:
            part = lax.dot_general(p[:, at:at + v.shape[1]], v, _NT, preferred_element_type=F32)
            pv = part if pv is None else pv + part
            at += v.shape[1]
        acc_ref[...] = alpha * acc_ref[...] + pv
        m_ref[...] = m_new

    def consume(group):
        x = jnp.concatenate([lbuf[slot, i] for slot in group for i in range(pp)], axis=0)
        parts = jnp.concatenate(_split3(x), axis=0).astype(BF16)
        c = _dot(parts, upper_ref[...])
        nr = len(group) * pp * n_heads
        fin = c[0:nr] + c[nr:2 * nr] + c[2 * nr:3 * nr]
        carry = fcar_ref[...]
        biases = []
        for i in range(len(group) * pp):
            fi = fin[i * n_heads:(i + 1) * n_heads, :]
            biases.append(jnp.concatenate([-(fi + carry)] * n_new, axis=0))
            carry = carry + jnp.broadcast_to(fi[:, page - 1:page], carry.shape)
        fcar_ref[...] = carry
        scores = jnp.concatenate([_dot(qbd, kbuf[slot]) for slot in group], axis=1)
        update(scores + jnp.concatenate(biases, axis=1), [vbuf[slot] for slot in group])

    first = d * (slots * pp)
    for s0 in range(0, slots, merge):
        group = list(range(s0, s0 + merge))
        for s in group:
            wait(s)
        consume(group)

        @pl.when(d + 1 < n_dsteps)
        def _():
            for s in group:
                start(s, first + (slots + s) * pp)

    if part == n_parts - 1:
        carry = fcar_ref[...]
        sn = _dot(qbd, kn_ref[seq].astype(F32))
        bias = jnp.concatenate([negc_ref[seq] - carry] * n_new, axis=0)
        r2 = lax.broadcasted_iota(jnp.int32, sn.shape, 0)
        c2 = lax.broadcasted_iota(jnp.int32, sn.shape, 1)
        ok = (c2 < n_new) & (c2 <= r2 // n_heads)
        update(jnp.where(ok, sn + bias, MASKED), [vn_ref[seq].astype(F32)])
        out = jnp.where(own, acc_ref[...] / l_ref[...], 0.0)
        o_ref[seq] = jnp.sum(out.reshape(n_new, n_heads, width), axis=1)


N_DECODE_IN = 8


def _decode_hook(pt_ref, dec_in, o_ref, scratch, tile, n_tiles, n_parts, n_heads, n_new, merge):
    refs = (pt_ref,) + tuple(dec_in) + (o_ref,) + tuple(scratch)
    per_tile = o_ref.shape[0] * n_parts
    n_dsteps = n_tiles * per_tile

    @pl.when(tile == 0)
    def _():
        _decode_prologue(refs)

    done = [0]

    def after_chunk(c, n_chunks):
        upto = ((c + 1) * per_tile) // n_chunks
        for k in range(done[0], upto):
            _decode_step(tile * per_tile + k, n_dsteps, k // n_parts, k % n_parts, n_parts, refs,
                         n_heads, n_new, merge)
        done[0] = upto

    return after_chunk


def _decode_operands(page_table, q, kn_t, vn_t, negc, upper, cache_kt, cache_vt, cache_lft, n_tiles,
                     flat_tile, n_heads):
    bd, n_new, width = q.shape
    n_pages = page_table.shape[1]
    page = cache_kt.shape[2]
    slots = DECODE_SLOTS
    pp = min(DECODE_PAGES, n_pages // slots)
    n_parts = n_pages // (slots * pp)
    assert n_pages == n_parts * slots * pp and (bd * n_parts) % n_tiles == 0
    per_tile = bd * n_parts // n_tiles
    assert per_tile % n_parts == 0
    seqs = per_tile // n_parts
    rows = n_new * n_heads
    seq = lambda shape: pl.BlockSpec((seqs,) + shape, lambda *idx: (flat_tile(*idx[:-1]), 0, 0))
    hbm = pl.BlockSpec(memory_space=pl.ANY)
    return {
        "n_parts": n_parts,
        "page_table": page_table.reshape(-1),
        "inputs": [q, kn_t, vn_t, negc, upper, cache_kt, cache_vt, cache_lft],
        "in_specs": [seq((n_new, width)), seq((width, page)), seq((width, page)),
                     seq((n_heads, LANES)), _const_spec(upper.shape), hbm, hbm, hbm],
        "out_spec": seq((n_new, width)),
        "out_shape": jax.ShapeDtypeStruct((bd, n_new, width), F32),
        "scratch": [pltpu.VMEM((slots, width, pp * page), F32),
                    pltpu.VMEM((slots, width, pp * page), F32),
                    pltpu.VMEM((slots, pp, n_heads, page), F32),
                    pltpu.SemaphoreType.DMA((slots, 3)),
                    pltpu.VMEM((rows, 1), F32), pltpu.VMEM((rows, 1), F32),
                    pltpu.VMEM((rows, width), F32), pltpu.VMEM((n_heads, LANES), F32)],
    }


def _ffn_decode_kernel(pt_ref, x_ref, g_ref, wg_ref, wu_ref, wd_ref, *refs, n_heads, n_new, n_parts):
    dec_in, (h_ref, o_ref), scratch = refs[:N_DECODE_IN], refs[N_DECODE_IN:N_DECODE_IN + 2], \
        refs[N_DECODE_IN + 2:]
    hook = _decode_hook(pt_ref, dec_in, o_ref, scratch, pl.program_id(0), pl.num_programs(0), n_parts,
                        n_heads, n_new, DECODE_MERGE_FFN)
    h_ref[...] = _ffn_value(x_ref[...], g_ref[...], wg_ref, wu_ref, wd_ref, hook)


def _ffn_decode(x, g, wg, wu, wd, tm, n_heads, *decode_args):
    n, d = x.shape
    dec = _decode_operands(*decode_args, n // tm, lambda i: i, n_heads)
    tok = pl.BlockSpec((tm, d), lambda i, pt_ref: (i, 0))
    grid_spec = pltpu.PrefetchScalarGridSpec(
        num_scalar_prefetch=1,
        grid=(n // tm,),
        in_specs=[tok, _const_spec(g.shape), _const_spec(wg.shape), _const_spec(wu.shape),
                  _const_spec(wd.shape)] + dec["in_specs"],
        out_specs=(tok, dec["out_spec"]),
        scratch_shapes=dec["scratch"],
    )
    n_new = dec["out_shape"].shape[1]
    return pl.pallas_call(
        functools.partial(_ffn_decode_kernel, n_heads=n_heads, n_new=n_new, n_parts=dec["n_parts"]),
        grid_spec=grid_spec,
        out_shape=(jax.ShapeDtypeStruct((n, d), F32), dec["out_shape"]),
        compiler_params=_params(("arbitrary",)),
        name="ffn_decode",
    )(dec["page_table"], x, g, wg, wu, wd, *dec["inputs"])


N_POST_IN = 15


def _post_value(h_ref, gpool_ref, sga_ref, attn_ref, p_ref, wab_ref, wout_ref, g2_ref, wg_ref,
                wu_ref, wd_ref, gple_ref, wpg_ref, wple_ref, gfin_ref, attn_transposed, after_chunk):
    if attn_transposed:
        branch_attn = lax.dot_general(attn_ref[...], wab_ref[...], _TN, preferred_element_type=F32)
    else:
        branch_attn = _dot(attn_ref[...].astype(BF16), wab_ref[...])
    merged = gpool_ref[...].astype(F32) + sga_ref[...].astype(F32) * branch_attn
    h = h_ref[...] + _dot(merged.astype(BF16), wout_ref[...])
    h = _ffn_value(h, g2_ref[...], wg_ref, wu_ref, wd_ref, after_chunk)
    gate = jax.nn.sigmoid(_dot(_rms(h, gple_ref[...]).astype(BF16), wpg_ref[...]))
    h = h + _dot(p_ref[...].astype(BF16), wple_ref[...]) * gate
    return _rms(h, gfin_ref[...])


def _post_kernel(*refs, attn_transposed):
    y_ref = refs[N_POST_IN]
    y_ref[...] = _post_value(*refs[:N_POST_IN], attn_transposed, None)


def _post_decode_kernel(pt_ref, *refs, attn_transposed, n_heads, n_new, n_parts):
    ins, refs = refs[:N_POST_IN], refs[N_POST_IN:]
    dec_in, (y_ref, o_ref), scratch = refs[:N_DECODE_IN], refs[N_DECODE_IN:N_DECODE_IN + 2], \
        refs[N_DECODE_IN + 2:]
    tile = pl.program_id(0) * pl.num_programs(1) + pl.program_id(1)
    hook = _decode_hook(pt_ref, dec_in, o_ref, scratch, tile, pl.num_programs(0) * pl.num_programs(1),
                        n_parts, n_heads, n_new, DECODE_MERGE_POST)
    y_ref[...] = _post_value(*ins, attn_transposed, hook)


def _post(h, gpool, sga, attn, p, w, tm, attn_transposed, n_heads=None, decode_args=None):
    b, t, d = h.shape
    nt = t // tm
    names = ["wab2" if attn_transposed else "wab", "wout", "g_ffn2", "wg2", "wu2", "wd2", "g_ple",
             "wpgate", "wple", "g_final"]
    consts = [w[n] for n in names]
    tok = lambda width: pl.BlockSpec((None, tm, width), lambda bi, i, *_: (bi, i, 0))
    if attn_transposed:
        attn_spec = pl.BlockSpec((None, attn.shape[1], tm), lambda bi, i, *_: (bi, 0, i))
    else:
        attn_spec = tok(attn.shape[2])
    in_specs = [tok(d), tok(d), tok(d), attn_spec, tok(p.shape[2])] + [_const_spec(c.shape) for c in consts]
    y_shape = jax.ShapeDtypeStruct((b, t, d), F32)
    if decode_args is None:
        return pl.pallas_call(
            functools.partial(_post_kernel, attn_transposed=attn_transposed),
            grid=(b, nt),
            in_specs=in_specs,
            out_specs=tok(d),
            out_shape=y_shape,
            compiler_params=_params(("parallel", "parallel")),
            name="post",
        )(h, gpool, sga, attn, p, *consts)
    dec = _decode_operands(*decode_args, b * nt, lambda bi, i: bi * nt + i, n_heads)
    grid_spec = pltpu.PrefetchScalarGridSpec(
        num_scalar_prefetch=1,
        grid=(b, nt),
        in_specs=in_specs + dec["in_specs"],
        out_specs=(tok(d), dec["out_spec"]),
        scratch_shapes=dec["scratch"],
    )
    return pl.pallas_call(
        functools.partial(_post_decode_kernel, attn_transposed=attn_transposed, n_heads=n_heads,
                          n_new=dec["out_shape"].shape[1], n_parts=dec["n_parts"]),
        grid_spec=grid_spec,
        out_shape=(y_shape, dec["out_shape"]),
        compiler_params=_params(("arbitrary", "arbitrary"), POST_DECODE_VMEM_LIMIT),
        name="post_decode",
    )(dec["page_table"], h, gpool, sga, attn, p, *consts, *dec["inputs"])


def _prep_weights(g_ffn1, w_ffn1_gate, w_ffn1_up, w_ffn1_down, g_mix, w_in, b_forget, w_pool_group,
                  pool_scale, w_attn_branch, w_out, g_ffn2, w_ffn2_gate, w_ffn2_up, w_ffn2_down,
                  g_ple, w_ple_gate, w_ple, g_final, n_heads, head_dim, pool_width, tm, order):
    d = w_in.shape[0]
    aw = n_heads * head_dim
    o = 0
    wpool = w_in[:, o:o + pool_width]; o += pool_width
    wq = w_in[:, o:o + aw] * (head_dim ** -0.5); o += aw
    wk = w_in[:, o:o + aw]; o += aw
    wv = w_in[:, o:o + aw]; o += aw
    wf = w_in[:, o:o + n_heads]; o += n_heads
    wgp = w_in[:, o:o + d]; o += d
    wga = w_in[:, o:o + d]

    qones = np.zeros((1, HEAD_PAD), np.float32)
    qones[0, head_dim:head_dim + F_PARTS] = 1.0
    headsel = np.zeros((aw, LANES), np.float32)
    for h in range(n_heads):
        headsel[h * head_dim:(h + 1) * head_dim, h] = 1.0
    parts, slots = np.arange(F_PARTS)[:, None], np.arange(n_heads)[None, :]
    place = jnp.zeros((LANES, n_heads * HEAD_PAD), F32).at[
        parts * n_heads + order[None, :], slots * HEAD_PAD + head_dim + parts].set(-1.0)
    by_slot_cols = lambda x: x.reshape(d, n_heads, head_dim)[:, order].reshape(d, aw)
    by_slot_rows = lambda x: x.reshape(n_heads, head_dim, -1)[order].reshape(aw, -1)
    row = lambda x: x.reshape(1, -1).astype(F32)
    bf = lambda x: x.astype(BF16)
    return {
        "g_ffn1": row(g_ffn1), "wg1": bf(w_ffn1_gate), "wu1": bf(w_ffn1_up), "wd1": bf(w_ffn1_down),
        "g_mix": row(g_mix), "wpool": bf(wpool), "wq": bf(wq), "wk": bf(wk), "wv": bf(wv),
        "wq2": bf(by_slot_cols(wq * LOG2E)), "wk2": bf(by_slot_cols(wk)),
        "wab2": bf(by_slot_rows(w_attn_branch)),
        "wkt": bf(wk.T), "wvt": bf(wv.T),
        "wf": bf(jnp.pad(wf, ((0, 0), (0, LANES - n_heads)))),
        "bf": jnp.pad(row(b_forget), ((0, 0), (0, LANES - n_heads))),
        "wgp": bf(wgp), "wga": bf(wga), "wpg": bf(w_pool_group), "pscale": row(pool_scale),
        "place": bf(place), "qones": jnp.asarray(qones, F32),
        "headsel": jnp.asarray(headsel, BF16),
        "tri": jnp.asarray(np.tril(np.ones((tm, tm), np.float32)), BF16),
        "wab": bf(w_attn_branch), "wout": bf(w_out),
        "g_ffn2": row(g_ffn2), "wg2": bf(w_ffn2_gate), "wu2": bf(w_ffn2_up), "wd2": bf(w_ffn2_down),
        "g_ple": row(g_ple), "wpgate": bf(w_ple_gate), "wple": bf(w_ple), "g_final": row(g_final),
    }


def kernel(x_prompt, x_sample, cache_k, cache_v, cache_logf, state_pool, page_table, p_prompt, p_sample, g_ffn1, w_ffn1_gate, w_ffn1_up, w_ffn1_down, g_mix, w_in, b_forget, w_pool_group, pool_scale, w_attn_branch, w_out, g_ffn2, w_ffn2_gate, w_ffn2_up, w_ffn2_down, g_ple, w_ple_gate, w_ple, g_final):
    depth = cache_k.shape[0]
    assert depth == 1, "one trunk layer"
    b, t, d = x_prompt.shape
    bd, n_new, _ = x_sample.shape
    _, n_pool_pages, page, n_heads, head_dim = cache_k.shape
    aw = n_heads * head_dim
    n_pre, pool_width = state_pool.shape[2], state_pool.shape[3]
    assert head_dim + F_PARTS <= HEAD_PAD and head_dim < V_ROWS and page == LANES
    assert pool_width == len(POOL_WINDOWS) * LANES and n_pre == POOL_WINDOWS[-1] - 1
    assert n_heads % ATTN_HEADS == 0
    tm = min(TOKEN_TILE, t)
    assert t // tm <= LANES
    ns = bd * n_new

    order = jnp.argsort(b_forget[0]).astype(jnp.int32)
    w = _prep_weights(g_ffn1[0], w_ffn1_gate[0], w_ffn1_up[0], w_ffn1_down[0], g_mix[0], w_in[0],
                      b_forget[0], w_pool_group[0], pool_scale[0], w_attn_branch[0], w_out[0],
                      g_ffn2[0], w_ffn2_gate[0], w_ffn2_up[0], w_ffn2_down[0], g_ple[0],
                      w_ple_gate[0], w_ple[0], g_final, n_heads, head_dim, pool_width, tm, order)

    step_major = lambda x: jnp.swapaxes(x, 0, 1).reshape(ns, x.shape[-1])
    seq_major = lambda x: jnp.swapaxes(x.reshape(n_new, bd, x.shape[-1]), 0, 1)
    hs1 = _ffn(step_major(x_sample), w["g_ffn1"], w["wg1"], w["wu1"], w["wd1"], ns)
    state = jnp.swapaxes(state_pool[0], 0, 1).reshape(n_pre * bd, pool_width)
    z_s, q_s, k_s, v_s, lf_s, fc_s, gpool_s, sga_s = _proj_sample(hs1, state, w, n_heads, n_new, bd)
    k_s, v_s, lf_s = seq_major(k_s), seq_major(v_s), seq_major(lf_s)
    new_page = lambda x: jnp.pad(jnp.swapaxes(x, 1, 2).astype(BF16), ((0, 0), (0, 0), (0, page - n_new)))
    negc = -jnp.swapaxes(seq_major(fc_s), 1, 2)
    negc = jnp.pad(negc, ((0, 0), (0, 0), (0, LANES - n_new)))
    upper = jnp.asarray(np.triu(np.ones((page, page), np.float32)), BF16)
    pages_t = lambda c: jnp.transpose(c[0], (0, 2, 3, 1)).reshape(n_pool_pages, aw, page)

    dec_args = (page_table, seq_major(q_s), new_page(k_s), new_page(v_s), negc)
    caches = (upper, pages_t(cache_k), pages_t(cache_v), jnp.swapaxes(cache_logf[0], 1, 2))
    half = bd // 2
    h1, attn_a = _ffn_decode(x_prompt.reshape(b * t, d), w["g_ffn1"], w["wg1"], w["wu1"], w["wd1"], tm,
                             n_heads, *(a[:half] for a in dec_args), *caches)
    h1 = h1.reshape(b, t, d)
    qa, ka, vt, kt_p, vt_p, lf_p, gpool, sga, zlast, qn, kn, fe = _proj_prompt(h1, w, tm, n_heads,
                                                                               order)
    stat = lambda x: jnp.swapaxes(x[:, :, 0, :n_heads], 1, 2).reshape(
        b, n_heads // ATTN_HEADS, ATTN_HEADS, t // tm)
    attn_t = _attention(qa, ka, vt, stat(qn), stat(kn), stat(fe[..., order]), head_dim, tm)
    y_prompt, attn_b = _post(h1, gpool, sga, attn_t, p_prompt[0], w, tm, True, n_heads,
                             tuple(a[half:] for a in dec_args) + caches)
    heads_last = lambda x: jnp.transpose(x.reshape(b, n_heads, head_dim, t), (0, 3, 1, 2))[None]

    attn_s = jnp.concatenate([attn_a, attn_b], axis=0)
    y_s = _post(hs1[None], gpool_s[None], sga_s[None], step_major(attn_s)[None],
                step_major(p_sample[0])[None], w, ns, False)
    y_sample = seq_major(y_s[0])
    pool_sample = jnp.concatenate([state_pool[0], seq_major(z_s)], axis=1)[:, -n_pre:]

    return (y_prompt, y_sample, heads_last(kt_p), heads_last(vt_p),
            lf_p[None], zlast[None, :, POOL_HALO - n_pre:],
            k_s.reshape(1, bd, n_new, n_heads, head_dim), v_s.reshape(1, bd, n_new, n_heads, head_dim),
            lf_s[None], pool_sample[None])
```

```python
import functools

import numpy as np
import jax
import jax.numpy as jnp
from jax import lax
from jax.experimental import pallas as pl
from jax.experimental.pallas import tpu as pltpu

F32 = jnp.float32
BF16 = jnp.bfloat16

RMS_EPS = 1e-6
MASKED = -1e30
POOL_WINDOWS = (2, 4, 8, 16)
POOL_HALO = 16

LANES = 128
HEAD_PAD = 128
V_ROWS = 80
F_PARTS = 3
FF_CHUNK = 768
TOKEN_TILE = 512
DECODE_PAGES = 8
DECODE_SLOTS = 4
DECODE_MERGE = 2
ATTN_HEADS = 2
ATTN_GROUPS = (1, 2)
VMEM_LIMIT = 56 * 1024 * 1024
POST_DECODE_VMEM_LIMIT = 60 * 1024 * 1024
LOG2E = 1.4426950408889634
NORM_SLACK = 1.01
SKIP_MARGIN = 140.0

_NT = (((1,), (1,)), ((), ()))
_TN = (((0,), (0,)), ((), ()))


def _rms(x, g):
    r = lax.rsqrt(jnp.mean(x * x, axis=-1, keepdims=True) + RMS_EPS)
    return x * r * g


def _dot(a, b):
    return jnp.dot(a, b, preferred_element_type=F32)


def _chunks(n, c):
    return [(s, min(c, n - s)) for s in range(0, n, c)]


def _split3(x):
    hi = x.astype(BF16).astype(F32)
    r = x - hi
    mid = r.astype(BF16).astype(F32)
    lo = (r - mid).astype(BF16).astype(F32)
    return hi, mid, lo


def _pack3(x, n):
    hi, mid, lo = _split3(x)
    return (hi + pltpu.roll(mid, n, 1) + pltpu.roll(lo, 2 * n, 1)).astype(BF16)


def _unpack3(c, n):
    return c + pltpu.roll(c, LANES - n, 1) + pltpu.roll(c, LANES - 2 * n, 1)


def _log_sigmoid(x):
    return jnp.minimum(x, 0.0) - jnp.log1p(jnp.exp(-jnp.abs(x)))


def _const_spec(shape):
    nd = len(shape)
    return pl.BlockSpec(shape, lambda *_: (0,) * nd, pipeline_mode=pl.Buffered(1))


def _params(sem, vmem_limit=None):
    return pltpu.CompilerParams(dimension_semantics=sem, vmem_limit_bytes=vmem_limit or VMEM_LIMIT)


def _ffn_value(x, g, wg_ref, wu_ref, wd_ref, after_chunk=None):
    u = _rms(x, g).astype(BF16)
    acc = None
    chunks = _chunks(wg_ref.shape[1], FF_CHUNK)
    for c, (s, n) in enumerate(chunks):
        gate = _dot(u, wg_ref[:, s:s + n])
        up = _dot(u, wu_ref[:, s:s + n])
        a = (gate * jax.nn.sigmoid(gate) * up).astype(BF16)
        d = _dot(a, wd_ref[s:s + n, :])
        acc = d if acc is None else acc + d
        if after_chunk is not None:
            after_chunk(c, len(chunks))
    return x + 0.5 * acc


def _ffn_kernel(x_ref, g_ref, wg_ref, wu_ref, wd_ref, o_ref):
    o_ref[...] = _ffn_value(x_ref[...], g_ref[...], wg_ref, wu_ref, wd_ref)


def _ffn(x, g, wg, wu, wd, tm):
    n, d = x.shape
    return pl.pallas_call(
        _ffn_kernel,
        grid=(n // tm,),
        in_specs=[pl.BlockSpec((tm, d), lambda i: (i, 0)),
                  _const_spec(g.shape), _const_spec(wg.shape), _const_spec(wu.shape),
                  _const_spec(wd.shape)],
        out_specs=pl.BlockSpec((tm, d), lambda i: (i, 0)),
        out_shape=jax.ShapeDtypeStruct((n, d), F32),
        compiler_params=_params(("parallel",)),
        name="ffn",
    )(x, g, wg, wu, wd)


def _pool_branch(pooled_groups, wpg_ref):
    outs = [_dot(p.astype(BF16), wpg_ref[g]) for g, p in enumerate(pooled_groups)]
    return jnp.concatenate(outs, axis=1)


def _head_lane_mask(shape, n_heads):
    lane = lax.broadcasted_iota(jnp.int32, shape, 1)
    return lane < n_heads


def _proj_prompt_kernel(order_ref, h_ref, g_ref, wpool_ref, wq_ref, wk_ref, wkt_ref, wvt_ref,
                        wf_ref, bf_ref, wgp_ref, wga_ref, wpg_ref, pscale_ref, place_ref,
                        tri_ref, qones_ref, headsel_ref,
                        qa_ref, ka_ref, vt_ref, kt_ref, vto_ref, lf_ref, gpool_ref, sga_ref,
                        zlast_ref, qn_ref, kn_ref, fe_ref, zext_ref, fcarry_ref, *, n_heads):
    i = pl.program_id(1)
    tm = h_ref.shape[0]
    gw = LANES

    @pl.when(i == 0)
    def _():
        zext_ref[0:POOL_HALO, :] = jnp.zeros((POOL_HALO, zext_ref.shape[1]), F32)
        fcarry_ref[...] = jnp.zeros(fcarry_ref.shape, F32)

    u = _rms(h_ref[...], g_ref[...]).astype(BF16)

    z = _dot(u, wpool_ref[...])
    zext_ref[POOL_HALO:POOL_HALO + tm, :] = z
    row = lax.broadcasted_iota(jnp.int32, (tm, gw), 0) + i * tm
    pooled = []
    for g, w in enumerate(POOL_WINDOWS):
        sl = slice(g * gw, (g + 1) * gw)
        zg = z[:, sl]
        acc = zg
        for j in range(1, w):
            acc = acc + zext_ref[POOL_HALO - j:POOL_HALO - j + tm, sl]
        cnt = jnp.minimum(row + 1, w).astype(F32)
        pooled.append(acc / cnt - zg)
    zlast_ref[...] = zext_ref[tm:tm + POOL_HALO, :]
    zext_ref[0:POOL_HALO, :] = zext_ref[tm:tm + POOL_HALO, :]
    branch_pool = _pool_branch(pooled, wpg_ref) * pscale_ref[...]
    gpool_ref[...] = (jax.nn.sigmoid(_dot(u, wgp_ref[...])) * branch_pool).astype(BF16)
    sga_ref[...] = jax.nn.sigmoid(_dot(u, wga_ref[...])).astype(BF16)

    lane_ok = _head_lane_mask((tm, LANES), n_heads)
    logf = jnp.where(lane_ok, _log_sigmoid(_dot(u, wf_ref[...]) + bf_ref[...]), 0.0)
    lf_ref[...] = logf[:, 0:n_heads]
    csum = _unpack3(_dot(tri_ref[...], _pack3(logf, n_heads)), n_heads)
    fcum = jnp.where(lane_ok, csum + fcarry_ref[...], 0.0)
    fcarry_ref[...] = fcum[tm - 1:tm, :]
    fcum2 = fcum * LOG2E
    fe_ref[...] = fcum2[tm - 1:tm, :]

    q = _dot(u, wq_ref[...]).astype(BF16).astype(F32)
    k = _dot(u, wk_ref[...]).astype(BF16).astype(F32)
    for x, n_ref in ((q, qn_ref), (k, kn_ref)):
        sq = (x * x * NORM_SLACK).astype(BF16)
        n_ref[...] = jnp.max(_dot(sq, headsel_ref[...]), axis=0, keepdims=True)
    kt = lax.dot_general(wkt_ref[...], u, _NT, preferred_element_type=F32)
    vt = lax.dot_general(wvt_ref[...], u, _NT, preferred_element_type=F32)
    kt_ref[...] = kt
    vto_ref[...] = vt

    head_dim = kt.shape[0] // n_heads
    per_slab = LANES // head_dim
    fcols = _dot(_pack3(fcum2, n_heads), place_ref[...])
    low = lax.broadcasted_iota(jnp.int32, (tm, LANES), 1) < head_dim
    rid = lax.broadcasted_iota(jnp.int32, (V_ROWS - head_dim, tm), 0)
    ones_rows = jnp.where(rid == 0, 1.0, 0.0).astype(BF16)
    for h in range(n_heads):
        col = (h // per_slab) * LANES
        shift = (h % per_slab) * head_dim
        qs, ks = q[:, col:col + LANES], k[:, col:col + LANES]
        if shift:
            qs, ks = pltpu.roll(qs, LANES - shift, 1), pltpu.roll(ks, LANES - shift, 1)
        qa_ref[h] = jnp.where(low, qs, qones_ref[...]).astype(BF16)
        ka_ref[h] = jnp.where(low, ks, fcols[:, h * HEAD_PAD:(h + 1) * HEAD_PAD]).astype(BF16)
        v_at = pl.multiple_of(order_ref[h] * head_dim, head_dim)
        vt_ref[h, 0:head_dim, :] = vto_ref[pl.ds(v_at, head_dim), :].astype(BF16)
        vt_ref[h, head_dim:V_ROWS, :] = ones_rows


def _proj_prompt(h, w, tm, n_heads, order):
    b, t, d = h.shape
    aw = w["wkt"].shape[0]
    pw = w["wpool"].shape[1]
    nt = t // tm
    names = ["g_mix", "wpool", "wq2", "wk2", "wkt", "wvt", "wf", "bf", "wgp", "wga", "wpg",
             "pscale", "place", "tri", "qones", "headsel"]
    consts = [w[n] for n in names]
    tok = lambda width: pl.BlockSpec((None, tm, width), lambda bi, i, *_: (bi, i, 0))
    tok_t = lambda rows: pl.BlockSpec((None, rows, tm), lambda bi, i, *_: (bi, 0, i))
    head = pl.BlockSpec((None, n_heads, tm, HEAD_PAD), lambda bi, i, *_: (bi, 0, i, 0))
    stat = pl.BlockSpec((None, None, 1, LANES), lambda bi, i, *_: (bi, i, 0, 0))
    stat_shape = jax.ShapeDtypeStruct((b, nt, 1, LANES), F32)
    out_shape = (
        jax.ShapeDtypeStruct((b, n_heads, t, HEAD_PAD), BF16),
        jax.ShapeDtypeStruct((b, n_heads, t, HEAD_PAD), BF16),
        jax.ShapeDtypeStruct((b, n_heads, V_ROWS, t), BF16),
        jax.ShapeDtypeStruct((b, aw, t), F32),
        jax.ShapeDtypeStruct((b, aw, t), F32),
        jax.ShapeDtypeStruct((b, t, n_heads), F32),
        jax.ShapeDtypeStruct((b, t, d), BF16),
        jax.ShapeDtypeStruct((b, t, d), BF16),
        jax.ShapeDtypeStruct((b, POOL_HALO, pw), F32),
        stat_shape, stat_shape, stat_shape,
    )
    out_specs = (
        head, head,
        pl.BlockSpec((None, n_heads, V_ROWS, tm), lambda bi, i, *_: (bi, 0, 0, i)),
        tok_t(aw), tok_t(aw), tok(n_heads), tok(d), tok(d),
        pl.BlockSpec((None, POOL_HALO, pw), lambda bi, i, *_: (bi, 0, 0)),
        stat, stat, stat,
    )
    grid_spec = pltpu.PrefetchScalarGridSpec(
        num_scalar_prefetch=1,
        grid=(b, nt),
        in_specs=[tok(d)] + [_const_spec(c.shape) for c in consts],
        out_specs=out_specs,
        scratch_shapes=[pltpu.VMEM((tm + POOL_HALO, pw), F32), pltpu.VMEM((1, LANES), F32)],
    )
    return pl.pallas_call(
        functools.partial(_proj_prompt_kernel, n_heads=n_heads),
        grid_spec=grid_spec,
        out_shape=out_shape,
        compiler_params=_params(("arbitrary", "arbitrary")),
        name="proj_prompt",
    )(order, h, *consts)


def _proj_sample_kernel(h_ref, state_ref, g_ref, wpool_ref, wq_ref, wk_ref, wv_ref, wf_ref,
                        bf_ref, wgp_ref, wga_ref, wpg_ref, pscale_ref,
                        z_ref, q_ref, k_ref, v_ref, lf_ref, fc_ref, gpool_ref, sga_ref,
                        *, n_heads, n_new, n_seq):
    gw = LANES
    n_pre = state_ref.shape[0] // n_seq
    u = _rms(h_ref[...], g_ref[...]).astype(BF16)

    z = _dot(u, wpool_ref[...])
    z_ref[...] = z

    def ext_rows(r, sl):
        if r < n_pre:
            return state_ref[r * n_seq:(r + 1) * n_seq, sl]
        return z[(r - n_pre) * n_seq:(r - n_pre + 1) * n_seq, sl]

    pooled = []
    for g, w in enumerate(POOL_WINDOWS):
        sl = slice(g * gw, (g + 1) * gw)
        steps = []
        for s in range(n_new):
            r = n_pre + s
            acc = ext_rows(r, sl)
            for j in range(1, w):
                if r - j >= 0:
                    acc = acc + ext_rows(r - j, sl)
            steps.append(acc / float(min(r + 1, w)) - ext_rows(r, sl))
        pooled.append(jnp.concatenate(steps, axis=0))
    branch_pool = _pool_branch(pooled, wpg_ref) * pscale_ref[...]
    gpool_ref[...] = (jax.nn.sigmoid(_dot(u, wgp_ref[...])) * branch_pool).astype(BF16)
    sga_ref[...] = jax.nn.sigmoid(_dot(u, wga_ref[...])).astype(BF16)

    tm = h_ref.shape[0]
    lane_ok = _head_lane_mask((tm, LANES), n_heads)
    logf = jnp.where(lane_ok, _log_sigmoid(_dot(u, wf_ref[...]) + bf_ref[...]), 0.0)
    lf_ref[...] = logf[:, 0:n_heads]
    run = None
    sums = []
    for s in range(n_new):
        blk = logf[s * n_seq:(s + 1) * n_seq, :]
        run = blk if run is None else run + blk
        sums.append(run)
    fc_ref[...] = jnp.concatenate(sums, axis=0)[:, 0:n_heads]

    q_ref[...] = _dot(u, wq_ref[...]).astype(BF16)
    k_ref[...] = _dot(u, wk_ref[...])
    v_ref[...] = _dot(u, wv_ref[...])


def _proj_sample(h, state, w, n_heads, n_new, n_seq):
    n, d = h.shape
    aw = w["wk"].shape[1]
    pw = w["wpool"].shape[1]
    names = ["g_mix", "wpool", "wq", "wk", "wv", "wf", "bf", "wgp", "wga", "wpg", "pscale"]
    consts = [w[n_] for n_ in names]
    full = lambda shape: pl.BlockSpec(shape, lambda i: (0,) * len(shape))
    out_shape = (
        jax.ShapeDtypeStruct((n, pw), F32),
        jax.ShapeDtypeStruct((n, aw), BF16),
        jax.ShapeDtypeStruct((n, aw), F32),
        jax.ShapeDtypeStruct((n, aw), F32),
        jax.ShapeDtypeStruct((n, n_heads), F32),
        jax.ShapeDtypeStruct((n, n_heads), F32),
        jax.ShapeDtypeStruct((n, d), BF16),
        jax.ShapeDtypeStruct((n, d), BF16),
    )
    return pl.pallas_call(
        functools.partial(_proj_sample_kernel, n_heads=n_heads, n_new=n_new, n_seq=n_seq),
        grid=(1,),
        in_specs=[full(h.shape), full(state.shape)] + [_const_spec(c.shape) for c in consts],
        out_specs=tuple(full(s.shape) for s in out_shape),
        out_shape=out_shape,
        compiler_params=_params(("arbitrary",)),
        name="proj_sample",
    )(h, state, *consts)


def _attn_kernel(q_ref, k_ref, vt_ref, qn_ref, kn_ref, fe_ref, o_ref, *, head_dim):
    i = pl.program_id(2)
    nh, tq, _ = q_ref.shape
    tk = tq
    nt = fe_ref.shape[1]
    qs = [q_ref[h] for h in range(nh)]

    def scores(h, j):
        start = pl.multiple_of(j * tk, tk)
        return lax.dot_general(k_ref[h, pl.ds(start, tk), :], qs[h], _NT, preferred_element_type=F32)

    def absorb(h, j, s, m, acc):
        start = pl.multiple_of(j * tk, tk)
        m_new = jnp.maximum(m, jnp.max(s, axis=0, keepdims=True))
        alpha = jnp.exp2(m - m_new)
        p = jnp.exp2(s - m_new).astype(BF16)
        acc = alpha * acc + _dot(vt_ref[h, :, pl.ds(start, tk)], p)
        return m_new, acc

    kpos = lax.broadcasted_iota(jnp.int32, (tk, tq), 0)
    qpos = lax.broadcasted_iota(jnp.int32, (tk, tq), 1)
    diag = [jnp.where(kpos <= qpos, scores(h, i), MASKED) for h in range(nh)]
    state = []
    for h in range(nh):
        state += list(absorb(h, i, diag[h], jnp.full((1, tq), MASKED, F32),
                             jnp.zeros((V_ROWS, tq), F32)))

    lane = lax.broadcasted_iota(jnp.int32, (1, nt), 1)
    n_back = None
    for h in range(nh):
        m_min = jnp.min(state[2 * h], axis=1, keepdims=True)
        q2 = jnp.max(jnp.where(lane == i, qn_ref[h:h + 1, :], 0.0), axis=1, keepdims=True)
        k2 = jnp.max(kn_ref[h:h + 1, :], axis=1, keepdims=True)
        bound = jnp.sqrt(q2 * k2) - fe_ref[h:h + 1, :]
        need = (lane < i) & (bound > m_min - SKIP_MARGIN)
        n_h = jnp.sum(need.astype(jnp.int32))
        n_back = n_h if n_back is None else jnp.maximum(n_back, n_h)

    def make_body(group):
        def body(jj, c):
            c = list(c)
            first = c[-1]
            tiles = [first - jj * group - g for g in range(group)]
            ss = [[scores(h, j) for h in range(nh)] for j in tiles]
            for g, j in enumerate(tiles):
                for h in range(nh):
                    c[2 * h], c[2 * h + 1] = absorb(h, j, ss[g][h], c[2 * h], c[2 * h + 1])
            return tuple(c)
        return body

    left, nearest, state = n_back, i - 1, tuple(state)
    for group in ATTN_GROUPS:
        bigger = [x for x in ATTN_GROUPS if x > group]
        trips = (left % min(bigger)) // group if bigger else left // group
        state = lax.fori_loop(0, trips, make_body(group), state + (nearest,))[:-1]
        left, nearest = left - trips * group, nearest - trips * group
    for h in range(nh):
        acc = state[2 * h + 1]
        o_ref[h * head_dim:(h + 1) * head_dim, :] = (
            acc[0:head_dim, :] / acc[head_dim:head_dim + 1, :]).astype(BF16)


def _attention(qa, ka, vt, qn, kn, fe, head_dim, tq):
    b, nh, t, _ = qa.shape
    g = ATTN_HEADS
    nt = t // tq
    stat = pl.BlockSpec((None, None, g, nt), lambda bi, h, i: (bi, h, 0, 0))
    return pl.pallas_call(
        functools.partial(_attn_kernel, head_dim=head_dim),
        grid=(b, nh // g, nt),
        in_specs=[pl.BlockSpec((None, g, tq, HEAD_PAD), lambda bi, h, i: (bi, h, i, 0)),
                  pl.BlockSpec((None, g, t, HEAD_PAD), lambda bi, h, i: (bi, h, 0, 0)),
                  pl.BlockSpec((None, g, V_ROWS, t), lambda bi, h, i: (bi, h, 0, 0)),
                  stat, stat, stat],
        out_specs=pl.BlockSpec((None, g * head_dim, tq), lambda bi, h, i: (bi, h, i)),
        out_shape=jax.ShapeDtypeStruct((b, nh * head_dim, t), BF16),
        compiler_params=_params(("parallel", "parallel", "arbitrary")),
        name="attn_prompt",
    )(qa, ka, vt, qn, kn, fe)


def _page_copies(ck_hbm, cv_hbm, cl_hbm, kbuf, vbuf, lbuf, sems, slot, page_ids):
    page = cl_hbm.shape[2]
    out = []
    for i, pid in enumerate(page_ids):
        lanes = slice(i * page, (i + 1) * page)
        out += [pltpu.make_async_copy(ck_hbm.at[pid], kbuf.at[slot, :, lanes], sems.at[slot, 0]),
                pltpu.make_async_copy(cv_hbm.at[pid], vbuf.at[slot, :, lanes], sems.at[slot, 1]),
                pltpu.make_async_copy(cl_hbm.at[pid], lbuf.at[slot, i], sems.at[slot, 2])]
    return out


def _start_copies(copies):
    for n, c in enumerate(copies):
        c.start(priority=1 if n % 3 == 1 else 0)


def _decode_prologue(refs):
    pt_ref, ck_hbm, cv_hbm, cl_hbm, kbuf, vbuf, lbuf, sems = refs[0], *refs[6:9], *refs[10:14]
    slots = kbuf.shape[0]
    pp = kbuf.shape[2] // cl_hbm.shape[2]
    for s in range(slots):
        _start_copies(_page_copies(ck_hbm, cv_hbm, cl_hbm, kbuf, vbuf, lbuf, sems, s,
                                   [pt_ref[s * pp + i] for i in range(pp)]))


def _decode_step(d, n_dsteps, seq, part, n_parts, refs, n_heads, n_new):
    (pt_ref, q_ref, kn_ref, vn_ref, negc_ref, upper_ref, ck_hbm, cv_hbm, cl_hbm, o_ref,
     kbuf, vbuf, lbuf, sems, m_ref, l_ref, acc_ref, fcar_ref) = refs
    slots, width, group_keys = kbuf.shape
    page = cl_hbm.shape[2]
    pp = group_keys // page
    head_dim = width // n_heads
    rows = n_new * n_heads

    def copies(slot, page_ids):
        return _page_copies(ck_hbm, cv_hbm, cl_hbm, kbuf, vbuf, lbuf, sems, slot, page_ids)

    def start(slot, first):
        _start_copies(copies(slot, [pt_ref[first + i] for i in range(pp)]))

    def wait(slot):
        for c in copies(slot, [0] * pp):
            c.wait()

    if part == 0:
        m_ref[...] = jnp.full(m_ref.shape, MASKED, F32)
        l_ref[...] = jnp.zeros(l_ref.shape, F32)
        acc_ref[...] = jnp.zeros(acc_ref.shape, F32)
        fcar_ref[...] = jnp.zeros(fcar_ref.shape, F32)

    q = q_ref[seq].astype(F32)
    qrows = jnp.concatenate([jnp.broadcast_to(q[s:s + 1, :], (n_heads, width)) for s in range(n_new)],
                            axis=0)
    rid = lax.broadcasted_iota(jnp.int32, (rows, width), 0)
    cid = lax.broadcasted_iota(jnp.int32, (rows, width), 1)
    own = (cid // head_dim) == (rid % n_heads)
    qbd = jnp.where(own, qrows, 0.0)

    def update(s, vals_t):
        m_new = jnp.maximum(m_ref[...], jnp.max(s, axis=1, keepdims=True))
        alpha = jnp.exp(m_ref[...] - m_new)
        p = jnp.exp(s - m_new)
        l_ref[...] = alpha * l_ref[...] + jnp.sum(p, axis=1, keepdims=True)
        pv, at = None, 0
        for v in vals_t:
            part = lax.dot_general(p[:, at:at + v.shape[1]], v, _NT, preferred_element_type=F32)
            pv = part if pv is None else pv + part
            at += v.shape[1]
        acc_ref[...] = alpha * acc_ref[...] + pv
        m_ref[...] = m_new

    def consume(group):
        x = jnp.concatenate([lbuf[slot, i] for slot in group for i in range(pp)], axis=0)
        parts = jnp.concatenate(_split3(x), axis=0).astype(BF16)
        c = _dot(parts, upper_ref[...])
        nr = len(group) * pp * n_heads
        fin = c[0:nr] + c[nr:2 * nr] + c[2 * nr:3 * nr]
        carry = fcar_ref[...]
        biases = []
        for i in range(len(group) * pp):
            fi = fin[i * n_heads:(i + 1) * n_heads, :]
            biases.append(jnp.concatenate([-(fi + carry)] * n_new, axis=0))
            carry = carry + jnp.broadcast_to(fi[:, page - 1:page], carry.shape)
        fcar_ref[...] = carry
        scores = jnp.concatenate([_dot(qbd, kbuf[slot]) for slot in group], axis=1)
        update(scores + jnp.concatenate(biases, axis=1), [vbuf[slot] for slot in group])

    first = d * (slots * pp)
    for s0 in range(0, slots, DECODE_MERGE):
        group = list(range(s0, s0 + DECODE_MERGE))
        for s in group:
            wait(s)
        consume(group)

        @pl.when(d + 1 < n_dsteps)
        def _():
            for s in group:
                start(s, first + (slots + s) * pp)

    if part == n_parts - 1:
        carry = fcar_ref[...]
        sn = _dot(qbd, kn_ref[seq].astype(F32))
        bias = jnp.concatenate([negc_ref[seq] - carry] * n_new, axis=0)
        r2 = lax.broadcasted_iota(jnp.int32, sn.shape, 0)
        c2 = lax.broadcasted_iota(jnp.int32, sn.shape, 1)
        ok = (c2 < n_new) & (c2 <= r2 // n_heads)
        update(jnp.where(ok, sn + bias, MASKED), [vn_ref[seq].astype(F32)])
        out = jnp.where(own, acc_ref[...] / l_ref[...], 0.0)
        o_ref[seq] = jnp.sum(out.reshape(n_new, n_heads, width), axis=1)


N_DECODE_IN = 8


def _decode_hook(pt_ref, dec_in, o_ref, scratch, tile, n_tiles, n_parts, n_heads, n_new):
    refs = (pt_ref,) + tuple(dec_in) + (o_ref,) + tuple(scratch)
    per_tile = o_ref.shape[0] * n_parts
    n_dsteps = n_tiles * per_tile

    @pl.when(tile == 0)
    def _():
        _decode_prologue(refs)

    done = [0]

    def after_chunk(c, n_chunks):
        upto = ((c + 1) * per_tile) // n_chunks
        for k in range(done[0], upto):
            _decode_step(tile * per_tile + k, n_dsteps, k // n_parts, k % n_parts, n_parts, refs,
                         n_heads, n_new)
        done[0] = upto

    return after_chunk


def _decode_operands(page_table, q, kn_t, vn_t, negc, upper, cache_kt, cache_vt, cache_lft, n_tiles,
                     flat_tile, n_heads):
    bd, n_new, width = q.shape
    n_pages = page_table.shape[1]
    page = cache_kt.shape[2]
    slots = DECODE_SLOTS
    pp = min(DECODE_PAGES, n_pages // slots)
    n_parts = n_pages // (slots * pp)
    assert n_pages == n_parts * slots * pp and (bd * n_parts) % n_tiles == 0
    per_tile = bd * n_parts // n_tiles
    assert per_tile % n_parts == 0
    seqs = per_tile // n_parts
    rows = n_new * n_heads
    seq = lambda shape: pl.BlockSpec((seqs,) + shape, lambda *idx: (flat_tile(*idx[:-1]), 0, 0))
    hbm = pl.BlockSpec(memory_space=pl.ANY)
    return {
        "n_parts": n_parts,
        "page_table": page_table.reshape(-1),
        "inputs": [q, kn_t, vn_t, negc, upper, cache_kt, cache_vt, cache_lft],
        "in_specs": [seq((n_new, width)), seq((width, page)), seq((width, page)),
                     seq((n_heads, LANES)), _const_spec(upper.shape), hbm, hbm, hbm],
        "out_spec": seq((n_new, width)),
        "out_shape": jax.ShapeDtypeStruct((bd, n_new, width), F32),
        "scratch": [pltpu.VMEM((slots, width, pp * page), F32),
                    pltpu.VMEM((slots, width, pp * page), F32),
                    pltpu.VMEM((slots, pp, n_heads, page), F32),
                    pltpu.SemaphoreType.DMA((slots, 3)),
                    pltpu.VMEM((rows, 1), F32), pltpu.VMEM((rows, 1), F32),
                    pltpu.VMEM((rows, width), F32), pltpu.VMEM((n_heads, LANES), F32)],
    }


def _ffn_decode_kernel(pt_ref, x_ref, g_ref, wg_ref, wu_ref, wd_ref, *refs, n_heads, n_new, n_parts):
    dec_in, (h_ref, o_ref), scratch = refs[:N_DECODE_IN], refs[N_DECODE_IN:N_DECODE_IN + 2], \
        refs[N_DECODE_IN + 2:]
    hook = _decode_hook(pt_ref, dec_in, o_ref, scratch, pl.program_id(0), pl.num_programs(0), n_parts,
                        n_heads, n_new)
    h_ref[...] = _ffn_value(x_ref[...], g_ref[...], wg_ref, wu_ref, wd_ref, hook)


def _ffn_decode(x, g, wg, wu, wd, tm, n_heads, *decode_args):
    n, d = x.shape
    dec = _decode_operands(*decode_args, n // tm, lambda i: i, n_heads)
    tok = pl.BlockSpec((tm, d), lambda i, pt_ref: (i, 0))
    grid_spec = pltpu.PrefetchScalarGridSpec(
        num_scalar_prefetch=1,
        grid=(n // tm,),
        in_specs=[tok, _const_spec(g.shape), _const_spec(wg.shape), _const_spec(wu.shape),
                  _const_spec(wd.shape)] + dec["in_specs"],
        out_specs=(tok, dec["out_spec"]),
        scratch_shapes=dec["scratch"],
    )
    n_new = dec["out_shape"].shape[1]
    return pl.pallas_call(
        functools.partial(_ffn_decode_kernel, n_heads=n_heads, n_new=n_new, n_parts=dec["n_parts"]),
        grid_spec=grid_spec,
        out_shape=(jax.ShapeDtypeStruct((n, d), F32), dec["out_shape"]),
        compiler_params=_params(("arbitrary",)),
        name="ffn_decode",
    )(dec["page_table"], x, g, wg, wu, wd, *dec["inputs"])


N_POST_IN = 15


def _post_value(h_ref, gpool_ref, sga_ref, attn_ref, p_ref, wab_ref, wout_ref, g2_ref, wg_ref,
                wu_ref, wd_ref, gple_ref, wpg_ref, wple_ref, gfin_ref, attn_transposed, after_chunk):
    if attn_transposed:
        branch_attn = lax.dot_general(attn_ref[...], wab_ref[...], _TN, preferred_element_type=F32)
    else:
        branch_attn = _dot(attn_ref[...].astype(BF16), wab_ref[...])
    merged = gpool_ref[...].astype(F32) + sga_ref[...].astype(F32) * branch_attn
    h = h_ref[...] + _dot(merged.astype(BF16), wout_ref[...])
    h = _ffn_value(h, g2_ref[...], wg_ref, wu_ref, wd_ref, after_chunk)
    gate = jax.nn.sigmoid(_dot(_rms(h, gple_ref[...]).astype(BF16), wpg_ref[...]))
    h = h + _dot(p_ref[...].astype(BF16), wple_ref[...]) * gate
    return _rms(h, gfin_ref[...])


def _post_kernel(*refs, attn_transposed):
    y_ref = refs[N_POST_IN]
    y_ref[...] = _post_value(*refs[:N_POST_IN], attn_transposed, None)


def _post_decode_kernel(pt_ref, *refs, attn_transposed, n_heads, n_new, n_parts):
    ins, refs = refs[:N_POST_IN], refs[N_POST_IN:]
    dec_in, (y_ref, o_ref), scratch = refs[:N_DECODE_IN], refs[N_DECODE_IN:N_DECODE_IN + 2], \
        refs[N_DECODE_IN + 2:]
    tile = pl.program_id(0) * pl.num_programs(1) + pl.program_id(1)
    hook = _decode_hook(pt_ref, dec_in, o_ref, scratch, tile, pl.num_programs(0) * pl.num_programs(1),
                        n_parts, n_heads, n_new)
    y_ref[...] = _post_value(*ins, attn_transposed, hook)


def _post(h, gpool, sga, attn, p, w, tm, attn_transposed, n_heads=None, decode_args=None):
    b, t, d = h.shape
    nt = t // tm
    names = ["wab2" if attn_transposed else "wab", "wout", "g_ffn2", "wg2", "wu2", "wd2", "g_ple",
             "wpgate", "wple", "g_final"]
    consts = [w[n] for n in names]
    tok = lambda width: pl.BlockSpec((None, tm, width), lambda bi, i, *_: (bi, i, 0))
    if attn_transposed:
        attn_spec = pl.BlockSpec((None, attn.shape[1], tm), lambda bi, i, *_: (bi, 0, i))
    else:
        attn_spec = tok(attn.shape[2])
    in_specs = [tok(d), tok(d), tok(d), attn_spec, tok(p.shape[2])] + [_const_spec(c.shape) for c in consts]
    y_shape = jax.ShapeDtypeStruct((b, t, d), F32)
    if decode_args is None:
        return pl.pallas_call(
            functools.partial(_post_kernel, attn_transposed=attn_transposed),
            grid=(b, nt),
            in_specs=in_specs,
            out_specs=tok(d),
            out_shape=y_shape,
            compiler_params=_params(("parallel", "parallel")),
            name="post",
        )(h, gpool, sga, attn, p, *consts)
    dec = _decode_operands(*decode_args, b * nt, lambda bi, i: bi * nt + i, n_heads)
    grid_spec = pltpu.PrefetchScalarGridSpec(
        num_scalar_prefetch=1,
        grid=(b, nt),
        in_specs=in_specs + dec["in_specs"],
        out_specs=(tok(d), dec["out_spec"]),
        scratch_shapes=dec["scratch"],
    )
    return pl.pallas_call(
        functools.partial(_post_decode_kernel, attn_transposed=attn_transposed, n_heads=n_heads,
                          n_new=dec["out_shape"].shape[1], n_parts=dec["n_parts"]),
        grid_spec=grid_spec,
        out_shape=(y_shape, dec["out_shape"]),
        compiler_params=_params(("arbitrary", "arbitrary"), POST_DECODE_VMEM_LIMIT),
        name="post_decode",
    )(dec["page_table"], h, gpool, sga, attn, p, *consts, *dec["inputs"])


def _prep_weights(g_ffn1, w_ffn1_gate, w_ffn1_up, w_ffn1_down, g_mix, w_in, b_forget, w_pool_group,
                  pool_scale, w_attn_branch, w_out, g_ffn2, w_ffn2_gate, w_ffn2_up, w_ffn2_down,
                  g_ple, w_ple_gate, w_ple, g_final, n_heads, head_dim, pool_width, tm, order):
    d = w_in.shape[0]
    aw = n_heads * head_dim
    o = 0
    wpool = w_in[:, o:o + pool_width]; o += pool_width
    wq = w_in[:, o:o + aw] * (head_dim ** -0.5); o += aw
    wk = w_in[:, o:o + aw]; o += aw
    wv = w_in[:, o:o + aw]; o += aw
    wf = w_in[:, o:o + n_heads]; o += n_heads
    wgp = w_in[:, o:o + d]; o += d
    wga = w_in[:, o:o + d]

    qones = np.zeros((1, HEAD_PAD), np.float32)
    qones[0, head_dim:head_dim + F_PARTS] = 1.0
    headsel = np.zeros((aw, LANES), np.float32)
    for h in range(n_heads):
        headsel[h * head_dim:(h + 1) * head_dim, h] = 1.0
    parts, slots = np.arange(F_PARTS)[:, None], np.arange(n_heads)[None, :]
    place = jnp.zeros((LANES, n_heads * HEAD_PAD), F32).at[
        parts * n_heads + order[None, :], slots * HEAD_PAD + head_dim + parts].set(-1.0)
    by_slot_cols = lambda x: x.reshape(d, n_heads, head_dim)[:, order].reshape(d, aw)
    by_slot_rows = lambda x: x.reshape(n_heads, head_dim, -1)[order].reshape(aw, -1)
    row = lambda x: x.reshape(1, -1).astype(F32)
    bf = lambda x: x.astype(BF16)
    return {
        "g_ffn1": row(g_ffn1), "wg1": bf(w_ffn1_gate), "wu1": bf(w_ffn1_up), "wd1": bf(w_ffn1_down),
        "g_mix": row(g_mix), "wpool": bf(wpool), "wq": bf(wq), "wk": bf(wk), "wv": bf(wv),
        "wq2": bf(by_slot_cols(wq * LOG2E)), "wk2": bf(by_slot_cols(wk)),
        "wab2": bf(by_slot_rows(w_attn_branch)),
        "wkt": bf(wk.T), "wvt": bf(wv.T),
        "wf": bf(jnp.pad(wf, ((0, 0), (0, LANES - n_heads)))),
        "bf": jnp.pad(row(b_forget), ((0, 0), (0, LANES - n_heads))),
        "wgp": bf(wgp), "wga": bf(wga), "wpg": bf(w_pool_group), "pscale": row(pool_scale),
        "place": bf(place), "qones": jnp.asarray(qones, F32),
        "headsel": jnp.asarray(headsel, BF16),
        "tri": jnp.asarray(np.tril(np.ones((tm, tm), np.float32)), BF16),
        "wab": bf(w_attn_branch), "wout": bf(w_out),
        "g_ffn2": row(g_ffn2), "wg2": bf(w_ffn2_gate), "wu2": bf(w_ffn2_up), "wd2": bf(w_ffn2_down),
        "g_ple": row(g_ple), "wpgate": bf(w_ple_gate), "wple": bf(w_ple), "g_final": row(g_final),
    }


def kernel(x_prompt, x_sample, cache_k, cache_v, cache_logf, state_pool, page_table, p_prompt, p_sample, g_ffn1, w_ffn1_gate, w_ffn1_up, w_ffn1_down, g_mix, w_in, b_forget, w_pool_group, pool_scale, w_attn_branch, w_out, g_ffn2, w_ffn2_gate, w_ffn2_up, w_ffn2_down, g_ple, w_ple_gate, w_ple, g_final):
    depth = cache_k.shape[0]
    assert depth == 1, "one trunk layer"
    b, t, d = x_prompt.shape
    bd, n_new, _ = x_sample.shape
    _, n_pool_pages, page, n_heads, head_dim = cache_k.shape
    aw = n_heads * head_dim
    n_pre, pool_width = state_pool.shape[2], state_pool.shape[3]
    assert head_dim + F_PARTS <= HEAD_PAD and head_dim < V_ROWS and page == LANES
    assert pool_width == len(POOL_WINDOWS) * LANES and n_pre == POOL_WINDOWS[-1] - 1
    assert n_heads % ATTN_HEADS == 0
    tm = min(TOKEN_TILE, t)
    assert t // tm <= LANES
    ns = bd * n_new

    order = jnp.argsort(b_forget[0]).astype(jnp.int32)
    w = _prep_weights(g_ffn1[0], w_ffn1_gate[0], w_ffn1_up[0], w_ffn1_down[0], g_mix[0], w_in[0],
                      b_forget[0], w_pool_group[0], pool_scale[0], w_attn_branch[0], w_out[0],
                      g_ffn2[0], w_ffn2_gate[0], w_ffn2_up[0], w_ffn2_down[0], g_ple[0],
                      w_ple_gate[0], w_ple[0], g_final, n_heads, head_dim, pool_width, tm, order)

    step_major = lambda x: jnp.swapaxes(x, 0, 1).reshape(ns, x.shape[-1])
    seq_major = lambda x: jnp.swapaxes(x.reshape(n_new, bd, x.shape[-1]), 0, 1)
    hs1 = _ffn(step_major(x_sample), w["g_ffn1"], w["wg1"], w["wu1"], w["wd1"], ns)
    state = jnp.swapaxes(state_pool[0], 0, 1).reshape(n_pre * bd, pool_width)
    z_s, q_s, k_s, v_s, lf_s, fc_s, gpool_s, sga_s = _proj_sample(hs1, state, w, n_heads, n_new, bd)
    k_s, v_s, lf_s = seq_major(k_s), seq_major(v_s), seq_major(lf_s)
    new_page = lambda x: jnp.pad(jnp.swapaxes(x, 1, 2).astype(BF16), ((0, 0), (0, 0), (0, page - n_new)))
    negc = -jnp.swapaxes(seq_major(fc_s), 1, 2)
    negc = jnp.pad(negc, ((0, 0), (0, 0), (0, LANES - n_new)))
    upper = jnp.asarray(np.triu(np.ones((page, page), np.float32)), BF16)
    pages_t = lambda c: jnp.transpose(c[0], (0, 2, 3, 1)).reshape(n_pool_pages, aw, page)

    dec_args = (page_table, seq_major(q_s), new_page(k_s), new_page(v_s), negc)
    caches = (upper, pages_t(cache_k), pages_t(cache_v), jnp.swapaxes(cache_logf[0], 1, 2))
    half = bd // 2
    h1, attn_a = _ffn_decode(x_prompt.reshape(b * t, d), w["g_ffn1"], w["wg1"], w["wu1"], w["wd1"], tm,
                             n_heads, *(a[:half] for a in dec_args), *caches)
    h1 = h1.reshape(b, t, d)
    qa, ka, vt, kt_p, vt_p, lf_p, gpool, sga, zlast, qn, kn, fe = _proj_prompt(h1, w, tm, n_heads,
                                                                               order)
    stat = lambda x: jnp.swapaxes(x[:, :, 0, :n_heads], 1, 2).reshape(
        b, n_heads // ATTN_HEADS, ATTN_HEADS, t // tm)
    attn_t = _attention(qa, ka, vt, stat(qn), stat(kn), stat(fe[..., order]), head_dim, tm)
    y_prompt, attn_b = _post(h1, gpool, sga, attn_t, p_prompt[0], w, tm, True, n_heads,
                             tuple(a[half:] for a in dec_args) + caches)
    heads_last = lambda x: jnp.transpose(x.reshape(b, n_heads, head_dim, t), (0, 3, 1, 2))[None]

    attn_s = jnp.concatenate([attn_a, attn_b], axis=0)
    y_s = _post(hs1[None], gpool_s[None], sga_s[None], step_major(attn_s)[None],
                step_major(p_sample[0])[None], w, ns, False)
    y_sample = seq_major(y_s[0])
    pool_sample = jnp.concatenate([state_pool[0], seq_major(z_s)], axis=1)[:, -n_pre:]

    return (y_prompt, y_sample, heads_last(kt_p), heads_last(vt_p),
            lf_p[None], zlast[None, :, POOL_HALO - n_pre:],
            k_s.reshape(1, bd, n_new, n_heads, head_dim), v_s.reshape(1, bd, n_new, n_heads, head_dim),
            lf_s[None], pool_sample[None])
```
